```python
import jax, jax.numpy as jnp
from jax import lax
import numpy as np

D_MODEL = 4096
BATCH = 2
SEQ = 8192
DEPTH = 1

MEM_LEN = 256
RMS_EPS = 1e-6

GLA_HEADS = 8
GLA_DK = 256
GLA_DV = 512
GLA_RANK = 16
GLA_TAU = 16.0
GLA_CHUNK = 64
GLA_QK = GLA_HEADS * GLA_DK
GLA_V = GLA_HEADS * GLA_DV

SWA_HEADS = 64
SWA_KV_HEADS = 8
SWA_GROUP = SWA_HEADS // SWA_KV_HEADS
SWA_HD = 64
SWA_WINDOW = 128
SWA_BLOCK = 128
SWA_Q = SWA_HEADS * SWA_HD
SWA_KV = SWA_KV_HEADS * SWA_HD

MEM_HEADS = 4
MEM_HD = D_MODEL // MEM_HEADS
MEM_Q = MEM_HEADS * MEM_HD

N_BRANCH = 3

PEER_HEADS = 8
PEER_NKEYS = 128
PEER_N = PEER_NKEYS * PEER_NKEYS
PEER_DQ = 256
PEER_TOPK = 16
PEER_CHUNK = 128

SPLITS = (GLA_QK, GLA_QK, GLA_V, GLA_V, GLA_RANK, SWA_Q, SWA_KV, SWA_KV, MEM_Q, N_BRANCH * D_MODEL)
D_IN = GLA_QK + GLA_QK + GLA_V + GLA_V + GLA_RANK + SWA_Q + SWA_KV + SWA_KV + MEM_Q + N_BRANCH * D_MODEL

kernel_name = 'hybrid_gla_swa_mem_peer_layer'


def rms_norm(x, gain):
    xf = x.astype(jnp.float32)
    y = xf * lax.rsqrt(jnp.mean(xf * xf, axis=-1, keepdims=True) + RMS_EPS)
    return (y * gain.astype(jnp.float32)).astype(x.dtype)


def gla_chunked(q, k, v, log_a):
    B, S = q.shape[0], q.shape[1]
    nc = S // GLA_CHUNK

    def to_chunks(t):
        return t.reshape(B, nc, GLA_CHUNK, t.shape[2], t.shape[3]).transpose(1, 0, 3, 2, 4)

    qc, kc, vc, ac = to_chunks(q), to_chunks(k), to_chunks(v), to_chunks(log_a)
    causal = jnp.tril(jnp.ones((GLA_CHUNK, GLA_CHUNK), dtype=bool))

    def step(state, inp):
        qi, ki, vi, ai = inp
        qf, kf, vf = qi.astype(jnp.float32), ki.astype(jnp.float32), vi.astype(jnp.float32)
        b = jnp.cumsum(ai.astype(jnp.float32), axis=2)
        b_last = b[:, :, -1:, :]
        o_inter = jnp.einsum('bhcd,bhde->bhce', qf * jnp.exp(b), state)
        diff = b[:, :, :, None, :] - b[:, :, None, :, :]
        decay = jnp.exp(jnp.where(causal[None, None, :, :, None], diff, -jnp.inf))
        attn = jnp.einsum('bhijd,bhjd->bhij', qf[:, :, :, None, :] * decay, kf)
        o = o_inter + jnp.einsum('bhij,bhje->bhie', attn, vf)
        k_dec = kf * jnp.exp(b_last - b)
        state = jnp.exp(b_last[:, :, 0, :])[..., None] * state + jnp.einsum('bhcd,bhce->bhde', k_dec, vf)
        return state, o

    state0 = jnp.zeros((B, GLA_HEADS, GLA_DK, GLA_DV), jnp.float32)
    _, o = lax.scan(step, state0, (qc, kc, vc, ac))
    return o.transpose(1, 0, 3, 2, 4).reshape(B, S, GLA_HEADS, GLA_DV).astype(v.dtype)


def sliding_window_attention(q, k, v, sinks):
    B, S = q.shape[0], q.shape[1]
    nb = S // SWA_BLOCK
    qb = q.reshape(B, nb, SWA_BLOCK, SWA_KV_HEADS, SWA_GROUP, SWA_HD)
    kb = k.reshape(B, nb, SWA_BLOCK, SWA_KV_HEADS, SWA_HD)
    vb = v.reshape(B, nb, SWA_BLOCK, SWA_KV_HEADS, SWA_HD)

    def with_prev(t):
        prev = jnp.concatenate([jnp.zeros_like(t[:, :1]), t[:, :-1]], axis=1)
        return jnp.concatenate([prev, t], axis=2)

    kw, vw = with_prev(kb), with_prev(vb)
    q_loc = jnp.arange(SWA_BLOCK)[:, None] + SWA_BLOCK
    k_loc = jnp.arange(2 * SWA_BLOCK)[None, :]
    band = (k_loc <= q_loc) & (q_loc - k_loc < SWA_WINDOW)
    sink = sinks.astype(jnp.float32).reshape(SWA_KV_HEADS, SWA_GROUP)[None, :, :, None, None]
    scale = SWA_HD ** -0.5

    def block_attn(args):
        qx, kx, vx, blk = args
        s = jnp.einsum('bqhgd,bkhd->bhgqk', qx, kx).astype(jnp.float32) * scale
        mask = band & (k_loc + (blk - 1) * SWA_BLOCK >= 0)
        s = jnp.where(mask, s, -jnp.inf)
        m = jnp.maximum(jnp.max(s, axis=-1, keepdims=True), sink)
        p = jnp.exp(s - m)
        denom = jnp.sum(p, axis=-1, keepdims=True) + jnp.exp(sink - m)
        return jnp.einsum('bhgqk,bkhd->bqhgd', (p / denom).astype(vx.dtype), vx)

    o = lax.map(block_attn, (jnp.moveaxis(qb, 1, 0), jnp.moveaxis(kw, 1, 0), jnp.moveaxis(vw, 1, 0), jnp.arange(nb)))
    return jnp.moveaxis(o, 0, 1).reshape(B, S, SWA_Q)


def memory_cross_attention(q, k, v):
    B, S = q.shape[0], q.shape[1]
    s = jnp.einsum('bshd,bmhd->bhsm', q, k).astype(jnp.float32) * (MEM_HD ** -0.5)
    p = jax.nn.softmax(s, axis=-1).astype(v.dtype)
    return jnp.einsum('bhsm,bmhd->bshd', p, v).reshape(B, S, MEM_Q)


def mixer_block(x, mem, norm_mix, w_in, gla_w_gate_up, gla_gate_bias, gla_out_norm,
                swa_q_norm, swa_k_norm, swa_sinks, mem_norm, w_mem_kv, mem_q_norm, mem_k_norm,
                w_branch_gla, w_branch_swa, w_branch_mem, w_out):
    B, S = x.shape[0], x.shape[1]
    xn = rms_norm(x, norm_mix)
    proj = xn @ w_in
    offsets = np.cumsum(SPLITS)[:-1].tolist()
    gq, gk, gv, gr, glr, sq, sk, sv, mq, gate_logits = jnp.split(proj, offsets, axis=-1)

    q_a = gq.reshape(B, S, GLA_HEADS, GLA_DK) * (GLA_DK ** -0.5)
    k_a = gk.reshape(B, S, GLA_HEADS, GLA_DK)
    v_a = gv.reshape(B, S, GLA_HEADS, GLA_DV)
    log_a = jax.nn.log_sigmoid((glr @ gla_w_gate_up + gla_gate_bias).astype(jnp.float32)) / GLA_TAU
    o_a = gla_chunked(q_a, k_a, v_a, log_a.reshape(B, S, GLA_HEADS, GLA_DK))
    o_a = rms_norm(o_a, gla_out_norm) * jax.nn.silu(gr.reshape(B, S, GLA_HEADS, GLA_DV))
    o_a = o_a.reshape(B, S, GLA_V)

    q_b = rms_norm(sq.reshape(B, S, SWA_HEADS, SWA_HD), swa_q_norm).reshape(B, S, SWA_KV_HEADS, SWA_GROUP, SWA_HD)
    k_b = rms_norm(sk.reshape(B, S, SWA_KV_HEADS, SWA_HD), swa_k_norm)
    v_b = sv.reshape(B, S, SWA_KV_HEADS, SWA_HD)
    o_b = sliding_window_attention(q_b, k_b, v_b, swa_sinks)

    kv_m = rms_norm(mem, mem_norm) @ w_mem_kv
    M = mem.shape[1]
    k_m, v_m = jnp.split(kv_m, 2, axis=-1)
    q_c = rms_norm(mq.reshape(B, S, MEM_HEADS, MEM_HD), mem_q_norm)
    k_c = rms_norm(k_m.reshape(B, M, MEM_HEADS, MEM_HD), mem_k_norm)
    o_c = memory_cross_attention(q_c, k_c, v_m.reshape(B, M, MEM_HEADS, MEM_HD))

    g = jax.nn.sigmoid(gate_logits.astype(jnp.float32)).astype(x.dtype).reshape(B, S, N_BRANCH, D_MODEL)
    mix = g[:, :, 0] * (o_a @ w_branch_gla) + g[:, :, 1] * (o_b @ w_branch_swa) + g[:, :, 2] * (o_c @ w_branch_mem)
    return x + mix @ w_out


def peer_ffn(x, w_q, sub_keys, u_table, v_table):
    B, S, D = x.shape
    T = B * S
    xt = x.reshape(T, D)
    q = (xt @ w_q).reshape(T, PEER_HEADS, 2, PEER_DQ // 2)
    s = jnp.einsum('thpd,hpnd->thpn', q, sub_keys).astype(jnp.float32)
    top_s, top_i = lax.top_k(s, PEER_TOPK)
    cand = top_s[:, :, 0, :, None] + top_s[:, :, 1, None, :]
    best_s, best_c = lax.top_k(cand.reshape(T, PEER_HEADS, PEER_TOPK * PEER_TOPK), PEER_TOPK)
    i1 = jnp.take_along_axis(top_i[:, :, 0], best_c // PEER_TOPK, axis=-1)
    i2 = jnp.take_along_axis(top_i[:, :, 1], best_c % PEER_TOPK, axis=-1)
    expert = i1 * PEER_NKEYS + i2
    gate = jax.nn.softmax(best_s, axis=-1).astype(x.dtype)
    nchunk = T // PEER_CHUNK

    def chunk_fn(args):
        xc, ec, gc = args
        act = jax.nn.gelu(jnp.einsum('chkd,cd->chk', u_table[ec], xc), approximate=False)
        return jnp.einsum('chk,chkd->cd', gc * act, v_table[ec])

    y = lax.map(chunk_fn, (xt.reshape(nchunk, PEER_CHUNK, D),
                           expert.reshape(nchunk, PEER_CHUNK, PEER_HEADS, PEER_TOPK),
                           gate.reshape(nchunk, PEER_CHUNK, PEER_HEADS, PEER_TOPK)))
    return y.reshape(B, S, D)


def setup_inputs(seed: int = 0) -> dict:
    key = jax.random.key(seed)
    ks = jax.random.split(key, 24)
    L = DEPTH
    f32 = jnp.float32

    def dense(k, shape, fan_in):
        return jax.random.normal(k, shape, f32) * (fan_in ** -0.5)

    def gain(k, shape):
        return 1.0 + 0.02 * jax.random.normal(k, shape, f32)

    return {
        'x': jax.random.normal(ks[0], (BATCH, SEQ, D_MODEL), f32),
        'mem': jax.random.normal(ks[1], (BATCH, MEM_LEN, D_MODEL), f32),
        'norm_mix': gain(ks[2], (L, D_MODEL)),
        'w_in': dense(ks[3], (L, D_MODEL, D_IN), D_MODEL),
        'gla_w_gate_up': dense(ks[4], (L, GLA_RANK, GLA_QK), GLA_RANK),
        'gla_gate_bias': 0.1 * jax.random.normal(ks[5], (L, GLA_QK), f32),
        'gla_out_norm': gain(ks[6], (L, GLA_HEADS, GLA_DV)),
        'swa_q_norm': gain(ks[7], (L, SWA_HD)),
        'swa_k_norm': gain(ks[8], (L, SWA_HD)),
        'swa_sinks': 0.5 * jax.random.normal(ks[9], (L, SWA_HEADS), f32),
        'mem_norm': gain(ks[10], (L, D_MODEL)),
        'w_mem_kv': dense(ks[11], (L, D_MODEL, 2 * MEM_Q), D_MODEL),
        'mem_q_norm': gain(ks[12], (L, MEM_HD)),
        'mem_k_norm': gain(ks[13], (L, MEM_HD)),
        'w_branch_gla': dense(ks[14], (L, GLA_V, D_MODEL), GLA_V),
        'w_branch_swa': dense(ks[15], (L, SWA_Q, D_MODEL), SWA_Q),
        'w_branch_mem': dense(ks[16], (L, MEM_Q, D_MODEL), MEM_Q),
        'w_out': dense(ks[17], (L, D_MODEL, D_MODEL), D_MODEL),
        'norm_ffn': gain(ks[18], (L, D_MODEL)),
        'peer_w_q': dense(ks[19], (L, D_MODEL, PEER_HEADS * PEER_DQ), D_MODEL),
        'peer_sub_keys': dense(ks[20], (L, PEER_HEADS, 2, PEER_NKEYS, PEER_DQ // 2), PEER_DQ // 2),
        'peer_u': dense(ks[21], (L, PEER_N, D_MODEL), D_MODEL),
        'peer_v': 0.5 * jax.random.normal(ks[22], (L, PEER_N, D_MODEL), f32),
    }


def reference(x, mem, norm_mix, w_in, gla_w_gate_up, gla_gate_bias, gla_out_norm,
              swa_q_norm, swa_k_norm, swa_sinks, mem_norm, w_mem_kv, mem_q_norm, mem_k_norm,
              w_branch_gla, w_branch_swa, w_branch_mem, w_out, norm_ffn,
              peer_w_q, peer_sub_keys, peer_u, peer_v):
    h = x
    for l in range(DEPTH):
        h = mixer_block(h, mem, norm_mix[l], w_in[l], gla_w_gate_up[l], gla_gate_bias[l], gla_out_norm[l],
                        swa_q_norm[l], swa_k_norm[l], swa_sinks[l], mem_norm[l], w_mem_kv[l],
                        mem_q_norm[l], mem_k_norm[l], w_branch_gla[l], w_branch_swa[l], w_branch_mem[l], w_out[l])
        h = h + peer_ffn(rms_norm(h, norm_ffn[l]), peer_w_q[l], peer_sub_keys[l], peer_u[l], peer_v[l])
    return h
```

```python
import functools

import jax
import jax.numpy as jnp
from jax import lax
from jax.experimental import pallas as pl
from jax.experimental.pallas import tpu as pltpu

_F32 = jnp.float32
_BF16 = jnp.bfloat16
_NEG_INF = float("-inf")

RMS_EPS = 1e-6
GLA_HEADS = 8
GLA_DK = 256
GLA_DV = 512
GLA_RANK = 16
GLA_TAU = 16.0
GLA_SUB = 16
SWA_HEADS = 64
SWA_KV_HEADS = 8
SWA_HD = 64
SWA_BLOCK = 128
MEM_HEADS = 4
PEER_HEADS = 8
PEER_NKEYS = 128
PEER_DQ = 256
PEER_TOPK = 16

LANES = 128
V7X_VMEM_BYTES = 64 * 1024 * 1024


def _tile(n, pref):
    t = min(n, pref)
    while n % t:
        t -= 1
    return t


def _params(semantics, vmem_bytes):
    limit = min(int(vmem_bytes) + (8 << 20), V7X_VMEM_BYTES - (4 << 20))
    return pltpu.CompilerParams(dimension_semantics=semantics, vmem_limit_bytes=limit)


def _nt(a, b):
    return lax.dot_general(a, b, (((1,), (1,)), ((), ())), preferred_element_type=_F32)


def _log2(n):
    assert n > 0 and n & (n - 1) == 0, n
    return n.bit_length() - 1


def _tn(a, b):
    return lax.dot_general(a, b, (((0,), (0,)), ((), ())), preferred_element_type=_F32)


def _rmsnorm_kernel(x_ref, g_ref, o_ref):
    x = x_ref[...].astype(_F32)
    ms = jnp.mean(x * x, axis=-1, keepdims=True)
    o_ref[...] = (x * lax.rsqrt(ms + RMS_EPS) * g_ref[...]).astype(o_ref.dtype)


def _rmsnorm(x, gain, out_dtype, rows=256):
    r, d = x.shape
    tr = _tile(r, rows)
    blk = tr * d * (x.dtype.itemsize + jnp.dtype(out_dtype).itemsize)
    return pl.pallas_call(
        _rmsnorm_kernel,
        grid=(r // tr,),
        in_specs=[pl.BlockSpec((tr, d), lambda i: (i, 0)), pl.BlockSpec((1, d), lambda i: (0, 0))],
        out_specs=pl.BlockSpec((tr, d), lambda i: (i, 0)),
        out_shape=jax.ShapeDtypeStruct((r, d), out_dtype),
        compiler_params=_params(("parallel",), 2 * blk + 4 * tr * d * 4),
        name="rmsnorm",
    )(x, gain.reshape(1, d).astype(_F32))


def _mm_kernel(x_ref, w_ref, o_ref):
    o_ref[...] = jnp.dot(x_ref[...], w_ref[...], preferred_element_type=_F32).astype(o_ref.dtype)


def _mm_res_kernel(x_ref, w_ref, r_ref, o_ref):
    acc = jnp.dot(x_ref[...], w_ref[...], preferred_element_type=_F32)
    o_ref[...] = (acc + r_ref[...].astype(_F32)).astype(o_ref.dtype)


def _matmul(x, w, out_dtype, tm=1024, tn=512, residual=None, name="matmul"):
    m, k = x.shape
    n = w.shape[1]
    tm, tn = _tile(m, tm), _tile(n, tn)
    osz = jnp.dtype(out_dtype).itemsize
    vmem = 2 * (tm * k * x.dtype.itemsize + k * tn * w.dtype.itemsize + tm * tn * osz) + tm * tn * 4
    in_specs = [pl.BlockSpec((tm, k), lambda j, i: (i, 0)), pl.BlockSpec((k, tn), lambda j, i: (0, j))]
    args = [x, w]
    kern = _mm_kernel
    if residual is not None:
        in_specs.append(pl.BlockSpec((tm, tn), lambda j, i: (i, j)))
        args.append(residual)
        kern = _mm_res_kernel
        vmem += 2 * tm * tn * residual.dtype.itemsize
    return pl.pallas_call(
        kern,
        grid=(n // tn, m // tm),
        in_specs=in_specs,
        out_specs=pl.BlockSpec((tm, tn), lambda j, i: (i, j)),
        out_shape=jax.ShapeDtypeStruct((m, n), out_dtype),
        compiler_params=_params(("parallel", "parallel"), vmem),
        name=name,
    )(*args)


def _log_sigmoid(z):
    return jnp.minimum(z, 0.0) - jnp.log(1.0 + jnp.exp(-jnp.abs(z)))


def _gla_kernel(q_ref, k_ref, v_ref, r_ref, glr_ref, wg_ref, bias_ref, gain_ref, o_ref, state_ref):
    c = q_ref.shape[0]

    @pl.when(pl.program_id(2) == 0)
    def _():
        state_ref[...] = jnp.zeros_like(state_ref)

    q = q_ref[...].astype(_F32) * (GLA_DK ** -0.5)
    k = k_ref[...].astype(_F32)
    v = v_ref[...]
    z = jnp.dot(glr_ref[...], wg_ref[...], preferred_element_type=_F32,
                precision=lax.Precision.HIGHEST) + bias_ref[...]
    log_a = _log_sigmoid(z) * (1.0 / GLA_TAU)

    row = lax.broadcasted_iota(jnp.int32, (c, c), 0)
    col = lax.broadcasted_iota(jnp.int32, (c, c), 1)
    tri = (row >= col).astype(_F32)
    b = jnp.dot(tri, log_a, preferred_element_type=_F32, precision=lax.Precision.HIGHEST)
    b_last = b[c - 1:c, :]

    attn = jnp.zeros((c, c), _F32)
    row1 = lax.broadcasted_iota(jnp.int32, (c, 1), 0)
    s = c // 2
    while s >= GLA_SUB:
        pieces = []
        for lo in range(0, c, 2 * s):
            mid = lo + s
            ref_row = b[mid - 1:mid, :]
            pieces.append(ref_row - b[lo:mid, :])
            pieces.append(b[mid:mid + s, :] - ref_row)
        x = jnp.exp(jnp.concatenate(pieces, axis=0))
        upper = ((row1 >> _log2(s)) & 1) == 1
        qt = jnp.where(upper, q * x, 0.0).astype(_BF16)
        kt = jnp.where(upper, 0.0, k * x).astype(_BF16)
        same = (row >> _log2(2 * s)) == (col >> _log2(2 * s))
        attn = attn + jnp.where(same, _nt(qt, kt), 0.0)
        s //= 2

    sub_row = lax.broadcasted_iota(jnp.int32, (GLA_SUB, 1), 0)
    lane = lax.broadcasted_iota(jnp.int32, (GLA_SUB, c), 1)
    diag = []
    for i0 in range(0, c, GLA_SUB):
        qi, ki, bi = q[i0:i0 + GLA_SUB, :], k[i0:i0 + GLA_SUB, :], b[i0:i0 + GLA_SUB, :]
        blk = jnp.zeros((GLA_SUB, c), _F32)
        for j in range(GLA_SUB):
            dec = jnp.exp(jnp.where(sub_row >= j, bi - bi[j:j + 1, :], _NEG_INF))
            sc = jnp.sum(qi * dec * ki[j:j + 1, :], axis=1, keepdims=True)
            blk = jnp.where(lane == i0 + j, sc, blk)
        diag.append(blk)
    attn = attn + jnp.concatenate(diag, axis=0)

    state = state_ref[...]
    qb = (q * jnp.exp(b)).astype(_BF16)
    o = _nt(qb, state.astype(_BF16)) + jnp.dot(attn.astype(_BF16), v, preferred_element_type=_F32)
    kd = (k * jnp.exp(b_last - b)).astype(_BF16)
    state_ref[...] = state * jnp.exp(b_last) + _tn(v, kd)

    ms = jnp.mean(o * o, axis=-1, keepdims=True)
    on = o * lax.rsqrt(ms + RMS_EPS) * gain_ref[...]
    r = r_ref[...].astype(_F32)
    o_ref[...] = (on * (r * jax.nn.sigmoid(r))).astype(o_ref.dtype)


def _gla(qkvr, glr, wg, bias, gain, batch, seq, chunk=128):
    t = batch * seq
    nc = seq // chunk
    kq = GLA_HEADS
    kv = 2 * GLA_HEADS * GLA_DK // GLA_DV
    row = lambda b, h, c: b * nc + c
    in_specs = [
        pl.BlockSpec((chunk, GLA_DK), lambda b, h, c: (row(b, h, c), h)),
        pl.BlockSpec((chunk, GLA_DK), lambda b, h, c: (row(b, h, c), kq + h)),
        pl.BlockSpec((chunk, GLA_DV), lambda b, h, c: (row(b, h, c), kv + h)),
        pl.BlockSpec((chunk, GLA_DV), lambda b, h, c: (row(b, h, c), kv + GLA_HEADS + h)),
        pl.BlockSpec((chunk, LANES), lambda b, h, c: (row(b, h, c), 0)),
        pl.BlockSpec((LANES, GLA_DK), lambda b, h, c: (0, h)),
        pl.BlockSpec((1, GLA_DK), lambda b, h, c: (0, h)),
        pl.BlockSpec((1, GLA_DV), lambda b, h, c: (0, h)),
    ]
    return pl.pallas_call(
        _gla_kernel,
        grid=(batch, GLA_HEADS, nc),
        in_specs=in_specs,
        out_specs=pl.BlockSpec((chunk, GLA_DV), lambda b, h, c: (row(b, h, c), h)),
        out_shape=jax.ShapeDtypeStruct((t, GLA_HEADS * GLA_DV), _BF16),
        scratch_shapes=[pltpu.VMEM((GLA_DV, GLA_DK), _F32)],
        compiler_params=_params(("parallel", "parallel", "arbitrary"), 24 << 20),
        name="gla",
    )(qkvr, qkvr, qkvr, qkvr, glr, wg, bias, gain)


def _pair_rmsnorm(x, gain2):
    lane = lax.broadcasted_iota(jnp.int32, x.shape, 1)
    low = lane < SWA_HD
    sq = x * x
    s_all = jnp.sum(sq, axis=1, keepdims=True)
    s_low = jnp.sum(jnp.where(low, sq, 0.0), axis=1, keepdims=True)
    ms = jnp.where(low, s_low, s_all - s_low) * (1.0 / SWA_HD)
    return x * lax.rsqrt(ms + RMS_EPS) * gain2


def _swa_kernel(sinks_ref, q_ref, kc_ref, kp_ref, vc_ref, vp_ref, qg_ref, kg_ref, o_ref):
    blk = pl.program_id(1)
    bs = SWA_BLOCK
    group = SWA_HEADS // SWA_KV_HEADS
    pairs = group // 2
    lane = lax.broadcasted_iota(jnp.int32, (bs, LANES), 1)
    low = lane < SWA_HD
    lane2 = lax.broadcasted_iota(jnp.int32, (2 * bs, LANES), 1)
    low2 = lane2 < SWA_HD

    q_loc = lax.broadcasted_iota(jnp.int32, (group * bs, 2 * bs), 0) & (bs - 1)
    k_loc = lax.broadcasted_iota(jnp.int32, (group * bs, 2 * bs), 1)
    prev_from = q_loc + jnp.where(blk > 0, 0, 2 * bs)
    band = ((k_loc < bs) & (k_loc > prev_from)) | ((k_loc >= bs) & (k_loc - bs <= q_loc))

    for p in range(SWA_KV_HEADS // 2):
        cs = slice(p * LANES, (p + 1) * LANES)
        k2 = jnp.concatenate([kp_ref[:, cs], kc_ref[:, cs]], axis=0).astype(_F32)
        k2 = _pair_rmsnorm(k2, kg_ref[...])
        v2 = jnp.concatenate([vp_ref[:, cs], vc_ref[:, cs]], axis=0).astype(_F32)
        k2s = pltpu.roll(k2, SWA_HD, axis=1)
        v2s = pltpu.roll(v2, SWA_HD, axis=1)
        for half in range(2):
            h = 2 * p + half
            kk = (jnp.where(low2, k2, k2s) if half == 0 else jnp.where(low2, k2s, k2)).astype(_BF16)
            vv = (jnp.where(low2, v2, v2s) if half == 0 else jnp.where(low2, v2s, v2)).astype(_BF16)
            qs, sink = [], []
            for t in range(pairs):
                c0 = h * group * SWA_HD + t * LANES
                qn = _pair_rmsnorm(q_ref[:, c0:c0 + LANES].astype(_F32), qg_ref[...]) * (SWA_HD ** -0.5)
                qs.append(jnp.where(low, qn, 0.0))
                qs.append(jnp.where(low, 0.0, qn))
                sink.append(jnp.full((bs, 1), sinks_ref[h * group + 2 * t], _F32))
                sink.append(jnp.full((bs, 1), sinks_ref[h * group + 2 * t + 1], _F32))
            qstack = jnp.concatenate(qs, axis=0).astype(_BF16)
            sink = jnp.concatenate(sink, axis=0)
            s = jnp.where(band, _nt(qstack, kk), _NEG_INF)
            m = jnp.maximum(jnp.max(s, axis=1, keepdims=True), sink)
            e = jnp.exp(s - m)
            denom = jnp.sum(e, axis=1, keepdims=True) + jnp.exp(sink - m)
            o = jnp.dot(e.astype(_BF16), vv, preferred_element_type=_F32) / denom
            for t in range(pairs):
                c0 = h * group * SWA_HD + t * LANES
                oa = o[(2 * t) * bs:(2 * t + 1) * bs, :]
                ob = o[(2 * t + 1) * bs:(2 * t + 2) * bs, :]
                o_ref[:, c0:c0 + LANES] = jnp.where(low, oa, ob).astype(o_ref.dtype)


def _swa(sq, skv, sinks, q_gain, k_gain, batch, seq):
    t = batch * seq
    nb = seq // SWA_BLOCK
    kvw = SWA_KV_HEADS * SWA_HD
    cur = lambda b, i, s: (b * nb + i, 0)
    prev = lambda b, i, s: (b * nb + jnp.maximum(i - 1, 0), 0)
    cur_v = lambda b, i, s: (b * nb + i, 1)
    prev_v = lambda b, i, s: (b * nb + jnp.maximum(i - 1, 0), 1)
    const = lambda b, i, s: (0, 0)
    gs = pltpu.PrefetchScalarGridSpec(
        num_scalar_prefetch=1,
        grid=(batch, nb),
        in_specs=[
            pl.BlockSpec((SWA_BLOCK, SWA_HEADS * SWA_HD), cur),
            pl.BlockSpec((SWA_BLOCK, kvw), cur),
            pl.BlockSpec((SWA_BLOCK, kvw), prev),
            pl.BlockSpec((SWA_BLOCK, kvw), cur_v),
            pl.BlockSpec((SWA_BLOCK, kvw), prev_v),
            pl.BlockSpec((1, LANES), const),
            pl.BlockSpec((1, LANES), const),
        ],
        out_specs=pl.BlockSpec((SWA_BLOCK, SWA_HEADS * SWA_HD), cur),
    )
    g2 = lambda g: jnp.concatenate([g, g]).reshape(1, LANES).astype(_F32)
    return pl.pallas_call(
        _swa_kernel,
        grid_spec=gs,
        out_shape=jax.ShapeDtypeStruct((t, SWA_HEADS * SWA_HD), _BF16),
        compiler_params=_params(("parallel", "arbitrary"), 24 << 20),
        name="swa",
    )(sinks.astype(_F32), sq, skv, skv, skv, skv, g2(q_gain), g2(k_gain))


def _memattn_kernel(q_ref, k_ref, v_ref, g_ref, o_ref):
    hd = q_ref.shape[1] // MEM_HEADS
    for h in range(MEM_HEADS):
        cs = slice(h * hd, (h + 1) * hd)
        q = q_ref[:, cs].astype(_F32)
        ms = jnp.mean(q * q, axis=-1, keepdims=True)
        qn = (q * lax.rsqrt(ms + RMS_EPS) * g_ref[...] * (hd ** -0.5)).astype(_BF16)
        s = _nt(qn, k_ref[:, cs])
        m = jnp.max(s, axis=-1, keepdims=True)
        e = jnp.exp(s - m)
        denom = jnp.sum(e, axis=-1, keepdims=True)
        o = jnp.dot(e.astype(_BF16), v_ref[:, cs], preferred_element_type=_F32) / denom
        o_ref[:, cs] = o.astype(o_ref.dtype)


def _memattn(mq, kc, vm, q_gain, batch, seq, mem_len, tm=512):
    t, d = mq.shape
    tm = _tile(seq, tm)
    ns = seq // tm
    hd = d // MEM_HEADS
    return pl.pallas_call(
        _memattn_kernel,
        grid=(batch, ns),
        in_specs=[
            pl.BlockSpec((tm, d), lambda b, i: (b * ns + i, 0)),
            pl.BlockSpec((mem_len, d), lambda b, i: (b, 0)),
            pl.BlockSpec((mem_len, d), lambda b, i: (b, 0)),
            pl.BlockSpec((1, hd), lambda b, i: (0, 0)),
        ],
        out_specs=pl.BlockSpec((tm, d), lambda b, i: (b * ns + i, 0)),
        out_shape=jax.ShapeDtypeStruct((t, d), _BF16),
        compiler_params=_params(("parallel", "arbitrary"), 4 * tm * d * 2 + 4 * mem_len * d * 2 + (8 << 20)),
        name="memattn",
    )(mq, kc, vm, q_gain.reshape(1, hd).astype(_F32))


def _merge_kernel(a_ref, b_ref, c_ref, wa_ref, wb_ref, wc_ref, ga_ref, gb_ref, gc_ref, o_ref):
    def branch(x_ref, w_ref, g_ref):
        y = jnp.dot(x_ref[...], w_ref[...], preferred_element_type=_F32)
        return jax.nn.sigmoid(g_ref[...].astype(_F32)) * y

    o_ref[...] = (branch(a_ref, wa_ref, ga_ref) + branch(b_ref, wb_ref, gb_ref)
                  + branch(c_ref, wc_ref, gc_ref)).astype(o_ref.dtype)


def _merge(oa, ob, oc, wa, wb, wc, gate, tm=512, tn=256):
    t, d = oa.shape
    n = wa.shape[1]
    tm, tn = _tile(t, tm), _tile(n, tn)
    nn = n // tn
    act = pl.BlockSpec((tm, d), lambda i, j: (i, 0))
    wsp = pl.BlockSpec((d, tn), lambda i, j: (0, j))
    gsp = lambda b: pl.BlockSpec((tm, tn), lambda i, j: (i, b * nn + j))
    vmem = 2 * 3 * (tm * d * 2 + d * tn * 2 + tm * tn * 2) + 8 * tm * tn * 4
    return pl.pallas_call(
        _merge_kernel,
        grid=(t // tm, nn),
        in_specs=[act, act, act, wsp, wsp, wsp, gsp(0), gsp(1), gsp(2)],
        out_specs=pl.BlockSpec((tm, tn), lambda i, j: (i, j)),
        out_shape=jax.ShapeDtypeStruct((t, n), _BF16),
        compiler_params=_params(("parallel", "arbitrary"), vmem),
        name="merge",
    )(oa, ob, oc, wa, wb, wc, gate, gate, gate)


def _extract_topk(s, payload, kk):
    n = s.shape[0]
    iota = lax.broadcasted_iota(jnp.int32, s.shape, 0)
    vals, sel = [], []
    for _ in range(kk):
        m = jnp.max(s, axis=0, keepdims=True)
        idx = jnp.min(jnp.where(s == m, iota, n), axis=0, keepdims=True)
        hit = iota == idx
        vals.append(m)
        sel.append(idx if payload is None else jnp.sum(jnp.where(hit, payload, 0), axis=0, keepdims=True))
        s = jnp.where(hit, _NEG_INF, s)
    return jnp.concatenate(vals, axis=0), jnp.concatenate(sel, axis=0)


def _route_kernel(q_ref, keys_ref, i1_ref, i2_ref, g_ref):
    kk = PEER_TOPK
    half = PEER_DQ // 2
    experts, gates = [], []
    for h in range(PEER_HEADS):
        top = []
        for p in range(2):
            c0 = (2 * h + p) * half
            st = _nt(keys_ref[2 * h + p].astype(_BF16), q_ref[:, c0:c0 + half].astype(_BF16))
            top.append(_extract_topk(st, None, kk))
        (s0, i0), (s1, i1) = top
        cand = jnp.concatenate([s0[a:a + 1, :] + s1 for a in range(kk)], axis=0)
        ids = jnp.concatenate([i0[a:a + 1, :] * PEER_NKEYS + i1 for a in range(kk)], axis=0)
        best, e = _extract_topk(cand, ids, kk)
        ex = jnp.exp(best - best[0:1, :])
        gates.append(ex / jnp.sum(ex, axis=0, keepdims=True))
        experts.append(e)
    e = jnp.concatenate(experts, axis=0).T
    i1_ref[...] = e >> _log2(PEER_NKEYS)
    i2_ref[...] = e & (PEER_NKEYS - 1)
    g_ref[...] = jnp.concatenate(gates, axis=0).T


def _route(pq, sub_keys, tt=256):
    t = pq.shape[0]
    tt = _tile(t, tt)
    j = PEER_HEADS * PEER_TOPK
    half = PEER_DQ // 2
    keys = sub_keys.reshape(PEER_HEADS * 2, PEER_NKEYS, half)
    out = pl.BlockSpec((tt, j), lambda i: (i, 0))
    return pl.pallas_call(
        _route_kernel,
        grid=(t // tt,),
        in_specs=[pl.BlockSpec((tt, PEER_HEADS * PEER_DQ), lambda i: (i, 0)),
                  pl.BlockSpec(keys.shape, lambda i: (0, 0, 0))],
        out_specs=[out, out, out],
        out_shape=[jax.ShapeDtypeStruct((t, j), jnp.int32), jax.ShapeDtypeStruct((t, j), jnp.int32),
                   jax.ShapeDtypeStruct((t, j), _F32)],
        compiler_params=_params(("parallel",), 24 << 20),
        name="peer_route",
    )(pq, keys)


def _gatemat_kernel(i1_ref, i2_ref, g_ref, w_ref):
    nk = PEER_NKEYS
    sub = lax.broadcasted_iota(jnp.int32, (nk, i1_ref.shape[1]), 0)

    def body(t, carry):
        a_t = jnp.where(sub == i1_ref[pl.ds(t, 1), :], 1.0, 0.0).astype(_BF16)
        b_t = jnp.where(sub == i2_ref[pl.ds(t, 1), :], g_ref[pl.ds(t, 1), :], 0.0).astype(_BF16)
        w_ref[t] = _nt(a_t, b_t)
        return carry

    lax.fori_loop(0, i1_ref.shape[0], body, 0)


def _gatemat(i1, i2, g, tt=64):
    t, j = i1.shape
    tt = _tile(t, tt)
    nk = PEER_NKEYS
    spec = pl.BlockSpec((tt, j), lambda i: (i, 0))
    return pl.pallas_call(
        _gatemat_kernel,
        grid=(t // tt,),
        in_specs=[spec, spec, spec],
        out_specs=pl.BlockSpec((tt, nk, nk), lambda i: (i, 0, 0)),
        out_shape=jax.ShapeDtypeStruct((t, nk, nk), _F32),
        compiler_params=_params(("parallel",), 2 * tt * nk * nk * 4 + (8 << 20)),
        name="peer_gatemat",
    )(i1, i2, g)


def _gelu(x):
    return 0.5 * x * (1.0 + lax.erf(x * (2.0 ** -0.5)))


def _peer_act_kernel(x_ref, u_ref, w_ref, s_ref):
    act = _gelu(_nt(x_ref[...], u_ref[...]))
    for a in range(w_ref.shape[1]):
        cs = slice(a * PEER_NKEYS, (a + 1) * PEER_NKEYS)
        s_ref[:, cs] = (act[:, cs] * w_ref[:, a, :]).astype(s_ref.dtype)


def _peer_act(xn, u, w, tm=512, rows_a=8):
    t, d = xn.shape
    ne = u.shape[0]
    nk = PEER_NKEYS
    tm = _tile(t, tm)
    eb = rows_a * nk
    vmem = 2 * (tm * d * 2 + eb * d * 2 + tm * rows_a * nk * 4 + tm * eb * 2) + 3 * tm * eb * 4
    return pl.pallas_call(
        _peer_act_kernel,
        grid=(ne // eb, t // tm),
        in_specs=[pl.BlockSpec((tm, d), lambda j, i: (i, 0)),
                  pl.BlockSpec((eb, d), lambda j, i: (j, 0)),
                  pl.BlockSpec((tm, rows_a, nk), lambda j, i: (i, j, 0))],
        out_specs=pl.BlockSpec((tm, eb), lambda j, i: (i, j)),
        out_shape=jax.ShapeDtypeStruct((t, ne), _BF16),
        compiler_params=_params(("parallel", "parallel"), vmem),
        name="peer_act",
    )(xn, u, w)


def _mm_acc_kernel(x_ref, w_ref, r_ref, o_ref):
    acc = jnp.dot(x_ref[...], w_ref[...], preferred_element_type=_F32)

    @pl.when(pl.program_id(2) == 0)
    def _():
        o_ref[...] = r_ref[...] + acc

    @pl.when(pl.program_id(2) > 0)
    def _():
        o_ref[...] += acc


def _matmul_acc(x, w, residual, tm=1024, tn=2048, tk=1024):
    m, k = x.shape
    n = w.shape[1]
    tm, tn, tk = _tile(m, tm), _tile(n, tn), _tile(k, tk)
    vmem = 2 * (tm * tk * 2 + tk * tn * 2 + 2 * tm * tn * 4) + tm * tn * 4
    return pl.pallas_call(
        _mm_acc_kernel,
        grid=(m // tm, n // tn, k // tk),
        in_specs=[pl.BlockSpec((tm, tk), lambda i, j, l: (i, l)),
                  pl.BlockSpec((tk, tn), lambda i, j, l: (l, j)),
                  pl.BlockSpec((tm, tn), lambda i, j, l: (i, j))],
        out_specs=pl.BlockSpec((tm, tn), lambda i, j, l: (i, j)),
        out_shape=jax.ShapeDtypeStruct((m, n), _F32),
        compiler_params=_params(("parallel", "parallel", "arbitrary"), vmem),
        name="peer_out",
    )(x, w, residual)


def _layer(h, mem, norm_mix, w_in, gla_w_gate_up, gla_gate_bias, gla_out_norm, swa_q_norm, swa_k_norm,
           swa_sinks, mem_norm, w_mem_kv, mem_q_norm, mem_k_norm, w_branch_gla, w_branch_swa,
           w_branch_mem, w_out, norm_ffn, peer_w_q, peer_sub_keys, peer_u, peer_v, batch, seq):
    t, d = h.shape
    mem_len = mem.shape[0] // batch
    qk = GLA_HEADS * GLA_DK
    gv = GLA_HEADS * GLA_DV
    swq = SWA_HEADS * SWA_HD
    skv = SWA_KV_HEADS * SWA_HD
    o_glr = 2 * qk + 2 * gv
    o_sq = o_glr + GLA_RANK
    o_sk = o_sq + swq
    o_mq = o_sk + 2 * skv
    o_gate = o_mq + d
    bf = lambda w: w.astype(_BF16)

    xn = _rmsnorm(h, norm_mix, _BF16)
    qkvr = _matmul(xn, bf(w_in[:, :o_glr]), _BF16, name="proj_gla")
    w_glr = jnp.pad(w_in[:, o_glr:o_sq], ((0, 0), (0, LANES - GLA_RANK)))
    glr = _matmul(xn, bf(w_glr), _F32, name="proj_glr")
    sq = _matmul(xn, bf(w_in[:, o_sq:o_sk]), _BF16, name="proj_swa_q")
    skvp = _matmul(xn, bf(w_in[:, o_sk:o_mq]), _BF16, name="proj_swa_kv")
    mq = _matmul(xn, bf(w_in[:, o_mq:o_gate]), _BF16, name="proj_mem_q")
    gate = _matmul(xn, bf(w_in[:, o_gate:]), _BF16, name="proj_gate")

    wg = jnp.pad(gla_w_gate_up, ((0, LANES - GLA_RANK), (0, 0))).astype(_F32)
    o_a = _gla(qkvr, glr, wg, gla_gate_bias.reshape(1, qk).astype(_F32),
               gla_out_norm.reshape(1, gv).astype(_F32), batch, seq)

    o_b = _swa(sq, skvp, swa_sinks, swa_q_norm, swa_k_norm, batch, seq)

    memn = _rmsnorm(mem, mem_norm, _BF16)
    kv_m = _matmul(memn, bf(w_mem_kv), _BF16, name="mem_kv")
    hd = d // MEM_HEADS
    k_c = _rmsnorm(kv_m[:, :d].reshape(-1, hd), mem_k_norm, _BF16).reshape(-1, d)
    o_c = _memattn(mq, k_c, kv_m[:, d:], mem_q_norm, batch, seq, mem_len)

    mix = _merge(o_a, o_b, o_c, bf(w_branch_gla), bf(w_branch_swa), bf(w_branch_mem), gate)
    h = _matmul(mix, bf(w_out), _F32, residual=h, name="out_proj")

    hn = _rmsnorm(h, norm_ffn, _BF16)
    pq = _matmul(hn, bf(peer_w_q), _F32, name="peer_q")
    i1, i2, g = _route(pq, peer_sub_keys.astype(_F32))
    w = _gatemat(i1, i2, g)
    s = _peer_act(hn, bf(peer_u), w)
    return _matmul_acc(s, bf(peer_v), h)


def kernel(x, mem, norm_mix, w_in, gla_w_gate_up, gla_gate_bias, gla_out_norm, swa_q_norm, swa_k_norm,
           swa_sinks, mem_norm, w_mem_kv, mem_q_norm, mem_k_norm, w_branch_gla, w_branch_swa, w_branch_mem,
           w_out, norm_ffn, peer_w_q, peer_sub_keys, peer_u, peer_v):
    batch, seq, d = x.shape
    h = x.reshape(batch * seq, d)
    memf = mem.reshape(-1, d)
    for l in range(norm_mix.shape[0]):
        h = _layer(h, memf, norm_mix[l], w_in[l], gla_w_gate_up[l], gla_gate_bias[l], gla_out_norm[l],
                   swa_q_norm[l], swa_k_norm[l], swa_sinks[l], mem_norm[l], w_mem_kv[l], mem_q_norm[l],
                   mem_k_norm[l], w_branch_gla[l], w_branch_swa[l], w_branch_mem[l], w_out[l], norm_ffn[l],
                   peer_w_q[l], peer_sub_keys[l], peer_u[l], peer_v[l], batch, seq)
    return h.reshape(batch, seq, d)
```

```python
import functools

import jax
import jax.numpy as jnp
from jax import lax
from jax.experimental import pallas as pl
from jax.experimental.pallas import tpu as pltpu

_F32 = jnp.float32
_BF16 = jnp.bfloat16
_NEG_INF = float("-inf")

RMS_EPS = 1e-6
GLA_HEADS = 8
GLA_DK = 256
GLA_DV = 512
GLA_RANK = 16
GLA_TAU = 16.0
GLA_SUB = 8
SWA_HEADS = 64
SWA_KV_HEADS = 8
SWA_HD = 64
SWA_BLOCK = 128
MEM_HEADS = 4
PEER_HEADS = 8
PEER_NKEYS = 128
PEER_DQ = 256
PEER_TOPK = 16

LANES = 128
V7X_VMEM_BYTES = 64 * 1024 * 1024


def _tile(n, pref):
    t = min(n, pref)
    while n % t:
        t -= 1
    return t


def _params(semantics, vmem_bytes):
    limit = min(int(vmem_bytes) + (8 << 20), V7X_VMEM_BYTES - (4 << 20))
    return pltpu.CompilerParams(dimension_semantics=semantics, vmem_limit_bytes=limit)


def _nt(a, b):
    return lax.dot_general(a, b, (((1,), (1,)), ((), ())), preferred_element_type=_F32)


def _log2(n):
    assert n > 0 and n & (n - 1) == 0, n
    return n.bit_length() - 1


def _tn(a, b):
    return lax.dot_general(a, b, (((0,), (0,)), ((), ())), preferred_element_type=_F32)


def _rmsnorm_kernel(x_ref, g_ref, o_ref):
    x = x_ref[...].astype(_F32)
    ms = jnp.mean(x * x, axis=-1, keepdims=True)
    o_ref[...] = (x * lax.rsqrt(ms + RMS_EPS) * g_ref[...]).astype(o_ref.dtype)


def _rmsnorm(x, gain, out_dtype, rows=256):
    r, d = x.shape
    tr = _tile(r, rows)
    blk = tr * d * (x.dtype.itemsize + jnp.dtype(out_dtype).itemsize)
    return pl.pallas_call(
        _rmsnorm_kernel,
        grid=(r // tr,),
        in_specs=[pl.BlockSpec((tr, d), lambda i: (i, 0)), pl.BlockSpec((1, d), lambda i: (0, 0))],
        out_specs=pl.BlockSpec((tr, d), lambda i: (i, 0)),
        out_shape=jax.ShapeDtypeStruct((r, d), out_dtype),
        compiler_params=_params(("parallel",), 2 * blk + 4 * tr * d * 4),
        name="rmsnorm",
    )(x, gain.reshape(1, d).astype(_F32))


def _mm_kernel(x_ref, w_ref, o_ref):
    o_ref[...] = jnp.dot(x_ref[...], w_ref[...], preferred_element_type=_F32).astype(o_ref.dtype)


def _mm_res_kernel(x_ref, w_ref, r_ref, o_ref):
    acc = jnp.dot(x_ref[...], w_ref[...], preferred_element_type=_F32)
    o_ref[...] = (acc + r_ref[...].astype(_F32)).astype(o_ref.dtype)


def _matmul(x, w, out_dtype, tm=1024, tn=512, residual=None, name="matmul"):
    m, k = x.shape
    n = w.shape[1]
    tm, tn = _tile(m, tm), _tile(n, tn)
    osz = jnp.dtype(out_dtype).itemsize
    vmem = 2 * (tm * k * x.dtype.itemsize + k * tn * w.dtype.itemsize + tm * tn * osz) + tm * tn * 4
    in_specs = [pl.BlockSpec((tm, k), lambda j, i: (i, 0)), pl.BlockSpec((k, tn), lambda j, i: (0, j))]
    args = [x, w]
    kern = _mm_kernel
    if residual is not None:
        in_specs.append(pl.BlockSpec((tm, tn), lambda j, i: (i, j)))
        args.append(residual)
        kern = _mm_res_kernel
        vmem += 2 * tm * tn * residual.dtype.itemsize
    return pl.pallas_call(
        kern,
        grid=(n // tn, m // tm),
        in_specs=in_specs,
        out_specs=pl.BlockSpec((tm, tn), lambda j, i: (i, j)),
        out_shape=jax.ShapeDtypeStruct((m, n), out_dtype),
        compiler_params=_params(("parallel", "parallel"), vmem),
        name=name,
    )(*args)


def _log_sigmoid(z):
    return jnp.minimum(z, 0.0) - jnp.log(1.0 + jnp.exp(-jnp.abs(z)))


def _gla_kernel(q_ref, k_ref, v_ref, r_ref, glr_ref, wg_ref, bias_ref, gain_ref, o_ref, state_ref):
    c = q_ref.shape[0]

    @pl.when(pl.program_id(2) == 0)
    def _():
        state_ref[...] = jnp.zeros_like(state_ref)

    q = q_ref[...].astype(_F32) * (GLA_DK ** -0.5)
    k = k_ref[...].astype(_F32)
    v = v_ref[...]
    z = jnp.dot(glr_ref[...], wg_ref[...], preferred_element_type=_F32) + bias_ref[...]
    log_a = _log_sigmoid(z) * (1.0 / GLA_TAU)

    row = lax.broadcasted_iota(jnp.int32, (c, c), 0)
    col = lax.broadcasted_iota(jnp.int32, (c, c), 1)
    tri = (row >= col).astype(_F32)
    b = jnp.dot(tri, log_a, preferred_element_type=_F32, precision=lax.Precision.HIGHEST)
    b_last = b[c - 1:c, :]

    attn = jnp.zeros((c, c), _F32)
    row1 = lax.broadcasted_iota(jnp.int32, (c, 1), 0)
    s = c // 2
    while s >= GLA_SUB:
        pieces = []
        for lo in range(0, c, 2 * s):
            mid = lo + s
            ref_row = b[mid - 1:mid, :]
            pieces.append(ref_row - b[lo:mid, :])
            pieces.append(b[mid:mid + s, :] - ref_row)
        x = jnp.exp(jnp.concatenate(pieces, axis=0))
        upper = ((row1 >> _log2(s)) & 1) == 1
        qt = jnp.where(upper, q * x, 0.0).astype(_BF16)
        kt = jnp.where(upper, 0.0, k * x).astype(_BF16)
        same = (row >> _log2(2 * s)) == (col >> _log2(2 * s))
        attn = attn + jnp.where(same, _nt(qt, kt), 0.0)
        s //= 2

    sub_row = lax.broadcasted_iota(jnp.int32, (GLA_SUB, 1), 0)
    lane = lax.broadcasted_iota(jnp.int32, (GLA_SUB, c), 1)
    diag = []
    for i0 in range(0, c, GLA_SUB):
        qi, ki, bi = q[i0:i0 + GLA_SUB, :], k[i0:i0 + GLA_SUB, :], b[i0:i0 + GLA_SUB, :]
        blk = jnp.zeros((GLA_SUB, c), _F32)
        for j in range(GLA_SUB):
            dec = jnp.exp(jnp.where(sub_row >= j, bi - bi[j:j + 1, :], _NEG_INF))
            sc = jnp.sum(qi * dec * ki[j:j + 1, :], axis=1, keepdims=True)
            blk = jnp.where(lane == i0 + j, sc, blk)
        diag.append(blk)
    attn = attn + jnp.concatenate(diag, axis=0)

    state = state_ref[...]
    qb = (q * jnp.exp(b)).astype(_BF16)
    o = _nt(qb, state.astype(_BF16)) + jnp.dot(attn.astype(_BF16), v, preferred_element_type=_F32)
    kd = (k * jnp.exp(b_last - b)).astype(_BF16)
    state_ref[...] = state * jnp.exp(b_last) + _tn(v, kd)

    ms = jnp.mean(o * o, axis=-1, keepdims=True)
    on = o * lax.rsqrt(ms + RMS_EPS) * gain_ref[...]
    r = r_ref[...].astype(_F32)
    o_ref[...] = (on * (r * jax.nn.sigmoid(r))).astype(o_ref.dtype)


def _gla(qkvr, glr, wg, bias, gain, batch, seq, chunk=128):
    t = batch * seq
    nc = seq // chunk
    kq = GLA_HEADS
    kv = 2 * GLA_HEADS * GLA_DK // GLA_DV
    row = lambda b, h, c: b * nc + c
    in_specs = [
        pl.BlockSpec((chunk, GLA_DK), lambda b, h, c: (row(b, h, c), h)),
        pl.BlockSpec((chunk, GLA_DK), lambda b, h, c: (row(b, h, c), kq + h)),
        pl.BlockSpec((chunk, GLA_DV), lambda b, h, c: (row(b, h, c), kv + h)),
        pl.BlockSpec((chunk, GLA_DV), lambda b, h, c: (row(b, h, c), kv + GLA_HEADS + h)),
        pl.BlockSpec((chunk, LANES), lambda b, h, c: (row(b, h, c), 0)),
        pl.BlockSpec((LANES, GLA_DK), lambda b, h, c: (0, h)),
        pl.BlockSpec((1, GLA_DK), lambda b, h, c: (0, h)),
        pl.BlockSpec((1, GLA_DV), lambda b, h, c: (0, h)),
    ]
    return pl.pallas_call(
        _gla_kernel,
        grid=(batch, GLA_HEADS, nc),
        in_specs=in_specs,
        out_specs=pl.BlockSpec((chunk, GLA_DV), lambda b, h, c: (row(b, h, c), h)),
        out_shape=jax.ShapeDtypeStruct((t, GLA_HEADS * GLA_DV), _BF16),
        scratch_shapes=[pltpu.VMEM((GLA_DV, GLA_DK), _F32)],
        compiler_params=_params(("parallel", "parallel", "arbitrary"), 24 << 20),
        name="gla",
    )(qkvr, qkvr, qkvr, qkvr, glr, wg, bias, gain)


def _pair_rmsnorm(x, gain2):
    lane = lax.broadcasted_iota(jnp.int32, x.shape, 1)
    low = lane < SWA_HD
    sq = x * x
    s_all = jnp.sum(sq, axis=1, keepdims=True)
    s_low = jnp.sum(jnp.where(low, sq, 0.0), axis=1, keepdims=True)
    ms = jnp.where(low, s_low, s_all - s_low) * (1.0 / SWA_HD)
    return x * lax.rsqrt(ms + RMS_EPS) * gain2


def _swa_kernel(sinks_ref, q_ref, kc_ref, kp_ref, vc_ref, vp_ref, qg_ref, kg_ref, o_ref):
    blk = pl.program_id(1)
    bs = SWA_BLOCK
    group = SWA_HEADS // SWA_KV_HEADS
    pairs = group // 2
    lane = lax.broadcasted_iota(jnp.int32, (bs, LANES), 1)
    low = lane < SWA_HD

    q_loc = lax.broadcasted_iota(jnp.int32, (group * bs, bs), 0) & (bs - 1)
    from_cur = lax.broadcasted_iota(jnp.int32, (group * bs, bs), 1) <= q_loc
    prev_bias = jnp.where(blk > 0, 0.0, _NEG_INF)

    for p in range(SWA_KV_HEADS // 2):
        cs = slice(p * LANES, (p + 1) * LANES)
        tiles = [_pair_rmsnorm(kc_ref[:, cs].astype(_F32), kg_ref[...]),
                 _pair_rmsnorm(kp_ref[:, cs].astype(_F32), kg_ref[...]),
                 vc_ref[:, cs].astype(_F32), vp_ref[:, cs].astype(_F32)]
        swapped = [pltpu.roll(x, SWA_HD, axis=1) for x in tiles]
        for half in range(2):
            h = 2 * p + half
            kc, kp, vc, vp = [(jnp.where(low, x, xs) if half == 0 else jnp.where(low, xs, x)).astype(_BF16)
                              for x, xs in zip(tiles, swapped)]
            qs, sink = [], []
            for t in range(pairs):
                c0 = h * group * SWA_HD + t * LANES
                qn = _pair_rmsnorm(q_ref[:, c0:c0 + LANES].astype(_F32), qg_ref[...]) * (SWA_HD ** -0.5)
                qs.append(jnp.where(low, qn, 0.0))
                qs.append(jnp.where(low, 0.0, qn))
                sink.append(jnp.full((bs, 1), sinks_ref[h * group + 2 * t], _F32))
                sink.append(jnp.full((bs, 1), sinks_ref[h * group + 2 * t + 1], _F32))
            qstack = jnp.concatenate(qs, axis=0).astype(_BF16)
            sink = jnp.concatenate(sink, axis=0)
            s = jnp.where(from_cur, _nt(qstack, kc), _nt(qstack, kp) + prev_bias)
            m = jnp.maximum(jnp.max(s, axis=1, keepdims=True), sink)
            e = jnp.exp(s - m)
            denom = jnp.sum(e, axis=1, keepdims=True) + jnp.exp(sink - m)
            e_cur = jnp.where(from_cur, e, 0.0).astype(_BF16)
            e_prev = jnp.where(from_cur, 0.0, e).astype(_BF16)
            o = (jnp.dot(e_cur, vc, preferred_element_type=_F32)
                 + jnp.dot(e_prev, vp, preferred_element_type=_F32)) / denom
            for t in range(pairs):
                c0 = h * group * SWA_HD + t * LANES
                oa = o[(2 * t) * bs:(2 * t + 1) * bs, :]
                ob = o[(2 * t + 1) * bs:(2 * t + 2) * bs, :]
                o_ref[:, c0:c0 + LANES] = jnp.where(low, oa, ob).astype(o_ref.dtype)


def _swa(sq, skv, sinks, q_gain, k_gain, batch, seq):
    t = batch * seq
    nb = seq // SWA_BLOCK
    kvw = SWA_KV_HEADS * SWA_HD
    cur = lambda b, i, s: (b * nb + i, 0)
    prev = lambda b, i, s: (b * nb + jnp.maximum(i - 1, 0), 0)
    cur_v = lambda b, i, s: (b * nb + i, 1)
    prev_v = lambda b, i, s: (b * nb + jnp.maximum(i - 1, 0), 1)
    const = lambda b, i, s: (0, 0)
    gs = pltpu.PrefetchScalarGridSpec(
        num_scalar_prefetch=1,
        grid=(batch, nb),
        in_specs=[
            pl.BlockSpec((SWA_BLOCK, SWA_HEADS * SWA_HD), cur),
            pl.BlockSpec((SWA_BLOCK, kvw), cur),
            pl.BlockSpec((SWA_BLOCK, kvw), prev),
            pl.BlockSpec((SWA_BLOCK, kvw), cur_v),
            pl.BlockSpec((SWA_BLOCK, kvw), prev_v),
            pl.BlockSpec((1, LANES), const),
            pl.BlockSpec((1, LANES), const),
        ],
        out_specs=pl.BlockSpec((SWA_BLOCK, SWA_HEADS * SWA_HD), cur),
    )
    g2 = lambda g: jnp.concatenate([g, g]).reshape(1, LANES).astype(_F32)
    return pl.pallas_call(
        _swa_kernel,
        grid_spec=gs,
        out_shape=jax.ShapeDtypeStruct((t, SWA_HEADS * SWA_HD), _BF16),
        compiler_params=_params(("parallel", "arbitrary"), 24 << 20),
        name="swa",
    )(sinks.astype(_F32), sq, skv, skv, skv, skv, g2(q_gain), g2(k_gain))


def _memattn_kernel(q_ref, k_ref, v_ref, g_ref, o_ref):
    hd = q_ref.shape[1] // MEM_HEADS
    for h in range(MEM_HEADS):
        cs = slice(h * hd, (h + 1) * hd)
        q = q_ref[:, cs].astype(_F32)
        ms = jnp.mean(q * q, axis=-1, keepdims=True)
        qn = (q * lax.rsqrt(ms + RMS_EPS) * g_ref[...] * (hd ** -0.5)).astype(_BF16)
        s = _nt(qn, k_ref[:, cs])
        m = jnp.max(s, axis=-1, keepdims=True)
        e = jnp.exp(s - m)
        denom = jnp.sum(e, axis=-1, keepdims=True)
        o = jnp.dot(e.astype(_BF16), v_ref[:, cs], preferred_element_type=_F32) / denom
        o_ref[:, cs] = o.astype(o_ref.dtype)


def _memattn(mq, kc, vm, q_gain, batch, seq, mem_len, tm=512):
    t, d = mq.shape
    tm = _tile(seq, tm)
    ns = seq // tm
    hd = d // MEM_HEADS
    return pl.pallas_call(
        _memattn_kernel,
        grid=(batch, ns),
        in_specs=[
            pl.BlockSpec((tm, d), lambda b, i: (b * ns + i, 0)),
            pl.BlockSpec((mem_len, d), lambda b, i: (b, 0)),
            pl.BlockSpec((mem_len, d), lambda b, i: (b, 0)),
            pl.BlockSpec((1, hd), lambda b, i: (0, 0)),
        ],
        out_specs=pl.BlockSpec((tm, d), lambda b, i: (b * ns + i, 0)),
        out_shape=jax.ShapeDtypeStruct((t, d), _BF16),
        compiler_params=_params(("parallel", "arbitrary"), 4 * tm * d * 2 + 4 * mem_len * d * 2 + (8 << 20)),
        name="memattn",
    )(mq, kc, vm, q_gain.reshape(1, hd).astype(_F32))


def _merge_kernel(a_ref, b_ref, c_ref, wa_ref, wb_ref, wc_ref, ga_ref, gb_ref, gc_ref, o_ref):
    def branch(x_ref, w_ref, g_ref):
        y = jnp.dot(x_ref[...], w_ref[...], preferred_element_type=_F32)
        return jax.nn.sigmoid(g_ref[...].astype(_F32)) * y

    o_ref[...] = (branch(a_ref, wa_ref, ga_ref) + branch(b_ref, wb_ref, gb_ref)
                  + branch(c_ref, wc_ref, gc_ref)).astype(o_ref.dtype)


def _merge(oa, ob, oc, wa, wb, wc, gate, tm=512, tn=512):
    t, d = oa.shape
    n = wa.shape[1]
    tm, tn = _tile(t, tm), _tile(n, tn)
    nn = n // tn
    act = pl.BlockSpec((tm, d), lambda j, i: (i, 0))
    wsp = pl.BlockSpec((d, tn), lambda j, i: (0, j), pipeline_mode=pl.Buffered(1))
    gsp = lambda b: pl.BlockSpec((tm, tn), lambda j, i: (i, b * nn + j))
    vmem = 3 * (2 * tm * d * 2 + d * tn * 2 + 2 * tm * tn * 2) + 8 * tm * tn * 4
    return pl.pallas_call(
        _merge_kernel,
        grid=(nn, t // tm),
        in_specs=[act, act, act, wsp, wsp, wsp, gsp(0), gsp(1), gsp(2)],
        out_specs=pl.BlockSpec((tm, tn), lambda j, i: (i, j)),
        out_shape=jax.ShapeDtypeStruct((t, n), _BF16),
        compiler_params=_params(("parallel", "arbitrary"), vmem),
        name="merge",
    )(oa, ob, oc, wa, wb, wc, gate, gate, gate)


def _extract_topk(s, payload, kk):
    n = s.shape[0]
    pos = lax.broadcasted_iota(jnp.int32, s.shape, 0).astype(_F32)
    vals, sel = [], []
    for _ in range(kk):
        m = jnp.max(s, axis=0, keepdims=True)
        first = jnp.min(jnp.where(s == m, pos, float(n)), axis=0, keepdims=True)
        hit = pos == first
        vals.append(m)
        sel.append(first if payload is None else jnp.max(jnp.where(hit, payload, -1.0), axis=0, keepdims=True))
        s = jnp.where(hit, _NEG_INF, s)
    return jnp.concatenate(vals, axis=0), jnp.concatenate(sel, axis=0)


def _route_kernel(q_ref, keys_ref, i1_ref, i2_ref, g_ref):
    kk = PEER_TOPK
    half = PEER_DQ // 2
    experts, gates = [], []
    for h in range(PEER_HEADS):
        top = []
        for p in range(2):
            c0 = (2 * h + p) * half
            st = _nt(keys_ref[2 * h + p].astype(_BF16), q_ref[:, c0:c0 + half].astype(_BF16))
            top.append(_extract_topk(st, None, kk))
        (s0, i0), (s1, i1) = top
        cand, ids = [], []
        tail = kk // 2
        for a in range(tail):
            nb = kk // (a + 1)
            rows = -(-nb // 8) * 8
            c = s0[a:a + 1, :] + s1[0:rows, :]
            if rows != nb:
                c = jnp.where(lax.broadcasted_iota(jnp.int32, c.shape, 0) < nb, c, _NEG_INF)
            cand.append(c)
            ids.append(i0[a:a + 1, :] * PEER_NKEYS + i1[0:rows, :])
        cand.append(s0[tail:kk, :] + s1[0:1, :])
        ids.append(i0[tail:kk, :] * PEER_NKEYS + i1[0:1, :])
        best, e = _extract_topk(jnp.concatenate(cand, axis=0), jnp.concatenate(ids, axis=0), kk)
        ex = jnp.exp(best - best[0:1, :])
        gates.append(ex / jnp.sum(ex, axis=0, keepdims=True))
        experts.append(e)
    e = jnp.concatenate(experts, axis=0).T.astype(jnp.int32)
    i1_ref[...] = e >> _log2(PEER_NKEYS)
    i2_ref[...] = e & (PEER_NKEYS - 1)
    g_ref[...] = jnp.concatenate(gates, axis=0).T


def _route(pq, sub_keys, tt=256):
    t = pq.shape[0]
    tt = _tile(t, tt)
    j = PEER_HEADS * PEER_TOPK
    half = PEER_DQ // 2
    keys = sub_keys.reshape(PEER_HEADS * 2, PEER_NKEYS, half)
    out = pl.BlockSpec((tt, j), lambda i: (i, 0))
    return pl.pallas_call(
        _route_kernel,
        grid=(t // tt,),
        in_specs=[pl.BlockSpec((tt, PEER_HEADS * PEER_DQ), lambda i: (i, 0)),
                  pl.BlockSpec(keys.shape, lambda i: (0, 0, 0))],
        out_specs=[out, out, out],
        out_shape=[jax.ShapeDtypeStruct((t, j), jnp.int32), jax.ShapeDtypeStruct((t, j), jnp.int32),
                   jax.ShapeDtypeStruct((t, j), _F32)],
        compiler_params=_params(("parallel",), 24 << 20),
        name="peer_route",
    )(pq, keys)


def _gatemat_kernel(i1_ref, i2_ref, g_ref, w_ref):
    nk = PEER_NKEYS
    sub = lax.broadcasted_iota(jnp.int32, (nk, i1_ref.shape[1]), 0)

    def body(t, carry):
        a_t = jnp.where(sub == i1_ref[pl.ds(t, 1), :], 1.0, 0.0).astype(_BF16)
        b_t = jnp.where(sub == i2_ref[pl.ds(t, 1), :], g_ref[pl.ds(t, 1), :], 0.0).astype(_BF16)
        w_ref[t] = _nt(a_t, b_t)
        return carry

    lax.fori_loop(0, i1_ref.shape[0], body, 0, unroll=8)


def _gatemat(i1, i2, g, tt=64):
    t, j = i1.shape
    tt = _tile(t, tt)
    nk = PEER_NKEYS
    spec = pl.BlockSpec((tt, j), lambda i: (i, 0))
    return pl.pallas_call(
        _gatemat_kernel,
        grid=(t // tt,),
        in_specs=[spec, spec, spec],
        out_specs=pl.BlockSpec((tt, nk, nk), lambda i: (i, 0, 0)),
        out_shape=jax.ShapeDtypeStruct((t, nk, nk), _F32),
        compiler_params=_params(("parallel",), 2 * tt * nk * nk * 4 + (8 << 20)),
        name="peer_gatemat",
    )(i1, i2, g)


def _gelu(x):
    return 0.5 * x * (1.0 + lax.erf(x * (2.0 ** -0.5)))


def _peer_act_kernel(x_ref, u_ref, w_ref, s_ref):
    act = _gelu(_nt(x_ref[...], u_ref[...]))
    w = jnp.transpose(w_ref[...], (1, 0, 2))
    for a in range(w.shape[0]):
        cs = slice(a * PEER_NKEYS, (a + 1) * PEER_NKEYS)
        s_ref[:, cs] = (act[:, cs] * w[a]).astype(s_ref.dtype)


def _peer_act(xn, u, w, tm=512, rows_a=8):
    t, d = xn.shape
    ne = u.shape[0]
    nk = PEER_NKEYS
    tm = _tile(t, tm)
    eb = rows_a * nk
    vmem = 2 * (tm * d * 2 + eb * d * 2 + tm * rows_a * nk * 4 + tm * eb * 2) + 3 * tm * eb * 4
    return pl.pallas_call(
        _peer_act_kernel,
        grid=(ne // eb, t // tm),
        in_specs=[pl.BlockSpec((tm, d), lambda j, i: (i, 0)),
                  pl.BlockSpec((eb, d), lambda j, i: (j, 0)),
                  pl.BlockSpec((tm, rows_a, nk), lambda j, i: (i, j, 0))],
        out_specs=pl.BlockSpec((tm, eb), lambda j, i: (i, j)),
        out_shape=jax.ShapeDtypeStruct((t, ne), _BF16),
        compiler_params=_params(("parallel", "parallel"), vmem),
        name="peer_act",
    )(xn, u, w)


def _mm_acc_kernel(x_ref, w_ref, r_ref, o_ref):
    acc = jnp.dot(x_ref[...], w_ref[...], preferred_element_type=_F32)

    @pl.when(pl.program_id(2) == 0)
    def _():
        o_ref[...] = r_ref[...] + acc

    @pl.when(pl.program_id(2) > 0)
    def _():
        o_ref[...] += acc


def _matmul_acc(x, w, residual, tm=1024, tn=1024, tk=2048):
    m, k = x.shape
    n = w.shape[1]
    tm, tn, tk = _tile(m, tm), _tile(n, tn), _tile(k, tk)
    vmem = 2 * (tm * tk * 2 + tk * tn * 2 + 2 * tm * tn * 4) + tm * tn * 4
    return pl.pallas_call(
        _mm_acc_kernel,
        grid=(m // tm, n // tn, k // tk),
        in_specs=[pl.BlockSpec((tm, tk), lambda i, j, l: (i, l)),
                  pl.BlockSpec((tk, tn), lambda i, j, l: (l, j)),
                  pl.BlockSpec((tm, tn), lambda i, j, l: (i, j))],
        out_specs=pl.BlockSpec((tm, tn), lambda i, j, l: (i, j)),
        out_shape=jax.ShapeDtypeStruct((m, n), _F32),
        compiler_params=_params(("parallel", "parallel", "arbitrary"), vmem),
        name="peer_out",
    )(x, w, residual)


def _layer(h, mem, norm_mix, w_in, gla_w_gate_up, gla_gate_bias, gla_out_norm, swa_q_norm, swa_k_norm,
           swa_sinks, mem_norm, w_mem_kv, mem_q_norm, mem_k_norm, w_branch_gla, w_branch_swa,
           w_branch_mem, w_out, norm_ffn, peer_w_q, peer_sub_keys, peer_u, peer_v, batch, seq):
    t, d = h.shape
    mem_len = mem.shape[0] // batch
    qk = GLA_HEADS * GLA_DK
    gv = GLA_HEADS * GLA_DV
    swq = SWA_HEADS * SWA_HD
    skv = SWA_KV_HEADS * SWA_HD
    o_glr = 2 * qk + 2 * gv
    o_sq = o_glr + GLA_RANK
    o_sk = o_sq + swq
    o_mq = o_sk + 2 * skv
    o_gate = o_mq + d
    bf = lambda w: w.astype(_BF16)

    xn = _rmsnorm(h, norm_mix, _BF16)
    qkvr = _matmul(xn, bf(w_in[:, :o_glr]), _BF16, name="proj_gla")
    w_glr = jnp.pad(w_in[:, o_glr:o_sq], ((0, 0), (0, LANES - GLA_RANK)))
    glr = _matmul(xn, bf(w_glr), _BF16, name="proj_glr")
    sq = _matmul(xn, bf(w_in[:, o_sq:o_sk]), _BF16, name="proj_swa_q")
    skvp = _matmul(xn, bf(w_in[:, o_sk:o_mq]), _BF16, name="proj_swa_kv")
    mq = _matmul(xn, bf(w_in[:, o_mq:o_gate]), _BF16, name="proj_mem_q")
    gate = _matmul(xn, bf(w_in[:, o_gate:]), _BF16, name="proj_gate")

    wg = bf(jnp.pad(gla_w_gate_up, ((0, LANES - GLA_RANK), (0, 0))))
    o_a = _gla(qkvr, glr, wg, gla_gate_bias.reshape(1, qk).astype(_F32),
               gla_out_norm.reshape(1, gv).astype(_F32), batch, seq)

    o_b = _swa(sq, skvp, swa_sinks, swa_q_norm, swa_k_norm, batch, seq)

    memn = _rmsnorm(mem, mem_norm, _BF16)
    kv_m = _matmul(memn, bf(w_mem_kv), _BF16, name="mem_kv")
    hd = d // MEM_HEADS
    k_c = _rmsnorm(kv_m[:, :d].reshape(-1, hd), mem_k_norm, _BF16).reshape(-1, d)
    o_c = _memattn(mq, k_c, kv_m[:, d:], mem_q_norm, batch, seq, mem_len)

    mix = _merge(o_a, o_b, o_c, bf(w_branch_gla), bf(w_branch_swa), bf(w_branch_mem), gate)
    h = _matmul(mix, bf(w_out), _F32, residual=h, name="out_proj")

    hn = _rmsnorm(h, norm_ffn, _BF16)
    pq = _matmul(hn, bf(peer_w_q), _F32, name="peer_q")
    i1, i2, g = _route(pq, peer_sub_keys.astype(_F32))
    w = _gatemat(i1, i2, g)
    s = _peer_act(hn, bf(peer_u), w)
    return _matmul_acc(s, bf(peer_v), h)


def kernel(x, mem, norm_mix, w_in, gla_w_gate_up, gla_gate_bias, gla_out_norm, swa_q_norm, swa_k_norm,
           swa_sinks, mem_norm, w_mem_kv, mem_q_norm, mem_k_norm, w_branch_gla, w_branch_swa, w_branch_mem,
           w_out, norm_ffn, peer_w_q, peer_sub_keys, peer_u, peer_v):
    batch, seq, d = x.shape
    h = x.reshape(batch * seq, d)
    memf = mem.reshape(-1, d)
    for l in range(norm_mix.shape[0]):
        h = _layer(h, memf, norm_mix[l], w_in[l], gla_w_gate_up[l], gla_gate_bias[l], gla_out_norm[l],
                   swa_q_norm[l], swa_k_norm[l], swa_sinks[l], mem_norm[l], w_mem_kv[l], mem_q_norm[l],
                   mem_k_norm[l], w_branch_gla[l], w_branch_swa[l], w_branch_mem[l], w_out[l], norm_ffn[l],
                   peer_w_q[l], peer_sub_keys[l], peer_u[l], peer_v[l], batch, seq)
    return h.reshape(batch, seq, d)
```

```python
import functools

import jax
import jax.numpy as jnp
from jax import lax
from jax.experimental import pallas as pl
from jax.experimental.pallas import tpu as pltpu

_F32 = jnp.float32
_BF16 = jnp.bfloat16
_NEG_INF = float("-inf")

RMS_EPS = 1e-6
GLA_HEADS = 8
GLA_DK = 256
GLA_DV = 512
GLA_RANK = 16
GLA_TAU = 16.0
GLA_SUB = 8
GLA_STEP_HEADS = 2
_LOG2_E = 1.4426950408889634
SWA_HEADS = 64
SWA_KV_HEADS = 8
SWA_HD = 64
SWA_BLOCK = 128
MEM_HEADS = 4
PEER_HEADS = 8
PEER_NKEYS = 128
PEER_DQ = 256
PEER_TOPK = 16

LANES = 128
V7X_VMEM_BYTES = 64 * 1024 * 1024


def _tile(n, pref):
    t = min(n, pref)
    while n % t:
        t -= 1
    return t


def _params(semantics, vmem_bytes):
    limit = min(int(vmem_bytes) + (8 << 20), V7X_VMEM_BYTES - (4 << 20))
    return pltpu.CompilerParams(dimension_semantics=semantics, vmem_limit_bytes=limit)


def _nt(a, b):
    return lax.dot_general(a, b, (((1,), (1,)), ((), ())), preferred_element_type=_F32)


def _log2(n):
    assert n > 0 and n & (n - 1) == 0, n
    return n.bit_length() - 1


def _tn(a, b):
    return lax.dot_general(a, b, (((0,), (0,)), ((), ())), preferred_element_type=_F32)


def _rmsnorm_kernel(x_ref, g_ref, o_ref):
    x = x_ref[...].astype(_F32)
    ms = jnp.mean(x * x, axis=-1, keepdims=True)
    o_ref[...] = (x * lax.rsqrt(ms + RMS_EPS) * g_ref[...]).astype(o_ref.dtype)


def _rmsnorm(x, gain, out_dtype, rows=256):
    r, d = x.shape
    tr = _tile(r, rows)
    blk = tr * d * (x.dtype.itemsize + jnp.dtype(out_dtype).itemsize)
    return pl.pallas_call(
        _rmsnorm_kernel,
        grid=(r // tr,),
        in_specs=[pl.BlockSpec((tr, d), lambda i: (i, 0)), pl.BlockSpec((1, d), lambda i: (0, 0))],
        out_specs=pl.BlockSpec((tr, d), lambda i: (i, 0)),
        out_shape=jax.ShapeDtypeStruct((r, d), out_dtype),
        compiler_params=_params(("parallel",), 2 * blk + 4 * tr * d * 4),
        name="rmsnorm",
    )(x, gain.reshape(1, d).astype(_F32))


def _mm_kernel(x_ref, w_ref, o_ref):
    o_ref[...] = jnp.dot(x_ref[...], w_ref[...], preferred_element_type=_F32).astype(o_ref.dtype)


def _mm_res_kernel(x_ref, w_ref, r_ref, o_ref):
    acc = jnp.dot(x_ref[...], w_ref[...], preferred_element_type=_F32)
    o_ref[...] = (acc + r_ref[...].astype(_F32)).astype(o_ref.dtype)


def _matmul(x, w, out_dtype, tm=1024, tn=512, residual=None, name="matmul"):
    m, k = x.shape
    n = w.shape[1]
    tm, tn = _tile(m, tm), _tile(n, tn)
    osz = jnp.dtype(out_dtype).itemsize
    vmem = 2 * (tm * k * x.dtype.itemsize + k * tn * w.dtype.itemsize + tm * tn * osz) + tm * tn * 4
    in_specs = [pl.BlockSpec((tm, k), lambda j, i: (i, 0)), pl.BlockSpec((k, tn), lambda j, i: (0, j))]
    args = [x, w]
    kern = _mm_kernel
    if residual is not None:
        in_specs.append(pl.BlockSpec((tm, tn), lambda j, i: (i, j)))
        args.append(residual)
        kern = _mm_res_kernel
        vmem += 2 * tm * tn * residual.dtype.itemsize
    return pl.pallas_call(
        kern,
        grid=(n // tn, m // tm),
        in_specs=in_specs,
        out_specs=pl.BlockSpec((tm, tn), lambda j, i: (i, j)),
        out_shape=jax.ShapeDtypeStruct((m, n), out_dtype),
        compiler_params=_params(("parallel", "parallel"), vmem),
        name=name,
    )(*args)


def _mm_w32_kernel(*refs, shift, has_residual):
    x_ref, w_ref = refs[0], refs[1]
    o_ref, wbf_ref = refs[-2], refs[-1]
    tn = o_ref.shape[1]

    @pl.when(pl.program_id(1) == 0)
    def _():
        rows = _tile(w_ref.shape[0], 512)
        for r0 in range(0, w_ref.shape[0], rows):
            rs = slice(r0, r0 + rows)
            w = w_ref[rs, :]
            if shift:
                w = jnp.concatenate([w, refs[2][rs, :]], axis=1)
                w = pltpu.roll(w, w.shape[1] - shift, axis=1)[:, :tn]
            wbf_ref[rs, :] = w.astype(_BF16)

    acc = jnp.dot(x_ref[...], wbf_ref[...], preferred_element_type=_F32)
    if has_residual:
        acc = acc + refs[-3][...].astype(_F32)
    o_ref[...] = acc.astype(o_ref.dtype)


def _matmul_w32(x, w, col0, n, out_dtype, tm=1024, tn=512, residual=None, name="matmul_w32"):
    m, k = x.shape
    base = col0 // LANES * LANES
    shift = col0 - base
    tm = _tile(m, tm)
    tn = max(t for t in range(LANES, min(n, tn) + 1, LANES) if n % t == 0 and base % t == 0)
    osz = jnp.dtype(out_dtype).itemsize
    in_specs = [pl.BlockSpec((tm, k), lambda j, i: (i, 0)),
                pl.BlockSpec((k, tn), lambda j, i: (0, base // tn + j))]
    args = [x, w]
    vmem = 2 * (tm * k * 2 + k * tn * 4 + tm * tn * osz) + k * tn * 2 + tm * tn * 4
    if shift:
        in_specs.append(pl.BlockSpec((k, LANES), lambda j, i: (0, (base + (j + 1) * tn) // LANES)))
        args.append(w)
        vmem += 2 * k * LANES * 4
    if residual is not None:
        in_specs.append(pl.BlockSpec((tm, tn), lambda j, i: (i, j)))
        args.append(residual)
        vmem += 2 * tm * tn * residual.dtype.itemsize
    return pl.pallas_call(
        functools.partial(_mm_w32_kernel, shift=shift, has_residual=residual is not None),
        grid=(n // tn, m // tm),
        in_specs=in_specs,
        out_specs=pl.BlockSpec((tm, tn), lambda j, i: (i, j)),
        out_shape=jax.ShapeDtypeStruct((m, n), out_dtype),
        scratch_shapes=[pltpu.VMEM((k, tn), _BF16)],
        compiler_params=_params(("parallel", "arbitrary"), vmem),
        name=name,
    )(*args)


def _log_sigmoid(z):
    return jnp.minimum(z, 0.0) - jnp.log(1.0 + jnp.exp(-jnp.abs(z)))


def _cumsum_rows(x):
    c, n = x.shape
    row = lax.broadcasted_iota(jnp.int32, (c, c), 0)
    col = lax.broadcasted_iota(jnp.int32, (c, c), 1)
    tri = jnp.where(row >= col, 1.0, 0.0).astype(_BF16)
    hi = x.astype(_BF16)
    rest = x - hi.astype(_F32)
    mid = rest.astype(_BF16)
    lo = (rest - mid.astype(_F32)).astype(_BF16)
    parts = jnp.dot(tri, jnp.concatenate([hi, mid, lo], axis=1), preferred_element_type=_F32)
    return parts[:, :n] + parts[:, n:2 * n] + parts[:, 2 * n:]


def _gla_chunk(q, k, v, b, state):
    c = q.shape[0]
    row = lax.broadcasted_iota(jnp.int32, (c, c), 0)
    col = lax.broadcasted_iota(jnp.int32, (c, c), 1)
    b_last = b[c - 1:c, :]

    attn = jnp.zeros((c, c), _F32)
    row1 = lax.broadcasted_iota(jnp.int32, (c, 1), 0)
    s = c // 2
    while s >= GLA_SUB:
        pieces = []
        for lo in range(0, c, 2 * s):
            mid = lo + s
            ref_row = b[mid - 1:mid, :]
            pieces.append(ref_row - b[lo:mid, :])
            pieces.append(b[mid:mid + s, :] - ref_row)
        x = jnp.exp2(jnp.concatenate(pieces, axis=0))
        upper = ((row1 >> _log2(s)) & 1) == 1
        qt = jnp.where(upper, q * x, 0.0).astype(_BF16)
        kt = jnp.where(upper, 0.0, k * x).astype(_BF16)
        same = (row >> _log2(2 * s)) == (col >> _log2(2 * s))
        attn = attn + jnp.where(same, _nt(qt, kt), 0.0)
        s //= 2

    sub_row = lax.broadcasted_iota(jnp.int32, (GLA_SUB, 1), 0)
    lane = lax.broadcasted_iota(jnp.int32, (GLA_SUB, c), 1)
    diag = []
    for i0 in range(0, c, GLA_SUB):
        qi, ki, bi = q[i0:i0 + GLA_SUB, :], k[i0:i0 + GLA_SUB, :], b[i0:i0 + GLA_SUB, :]
        blk = jnp.zeros((GLA_SUB, c), _F32)
        for j in range(GLA_SUB):
            dec = jnp.exp2(jnp.where(sub_row >= j, bi - bi[j:j + 1, :], _NEG_INF))
            sc = jnp.sum(qi * dec * ki[j:j + 1, :], axis=1, keepdims=True)
            blk = jnp.where(lane == i0 + j, sc, blk)
        diag.append(blk)
    attn = attn + jnp.concatenate(diag, axis=0)

    qb = (q * jnp.exp2(b)).astype(_BF16)
    o = _nt(qb, state.astype(_BF16)) + jnp.dot(attn.astype(_BF16), v, preferred_element_type=_F32)
    kd = (k * jnp.exp2(b_last - b)).astype(_BF16)
    return o, state * jnp.exp2(b_last) + _tn(v, kd)


def _gla_kernel(q_ref, k_ref, v_ref, r_ref, glr_ref, wg_ref, bias_ref, gain_ref, o_ref, state_ref):
    @pl.when(pl.program_id(2) == 0)
    def _():
        state_ref[...] = jnp.zeros_like(state_ref)

    z = jnp.dot(glr_ref[...], wg_ref[...], preferred_element_type=_F32) + bias_ref[...]
    b_all = _cumsum_rows(_log_sigmoid(z) * (_LOG2_E / GLA_TAU))
    for h in range(GLA_STEP_HEADS):
        ks = slice(h * GLA_DK, (h + 1) * GLA_DK)
        vs = slice(h * GLA_DV, (h + 1) * GLA_DV)
        q = q_ref[:, ks].astype(_F32) * (GLA_DK ** -0.5)
        o, state_ref[h] = _gla_chunk(q, k_ref[:, ks].astype(_F32), v_ref[:, vs], b_all[:, ks], state_ref[h])
        ms = jnp.mean(o * o, axis=-1, keepdims=True)
        on = o * lax.rsqrt(ms + RMS_EPS) * gain_ref[:, vs]
        r = r_ref[:, vs].astype(_F32)
        o_ref[:, vs] = (on * (r * jax.nn.sigmoid(r))).astype(o_ref.dtype)


def _gla(qkvr, glr, wg, bias, gain, batch, seq, chunk=128):
    t = batch * seq
    nc = seq // chunk
    hs = GLA_STEP_HEADS
    wk, wv = hs * GLA_DK, hs * GLA_DV
    groups = GLA_HEADS // hs
    kv = 2 * groups * wk // wv
    row = lambda b, h, c: b * nc + c
    in_specs = [
        pl.BlockSpec((chunk, wk), lambda b, h, c: (row(b, h, c), h)),
        pl.BlockSpec((chunk, wk), lambda b, h, c: (row(b, h, c), groups + h)),
        pl.BlockSpec((chunk, wv), lambda b, h, c: (row(b, h, c), kv + h)),
        pl.BlockSpec((chunk, wv), lambda b, h, c: (row(b, h, c), kv + groups + h)),
        pl.BlockSpec((chunk, LANES), lambda b, h, c: (row(b, h, c), 0)),
        pl.BlockSpec((LANES, wk), lambda b, h, c: (0, h)),
        pl.BlockSpec((1, wk), lambda b, h, c: (0, h)),
        pl.BlockSpec((1, wv), lambda b, h, c: (0, h)),
    ]
    return pl.pallas_call(
        _gla_kernel,
        grid=(batch, groups, nc),
        in_specs=in_specs,
        out_specs=pl.BlockSpec((chunk, wv), lambda b, h, c: (row(b, h, c), h)),
        out_shape=jax.ShapeDtypeStruct((t, GLA_HEADS * GLA_DV), _BF16),
        scratch_shapes=[pltpu.VMEM((hs, GLA_DV, GLA_DK), _F32)],
        compiler_params=_params(("parallel", "parallel", "arbitrary"), 24 << 20),
        name="gla",
    )(qkvr, qkvr, qkvr, qkvr, glr, wg, bias, gain)


def _pair_rmsnorm(x, gain2):
    lane = lax.broadcasted_iota(jnp.int32, x.shape, 1)
    low = lane < SWA_HD
    sq = x * x
    s_all = jnp.sum(sq, axis=1, keepdims=True)
    s_low = jnp.sum(jnp.where(low, sq, 0.0), axis=1, keepdims=True)
    ms = jnp.where(low, s_low, s_all - s_low) * (1.0 / SWA_HD)
    return x * lax.rsqrt(ms + RMS_EPS) * gain2


def _swa_kernel(sinks_ref, q_ref, kc_ref, kp_ref, vc_ref, vp_ref, qg_ref, kg_ref, o_ref):
    blk = pl.program_id(1)
    bs = SWA_BLOCK
    group = SWA_HEADS // SWA_KV_HEADS
    pairs = group // 2
    lane = lax.broadcasted_iota(jnp.int32, (bs, LANES), 1)
    low = lane < SWA_HD

    q_loc = lax.broadcasted_iota(jnp.int32, (group * bs, bs), 0) & (bs - 1)
    from_cur = lax.broadcasted_iota(jnp.int32, (group * bs, bs), 1) <= q_loc
    prev_bias = jnp.where(blk > 0, 0.0, _NEG_INF)

    for p in range(SWA_KV_HEADS // 2):
        cs = slice(p * LANES, (p + 1) * LANES)
        tiles = [_pair_rmsnorm(kc_ref[:, cs].astype(_F32), kg_ref[...]),
                 _pair_rmsnorm(kp_ref[:, cs].astype(_F32), kg_ref[...]),
                 vc_ref[:, cs].astype(_F32), vp_ref[:, cs].astype(_F32)]
        swapped = [pltpu.roll(x, SWA_HD, axis=1) for x in tiles]
        for half in range(2):
            h = 2 * p + half
            kc, kp, vc, vp = [(jnp.where(low, x, xs) if half == 0 else jnp.where(low, xs, x)).astype(_BF16)
                              for x, xs in zip(tiles, swapped)]
            qs, sink = [], []
            for t in range(pairs):
                c0 = h * group * SWA_HD + t * LANES
                qn = _pair_rmsnorm(q_ref[:, c0:c0 + LANES].astype(_F32), qg_ref[...]) * (SWA_HD ** -0.5)
                qs.append(jnp.where(low, qn, 0.0))
                qs.append(jnp.where(low, 0.0, qn))
                sink.append(jnp.full((bs, 1), sinks_ref[h * group + 2 * t], _F32))
                sink.append(jnp.full((bs, 1), sinks_ref[h * group + 2 * t + 1], _F32))
            qstack = jnp.concatenate(qs, axis=0).astype(_BF16)
            sink = jnp.concatenate(sink, axis=0)
            s = jnp.where(from_cur, _nt(qstack, kc), _nt(qstack, kp) + prev_bias)
            m = jnp.maximum(jnp.max(s, axis=1, keepdims=True), sink)
            e = jnp.exp(s - m)
            denom = jnp.sum(e, axis=1, keepdims=True) + jnp.exp(sink - m)
            e_cur = jnp.where(from_cur, e, 0.0).astype(_BF16)
            e_prev = jnp.where(from_cur, 0.0, e).astype(_BF16)
            o = (jnp.dot(e_cur, vc, preferred_element_type=_F32)
                 + jnp.dot(e_prev, vp, preferred_element_type=_F32)) / denom
            for t in range(pairs):
                c0 = h * group * SWA_HD + t * LANES
                oa = o[(2 * t) * bs:(2 * t + 1) * bs, :]
                ob = o[(2 * t + 1) * bs:(2 * t + 2) * bs, :]
                o_ref[:, c0:c0 + LANES] = jnp.where(low, oa, ob).astype(o_ref.dtype)


def _swa(sq, skv, sinks, q_gain, k_gain, batch, seq, q_blk=0, k_blk=0):
    t = batch * seq
    nb = seq // SWA_BLOCK
    kvw = SWA_KV_HEADS * SWA_HD
    cur_q = lambda b, i, s: (b * nb + i, q_blk)
    cur = lambda b, i, s: (b * nb + i, k_blk)
    prev = lambda b, i, s: (b * nb + jnp.maximum(i - 1, 0), k_blk)
    cur_v = lambda b, i, s: (b * nb + i, k_blk + 1)
    prev_v = lambda b, i, s: (b * nb + jnp.maximum(i - 1, 0), k_blk + 1)
    const = lambda b, i, s: (0, 0)
    gs = pltpu.PrefetchScalarGridSpec(
        num_scalar_prefetch=1,
        grid=(batch, nb),
        in_specs=[
            pl.BlockSpec((SWA_BLOCK, SWA_HEADS * SWA_HD), cur_q),
            pl.BlockSpec((SWA_BLOCK, kvw), cur),
            pl.BlockSpec((SWA_BLOCK, kvw), prev),
            pl.BlockSpec((SWA_BLOCK, kvw), cur_v),
            pl.BlockSpec((SWA_BLOCK, kvw), prev_v),
            pl.BlockSpec((1, LANES), const),
            pl.BlockSpec((1, LANES), const),
        ],
        out_specs=pl.BlockSpec((SWA_BLOCK, SWA_HEADS * SWA_HD), lambda b, i, s: (b * nb + i, 0)),
    )
    g2 = lambda g: jnp.concatenate([g, g]).reshape(1, LANES).astype(_F32)
    return pl.pallas_call(
        _swa_kernel,
        grid_spec=gs,
        out_shape=jax.ShapeDtypeStruct((t, SWA_HEADS * SWA_HD), _BF16),
        compiler_params=_params(("parallel", "arbitrary"), 24 << 20),
        name="swa",
    )(sinks.astype(_F32), sq, skv, skv, skv, skv, g2(q_gain), g2(k_gain))


def _memattn_kernel(q_ref, k_ref, v_ref, g_ref, o_ref):
    hd = q_ref.shape[1]
    q = q_ref[...].astype(_F32)
    ms = jnp.mean(q * q, axis=-1, keepdims=True)
    qn = (q * lax.rsqrt(ms + RMS_EPS) * g_ref[...] * (hd ** -0.5)).astype(_BF16)
    s = _nt(qn, k_ref[...])
    m = jnp.max(s, axis=-1, keepdims=True)
    e = jnp.exp(s - m)
    denom = jnp.sum(e, axis=-1, keepdims=True)
    o = jnp.dot(e.astype(_BF16), v_ref[...], preferred_element_type=_F32) / denom
    o_ref[...] = o.astype(o_ref.dtype)


def _memattn(mq, kc, vm, q_gain, batch, seq, mem_len, d, q_blk=0, v_blk=0, tm=1024):
    t = mq.shape[0]
    tm = _tile(seq, tm)
    ns = seq // tm
    hd = d // MEM_HEADS
    return pl.pallas_call(
        _memattn_kernel,
        grid=(batch, ns, MEM_HEADS),
        in_specs=[
            pl.BlockSpec((tm, hd), lambda b, i, h: (b * ns + i, q_blk + h)),
            pl.BlockSpec((mem_len, hd), lambda b, i, h: (b, h)),
            pl.BlockSpec((mem_len, hd), lambda b, i, h: (b, v_blk + h)),
            pl.BlockSpec((1, hd), lambda b, i, h: (0, 0)),
        ],
        out_specs=pl.BlockSpec((tm, hd), lambda b, i, h: (b * ns + i, h)),
        out_shape=jax.ShapeDtypeStruct((t, d), _BF16),
        compiler_params=_params(("parallel", "parallel", "arbitrary"),
                                4 * tm * hd * 2 + 4 * mem_len * hd * 2 + 6 * tm * hd * 4),
        name="memattn",
    )(mq, kc, vm, q_gain.reshape(1, hd).astype(_F32))


def _merge_kernel(a_ref, b_ref, c_ref, wa_ref, wb_ref, wc_ref, ga_ref, gb_ref, gc_ref, o_ref):
    def branch(x_ref, w_ref, g_ref):
        y = jnp.dot(x_ref[...], w_ref[...], preferred_element_type=_F32)
        return jax.nn.sigmoid(g_ref[...].astype(_F32)) * y

    o_ref[...] = (branch(a_ref, wa_ref, ga_ref) + branch(b_ref, wb_ref, gb_ref)
                  + branch(c_ref, wc_ref, gc_ref)).astype(o_ref.dtype)


def _merge(oa, ob, oc, wa, wb, wc, gate, gate_col=0, tm=512, tn=512):
    t, d = oa.shape
    n = wa.shape[1]
    tm, tn = _tile(t, tm), _tile(n, tn)
    nn = n // tn
    if gate_col % tn:
        gate, gate_col = gate[:, gate_col:gate_col + 3 * n], 0
    g0 = gate_col // tn
    act = pl.BlockSpec((tm, d), lambda j, i: (i, 0))
    wsp = pl.BlockSpec((d, tn), lambda j, i: (0, j), pipeline_mode=pl.Buffered(1))
    gsp = lambda b: pl.BlockSpec((tm, tn), lambda j, i: (i, g0 + b * nn + j))
    vmem = 3 * (2 * tm * d * 2 + d * tn * 2 + 2 * tm * tn * 2) + 8 * tm * tn * 4
    return pl.pallas_call(
        _merge_kernel,
        grid=(nn, t // tm),
        in_specs=[act, act, act, wsp, wsp, wsp, gsp(0), gsp(1), gsp(2)],
        out_specs=pl.BlockSpec((tm, tn), lambda j, i: (i, j)),
        out_shape=jax.ShapeDtypeStruct((t, n), _BF16),
        compiler_params=_params(("parallel", "arbitrary"), vmem),
        name="merge",
    )(oa, ob, oc, wa, wb, wc, gate, gate, gate)


def _extract_topk(s, payload, kk):
    n = s.shape[0]
    pos = lax.broadcasted_iota(jnp.int32, s.shape, 0).astype(_F32)
    vals, sel = [], []
    for _ in range(kk):
        m = jnp.max(s, axis=0, keepdims=True)
        first = jnp.min(jnp.where(s == m, pos, float(n)), axis=0, keepdims=True)
        hit = pos == first
        vals.append(m)
        sel.append(first if payload is None else jnp.max(jnp.where(hit, payload, -1.0), axis=0, keepdims=True))
        s = jnp.where(hit, _NEG_INF, s)
    return jnp.concatenate(vals, axis=0), jnp.concatenate(sel, axis=0)


def _route_kernel(q_ref, keys_ref, i1_ref, i2_ref, g_ref):
    kk = PEER_TOPK
    half = PEER_DQ // 2
    experts, gates = [], []
    for h in range(PEER_HEADS):
        top = []
        for p in range(2):
            c0 = (2 * h + p) * half
            st = _nt(keys_ref[2 * h + p].astype(_BF16), q_ref[:, c0:c0 + half].astype(_BF16))
            top.append(_extract_topk(st, None, kk))
        (s0, i0), (s1, i1) = top
        cand, ids = [], []
        tail = kk // 2
        for a in range(tail):
            nb = kk // (a + 1)
            rows = -(-nb // 8) * 8
            c = s0[a:a + 1, :] + s1[0:rows, :]
            if rows != nb:
                c = jnp.where(lax.broadcasted_iota(jnp.int32, c.shape, 0) < nb, c, _NEG_INF)
            cand.append(c)
            ids.append(i0[a:a + 1, :] * PEER_NKEYS + i1[0:rows, :])
        cand.append(s0[tail:kk, :] + s1[0:1, :])
        ids.append(i0[tail:kk, :] * PEER_NKEYS + i1[0:1, :])
        best, e = _extract_topk(jnp.concatenate(cand, axis=0), jnp.concatenate(ids, axis=0), kk)
        ex = jnp.exp(best - best[0:1, :])
        gates.append(ex / jnp.sum(ex, axis=0, keepdims=True))
        experts.append(e)
    e = jnp.concatenate(experts, axis=0).T.astype(jnp.int32)
    i1_ref[...] = e >> _log2(PEER_NKEYS)
    i2_ref[...] = e & (PEER_NKEYS - 1)
    g_ref[...] = jnp.concatenate(gates, axis=0).T


def _route(pq, sub_keys, tt=256):
    t = pq.shape[0]
    tt = _tile(t, tt)
    j = PEER_HEADS * PEER_TOPK
    half = PEER_DQ // 2
    keys = sub_keys.reshape(PEER_HEADS * 2, PEER_NKEYS, half)
    out = pl.BlockSpec((tt, j), lambda i: (i, 0))
    return pl.pallas_call(
        _route_kernel,
        grid=(t // tt,),
        in_specs=[pl.BlockSpec((tt, PEER_HEADS * PEER_DQ), lambda i: (i, 0)),
                  pl.BlockSpec(keys.shape, lambda i: (0, 0, 0))],
        out_specs=[out, out, out],
        out_shape=[jax.ShapeDtypeStruct((t, j), jnp.int32), jax.ShapeDtypeStruct((t, j), jnp.int32),
                   jax.ShapeDtypeStruct((t, j), _F32)],
        compiler_params=_params(("parallel",), 24 << 20),
        name="peer_route",
    )(pq, keys)


def _gatemat_kernel(i1_ref, i2_ref, g_ref, w_ref):
    nk = PEER_NKEYS
    sub = lax.broadcasted_iota(jnp.int32, (nk, i1_ref.shape[1]), 0)

    def body(t, carry):
        a_t = jnp.where(sub == i1_ref[pl.ds(t, 1), :], 1.0, 0.0).astype(_BF16)
        b_t = jnp.where(sub == i2_ref[pl.ds(t, 1), :], g_ref[pl.ds(t, 1), :], 0.0).astype(_BF16)
        w_ref[t] = _nt(a_t, b_t)
        return carry

    lax.fori_loop(0, i1_ref.shape[0], body, 0, unroll=8)


def _gatemat(i1, i2, g, tt=64):
    t, j = i1.shape
    tt = _tile(t, tt)
    nk = PEER_NKEYS
    spec = pl.BlockSpec((tt, j), lambda i: (i, 0))
    return pl.pallas_call(
        _gatemat_kernel,
        grid=(t // tt,),
        in_specs=[spec, spec, spec],
        out_specs=pl.BlockSpec((tt, nk, nk), lambda i: (i, 0, 0)),
        out_shape=jax.ShapeDtypeStruct((t, nk, nk), _F32),
        compiler_params=_params(("parallel",), 2 * tt * nk * nk * 4 + (8 << 20)),
        name="peer_gatemat",
    )(i1, i2, g)


def _gelu(x):
    return 0.5 * x * (1.0 + lax.erf(x * (2.0 ** -0.5)))


def _peer_act_kernel(x_ref, u_ref, w_ref, s_ref):
    act = _gelu(_nt(x_ref[...], u_ref[...]))
    w = jnp.transpose(w_ref[...], (1, 0, 2))
    for a in range(w.shape[0]):
        cs = slice(a * PEER_NKEYS, (a + 1) * PEER_NKEYS)
        s_ref[:, cs] = (act[:, cs] * w[a]).astype(s_ref.dtype)


def _peer_act(xn, u, w, tm=512, rows_a=8):
    t, d = xn.shape
    ne = u.shape[0]
    nk = PEER_NKEYS
    tm = _tile(t, tm)
    eb = rows_a * nk
    vmem = 2 * (tm * d * 2 + eb * d * 2 + tm * rows_a * nk * 4 + tm * eb * 2) + 3 * tm * eb * 4
    return pl.pallas_call(
        _peer_act_kernel,
        grid=(ne // eb, t // tm),
        in_specs=[pl.BlockSpec((tm, d), lambda j, i: (i, 0)),
                  pl.BlockSpec((eb, d), lambda j, i: (j, 0)),
                  pl.BlockSpec((tm, rows_a, nk), lambda j, i: (i, j, 0))],
        out_specs=pl.BlockSpec((tm, eb), lambda j, i: (i, j)),
        out_shape=jax.ShapeDtypeStruct((t, ne), _BF16),
        compiler_params=_params(("parallel", "parallel"), vmem),
        name="peer_act",
    )(xn, u, w)


def _mm_acc_kernel(x_ref, w_ref, r_ref, o_ref):
    acc = jnp.dot(x_ref[...], w_ref[...], preferred_element_type=_F32)

    @pl.when(pl.program_id(2) == 0)
    def _():
        o_ref[...] = r_ref[...] + acc

    @pl.when(pl.program_id(2) > 0)
    def _():
        o_ref[...] += acc


def _matmul_acc(x, w, residual, tm=1024, tn=1024, tk=4096):
    m, k = x.shape
    n = w.shape[1]
    tm, tn, tk = _tile(m, tm), _tile(n, tn), _tile(k, tk)
    vmem = 2 * (tm * tk * 2 + tk * tn * 2 + 2 * tm * tn * 4) + tm * tn * 4
    return pl.pallas_call(
        _mm_acc_kernel,
        grid=(m // tm, n // tn, k // tk),
        in_specs=[pl.BlockSpec((tm, tk), lambda i, j, l: (i, l)),
                  pl.BlockSpec((tk, tn), lambda i, j, l: (l, j)),
                  pl.BlockSpec((tm, tn), lambda i, j, l: (i, j))],
        out_specs=pl.BlockSpec((tm, tn), lambda i, j, l: (i, j)),
        out_shape=jax.ShapeDtypeStruct((m, n), _F32),
        compiler_params=_params(("parallel", "parallel", "arbitrary"), vmem),
        name="peer_out",
    )(x, w, residual)


def _layer(h, mem, norm_mix, w_in, gla_w_gate_up, gla_gate_bias, gla_out_norm, swa_q_norm, swa_k_norm,
           swa_sinks, mem_norm, w_mem_kv, mem_q_norm, mem_k_norm, w_branch_gla, w_branch_swa,
           w_branch_mem, w_out, norm_ffn, peer_w_q, peer_sub_keys, peer_u, peer_v, batch, seq):
    t, d = h.shape
    mem_len = mem.shape[0] // batch
    qk = GLA_HEADS * GLA_DK
    gv = GLA_HEADS * GLA_DV
    swq = SWA_HEADS * SWA_HD
    skv = SWA_KV_HEADS * SWA_HD
    o_glr = 2 * qk + 2 * gv
    o_sq = o_glr + GLA_RANK
    o_sk = o_sq + swq
    o_mq = o_sk + 2 * skv
    o_gate = o_mq + d
    bf = lambda w: w.astype(_BF16)
    hd = d // MEM_HEADS

    xn = _rmsnorm(h, norm_mix, _BF16)
    qkvr = _matmul_w32(xn, w_in, 0, o_glr, _BF16, name="proj_gla")
    glr = _matmul_w32(xn, w_in, o_glr, LANES, _BF16, name="proj_glr")
    tail = _matmul_w32(xn, w_in, o_sq, w_in.shape[1] - o_sq, _BF16, name="proj_tail")

    def cols(off, width, blk):
        return (tail, off // blk) if off % blk == 0 else (tail[:, off:off + width], 0)

    wg = bf(jnp.pad(gla_w_gate_up, ((0, LANES - GLA_RANK), (0, 0))))
    o_a = _gla(qkvr, glr, wg, gla_gate_bias.reshape(1, qk).astype(_F32),
               gla_out_norm.reshape(1, gv).astype(_F32), batch, seq)

    sq, q_blk = cols(0, swq, swq)
    skvp, k_blk = cols(swq, 2 * skv, skv)
    o_b = _swa(sq, skvp, swa_sinks, swa_q_norm, swa_k_norm, batch, seq, q_blk, k_blk)

    memn = _rmsnorm(mem, mem_norm, _BF16)
    kv_m = _matmul_w32(memn, w_mem_kv, 0, 2 * d, _BF16, name="mem_kv")
    k_c = _rmsnorm(kv_m[:, :d].reshape(-1, hd), mem_k_norm, _BF16).reshape(-1, d)
    mq, mq_blk = cols(swq + 2 * skv, d, hd)
    o_c = _memattn(mq, k_c, kv_m, mem_q_norm, batch, seq, mem_len, d, mq_blk, MEM_HEADS)

    mix = _merge(o_a, o_b, o_c, bf(w_branch_gla), bf(w_branch_swa), bf(w_branch_mem), tail,
                 swq + 2 * skv + d)
    h = _matmul_w32(mix, w_out, 0, d, _F32, residual=h, name="out_proj")

    hn = _rmsnorm(h, norm_ffn, _BF16)
    pq = _matmul_w32(hn, peer_w_q, 0, peer_w_q.shape[1], _F32, name="peer_q")
    i1, i2, g = _route(pq, peer_sub_keys.astype(_F32))
    w = _gatemat(i1, i2, g)
    s = _peer_act(hn, bf(peer_u), w)
    return _matmul_acc(s, bf(peer_v), h)


def kernel(x, mem, norm_mix, w_in, gla_w_gate_up, gla_gate_bias, gla_out_norm, swa_q_norm, swa_k_norm,
           swa_sinks, mem_norm, w_mem_kv, mem_q_norm, mem_k_norm, w_branch_gla, w_branch_swa, w_branch_mem,
           w_out, norm_ffn, peer_w_q, peer_sub_keys, peer_u, peer_v):
    batch, seq, d = x.shape
    h = x.reshape(batch * seq, d)
    memf = mem.reshape(-1, d)
    for l in range(norm_mix.shape[0]):
        h = _layer(h, memf, norm_mix[l], w_in[l], gla_w_gate_up[l], gla_gate_bias[l], gla_out_norm[l],
                   swa_q_norm[l], swa_k_norm[l], swa_sinks[l], mem_norm[l], w_mem_kv[l], mem_q_norm[l],
                   mem_k_norm[l], w_branch_gla[l], w_branch_swa[l], w_branch_mem[l], w_out[l], norm_ffn[l],
                   peer_w_q[l], peer_sub_keys[l], peer_u[l], peer_v[l], batch, seq)
    return h.reshape(batch, seq, d)
```

```python
import functools

import jax
import jax.numpy as jnp
from jax import lax
from jax.experimental import pallas as pl
from jax.experimental.pallas import tpu as pltpu

_F32 = jnp.float32
_BF16 = jnp.bfloat16
_NEG_INF = float("-inf")

RMS_EPS = 1e-6
GLA_HEADS = 8
GLA_DK = 256
GLA_DV = 512
GLA_RANK = 16
GLA_TAU = 16.0
GLA_SUB = 8
GLA_STEP_HEADS = 2
_LOG2_E = 1.4426950408889634
SWA_HEADS = 64
SWA_KV_HEADS = 8
SWA_HD = 64
SWA_BLOCK = 128
MEM_HEADS = 4
PEER_HEADS = 8
PEER_NKEYS = 128
PEER_DQ = 256
PEER_TOPK = 16

LANES = 128
V7X_VMEM_BYTES = 64 * 1024 * 1024


def _tile(n, pref):
    t = min(n, pref)
    while n % t:
        t -= 1
    return t


def _params(semantics, vmem_bytes):
    limit = min(int(vmem_bytes) + (8 << 20), V7X_VMEM_BYTES - (4 << 20))
    return pltpu.CompilerParams(dimension_semantics=semantics, vmem_limit_bytes=limit)


def _nt(a, b):
    return lax.dot_general(a, b, (((1,), (1,)), ((), ())), preferred_element_type=_F32)


def _log2(n):
    assert n > 0 and n & (n - 1) == 0, n
    return n.bit_length() - 1


def _tn(a, b):
    return lax.dot_general(a, b, (((0,), (0,)), ((), ())), preferred_element_type=_F32)


def _rmsnorm_kernel(x_ref, g_ref, o_ref):
    x = x_ref[...].astype(_F32)
    ms = jnp.mean(x * x, axis=-1, keepdims=True)
    o_ref[...] = (x * lax.rsqrt(ms + RMS_EPS) * g_ref[...]).astype(o_ref.dtype)


def _rmsnorm(x, gain, out_dtype, rows=256):
    r, d = x.shape
    tr = _tile(r, rows)
    blk = tr * d * (x.dtype.itemsize + jnp.dtype(out_dtype).itemsize)
    return pl.pallas_call(
        _rmsnorm_kernel,
        grid=(r // tr,),
        in_specs=[pl.BlockSpec((tr, d), lambda i: (i, 0)), pl.BlockSpec((1, d), lambda i: (0, 0))],
        out_specs=pl.BlockSpec((tr, d), lambda i: (i, 0)),
        out_shape=jax.ShapeDtypeStruct((r, d), out_dtype),
        compiler_params=_params(("parallel",), 2 * blk + 4 * tr * d * 4),
        name="rmsnorm",
    )(x, gain.reshape(1, d).astype(_F32))


def _mm_kernel(x_ref, w_ref, o_ref):
    o_ref[...] = jnp.dot(x_ref[...], w_ref[...], preferred_element_type=_F32).astype(o_ref.dtype)


def _mm_res_kernel(x_ref, w_ref, r_ref, o_ref):
    acc = jnp.dot(x_ref[...], w_ref[...], preferred_element_type=_F32)
    o_ref[...] = (acc + r_ref[...].astype(_F32)).astype(o_ref.dtype)


def _matmul(x, w, out_dtype, tm=1024, tn=512, residual=None, name="matmul"):
    m, k = x.shape
    n = w.shape[1]
    tm, tn = _tile(m, tm), _tile(n, tn)
    osz = jnp.dtype(out_dtype).itemsize
    vmem = 2 * (tm * k * x.dtype.itemsize + k * tn * w.dtype.itemsize + tm * tn * osz) + tm * tn * 4
    in_specs = [pl.BlockSpec((tm, k), lambda j, i: (i, 0)), pl.BlockSpec((k, tn), lambda j, i: (0, j))]
    args = [x, w]
    kern = _mm_kernel
    if residual is not None:
        in_specs.append(pl.BlockSpec((tm, tn), lambda j, i: (i, j)))
        args.append(residual)
        kern = _mm_res_kernel
        vmem += 2 * tm * tn * residual.dtype.itemsize
    return pl.pallas_call(
        kern,
        grid=(n // tn, m // tm),
        in_specs=in_specs,
        out_specs=pl.BlockSpec((tm, tn), lambda j, i: (i, j)),
        out_shape=jax.ShapeDtypeStruct((m, n), out_dtype),
        compiler_params=_params(("parallel", "parallel"), vmem),
        name=name,
    )(*args)


class _Side:
    def __init__(self, kernel, parts, args, in_specs, out_specs, out_shape, scratch_shapes=(), vmem=0):
        self.kernel, self.parts, self.args, self.in_specs = kernel, parts, list(args), list(in_specs)
        self.out_specs, self.out_shape = list(out_specs), list(out_shape)
        self.scratch_shapes, self.vmem = list(scratch_shapes), vmem


def _mm_w32_kernel(*refs, shift, has_residual, side):
    n_in = 2 + bool(shift) + has_residual
    n_side_in = len(side.in_specs) if side else 0
    n_side_out = len(side.out_specs) if side else 0
    x_ref, w_ref = refs[0], refs[1]
    o_ref = refs[n_in + n_side_in]
    wbf_ref = refs[n_in + n_side_in + 1 + n_side_out]
    tn = o_ref.shape[1]

    @pl.when(pl.program_id(1) == 0)
    def _():
        rows = _tile(w_ref.shape[0], 512)
        for r0 in range(0, w_ref.shape[0], rows):
            rs = slice(r0, r0 + rows)
            w = w_ref[rs, :]
            if shift:
                w = jnp.concatenate([w, refs[2][rs, :]], axis=1)
                w = pltpu.roll(w, w.shape[1] - shift, axis=1)[:, :tn]
            wbf_ref[rs, :] = w.astype(_BF16)

    def matmul_rows(rs):
        acc = jnp.dot(x_ref[rs, :], wbf_ref[...], preferred_element_type=_F32)
        if has_residual:
            acc = acc + refs[n_in - 1][rs, :].astype(_F32)
        o_ref[rs, :] = acc.astype(o_ref.dtype)

    if side:
        rows = x_ref.shape[0] // side.parts
        parts = [functools.partial(matmul_rows, slice(r * rows, (r + 1) * rows)) for r in range(side.parts)]
        step = pl.program_id(0) * pl.num_programs(1) + pl.program_id(1)
        side.kernel(step, parts, *refs[n_in:n_in + n_side_in],
                    *refs[n_in + n_side_in + 1:n_in + n_side_in + 1 + n_side_out],
                    *refs[n_in + n_side_in + 2 + n_side_out:])
    else:
        matmul_rows(slice(None))


def _matmul_w32(x, w, layer, windows, out_dtype, tm=1024, tn=512, residual=None, side=None,
                name="matmul_w32"):
    m, k = x.shape
    shift = windows[0][0] % LANES
    assert all(c0 % LANES == shift for c0, _ in windows)
    tm = _tile(m, tm)
    ok = lambda t: all(n % t == 0 and (c0 - shift) % t == 0 for c0, n in windows)
    tn = max(t for t in range(LANES, tn + 1, LANES) if ok(t))
    n = sum(n for _, n in windows)

    def col_block(j, width):
        blk, start = 0, 0
        for c0, wn in windows:
            first = ((c0 - shift) // tn - start) * (tn // width)
            blk = jnp.where(j >= start, first + j * (tn // width), blk)
            start += wn // tn
        return blk

    osz = jnp.dtype(out_dtype).itemsize
    in_specs = [pl.BlockSpec((tm, k), lambda j, i: (i, 0)),
                pl.BlockSpec((None, k, tn), lambda j, i: (layer, 0, col_block(j, tn)))]
    args = [x, w]
    vmem = 2 * (tm * k * 2 + k * tn * 4 + tm * tn * osz) + k * tn * 2 + tm * tn * 4
    if shift:
        in_specs.append(pl.BlockSpec((None, k, LANES),
                                     lambda j, i: (layer, 0, col_block(j, LANES) + tn // LANES)))
        args.append(w)
        vmem += 2 * k * LANES * 4
    if residual is not None:
        in_specs.append(pl.BlockSpec((tm, tn), lambda j, i: (i, j)))
        args.append(residual)
        vmem += 2 * tm * tn * residual.dtype.itemsize
    out_specs = [pl.BlockSpec((tm, tn), lambda j, i: (i, j))]
    out_shape = [jax.ShapeDtypeStruct((m, n), out_dtype)]
    scratch = [pltpu.VMEM((k, tn), _BF16)]
    alone = None
    if side:
        side, steps = side
        if steps != (n // tn) * (m // tm):
            alone, side = _run_side(side, steps, m // tm, name + "_side"), None
    if side:
        in_specs += side.in_specs
        args += side.args
        out_specs += side.out_specs
        out_shape += side.out_shape
        scratch += side.scratch_shapes
        vmem += side.vmem
    out = pl.pallas_call(
        functools.partial(_mm_w32_kernel, shift=shift, has_residual=residual is not None, side=side),
        grid=(n // tn, m // tm),
        in_specs=in_specs,
        out_specs=out_specs,
        out_shape=out_shape,
        scratch_shapes=scratch,
        compiler_params=_params(("arbitrary", "arbitrary"), vmem),
        name=name,
    )(*args)
    if alone is not None:
        return [out[0], *alone]
    return out if side else out[0]


def _log_sigmoid(z):
    return jnp.minimum(z, 0.0) - jnp.log(1.0 + jnp.exp(-jnp.abs(z)))


def _cumsum_rows(x):
    c, n = x.shape
    row = lax.broadcasted_iota(jnp.int32, (c, c), 0)
    col = lax.broadcasted_iota(jnp.int32, (c, c), 1)
    tri = jnp.where(row >= col, 1.0, 0.0).astype(_BF16)
    hi = x.astype(_BF16)
    rest = x - hi.astype(_F32)
    mid = rest.astype(_BF16)
    lo = (rest - mid.astype(_F32)).astype(_BF16)
    parts = jnp.dot(tri, jnp.concatenate([hi, mid, lo], axis=1), preferred_element_type=_F32)
    return parts[:, :n] + parts[:, n:2 * n] + parts[:, 2 * n:]


def _gla_chunk(q, k, v, b, state, midway):
    c = q.shape[0]
    row = lax.broadcasted_iota(jnp.int32, (c, c), 0)
    col = lax.broadcasted_iota(jnp.int32, (c, c), 1)
    b_last = b[c - 1:c, :]

    attn = jnp.zeros((c, c), _F32)
    row1 = lax.broadcasted_iota(jnp.int32, (c, 1), 0)
    s = c // 2
    while s >= GLA_SUB:
        pieces = []
        for lo in range(0, c, 2 * s):
            mid = lo + s
            ref_row = b[mid - 1:mid, :]
            pieces.append(ref_row - b[lo:mid, :])
            pieces.append(b[mid:mid + s, :] - ref_row)
        x = jnp.exp2(jnp.concatenate(pieces, axis=0))
        upper = ((row1 >> _log2(s)) & 1) == 1
        qt = jnp.where(upper, q * x, 0.0).astype(_BF16)
        kt = jnp.where(upper, 0.0, k * x).astype(_BF16)
        same = (row >> _log2(2 * s)) == (col >> _log2(2 * s))
        attn = attn + jnp.where(same, _nt(qt, kt), 0.0)
        s //= 2

    midway()
    sub_row = lax.broadcasted_iota(jnp.int32, (GLA_SUB, 1), 0)
    lane = lax.broadcasted_iota(jnp.int32, (GLA_SUB, c), 1)
    diag = []
    for i0 in range(0, c, GLA_SUB):
        qi, ki, bi = q[i0:i0 + GLA_SUB, :], k[i0:i0 + GLA_SUB, :], b[i0:i0 + GLA_SUB, :]
        blk = jnp.zeros((GLA_SUB, c), _F32)
        for j in range(GLA_SUB):
            dec = jnp.exp2(jnp.where(sub_row >= j, bi - bi[j:j + 1, :], _NEG_INF))
            sc = jnp.sum(qi * dec * ki[j:j + 1, :], axis=1, keepdims=True)
            blk = jnp.where(lane == i0 + j, sc, blk)
        diag.append(blk)
    attn = attn + jnp.concatenate(diag, axis=0)

    qb = (q * jnp.exp2(b)).astype(_BF16)
    o = _nt(qb, state.astype(_BF16)) + jnp.dot(attn.astype(_BF16), v, preferred_element_type=_F32)
    kd = (k * jnp.exp2(b_last - b)).astype(_BF16)
    return o, state * jnp.exp2(b_last) + _tn(v, kd)


def _gla_kernel(step, parts, q_ref, k_ref, v_ref, r_ref, glr_ref, wg_ref, bias_ref, gain_ref, o_ref,
                state_ref, *, nc):
    @pl.when(step % nc == 0)
    def _():
        state_ref[...] = jnp.zeros_like(state_ref)

    z = jnp.dot(glr_ref[...], wg_ref[...], preferred_element_type=_F32) + bias_ref[...]
    b_all = _cumsum_rows(_log_sigmoid(z) * (_LOG2_E / GLA_TAU))
    for h in range(GLA_STEP_HEADS):
        parts[2 * h]()
        ks = slice(h * GLA_DK, (h + 1) * GLA_DK)
        vs = slice(h * GLA_DV, (h + 1) * GLA_DV)
        q = q_ref[:, ks].astype(_F32) * (GLA_DK ** -0.5)
        o, state_ref[h] = _gla_chunk(q, k_ref[:, ks].astype(_F32), v_ref[:, vs], b_all[:, ks], state_ref[h],
                                     parts[2 * h + 1])
        ms = jnp.mean(o * o, axis=-1, keepdims=True)
        on = o * lax.rsqrt(ms + RMS_EPS) * gain_ref[:, vs]
        r = r_ref[:, vs].astype(_F32)
        o_ref[:, vs] = (on * (r * jax.nn.sigmoid(r))).astype(o_ref.dtype)


def _run_side(side, steps, ni, name):
    def kern(*refs):
        side.kernel(pl.program_id(0) * pl.num_programs(1) + pl.program_id(1),
                    [lambda: None] * side.parts, *refs)

    return pl.pallas_call(
        kern, grid=(steps // ni, ni), in_specs=side.in_specs, out_specs=side.out_specs,
        out_shape=side.out_shape, scratch_shapes=side.scratch_shapes,
        compiler_params=_params(("arbitrary", "arbitrary"), side.vmem), name=name)(*side.args)


def _gla_side(qkvr, glr, wg, bias, gain, batch, seq, ni, chunk=128):
    t = batch * seq
    nc = seq // chunk
    hs = GLA_STEP_HEADS
    wk, wv = hs * GLA_DK, hs * GLA_DV
    groups = GLA_HEADS // hs
    kv = 2 * groups * wk // wv

    def at(col):
        def index(j, i):
            s = j * ni + i
            return (s // (groups * nc)) * nc + s % nc, col + (s // nc) % groups
        return index

    head = lambda j, i: (0, ((j * ni + i) // nc) % groups)
    in_specs = [
        pl.BlockSpec((chunk, wk), at(0)),
        pl.BlockSpec((chunk, wk), at(groups)),
        pl.BlockSpec((chunk, wv), at(kv)),
        pl.BlockSpec((chunk, wv), at(kv + groups)),
        pl.BlockSpec((chunk, LANES), lambda j, i: (at(0)(j, i)[0], 0)),
        pl.BlockSpec((LANES, wk), head),
        pl.BlockSpec((1, wk), head),
        pl.BlockSpec((1, wv), head),
    ]
    side = _Side(functools.partial(_gla_kernel, nc=nc), 2 * hs, [qkvr, qkvr, qkvr, qkvr, glr, wg, bias, gain], in_specs,
                 [pl.BlockSpec((chunk, wv), at(0))], [jax.ShapeDtypeStruct((t, GLA_HEADS * GLA_DV), _BF16)],
                 [pltpu.VMEM((hs, GLA_DV, GLA_DK), _F32)], vmem=16 << 20)
    return side, batch * groups * nc


def _pair_rmsnorm(x, gain2):
    lane = lax.broadcasted_iota(jnp.int32, x.shape, 1)
    low = lane < SWA_HD
    sq = x * x
    s_all = jnp.sum(sq, axis=1, keepdims=True)
    s_low = jnp.sum(jnp.where(low, sq, 0.0), axis=1, keepdims=True)
    ms = jnp.where(low, s_low, s_all - s_low) * (1.0 / SWA_HD)
    return x * lax.rsqrt(ms + RMS_EPS) * gain2


def _swa_kernel(step, parts, sinks_ref, q_ref, kc_ref, kp_ref, vc_ref, vp_ref, qg_ref, kg_ref, o_ref, *, nb):
    npairs = SWA_KV_HEADS // 2
    blk = (step // npairs) % nb
    p = step % npairs
    bs = SWA_BLOCK
    group = SWA_HEADS // SWA_KV_HEADS
    pairs = group // 2
    lane = lax.broadcasted_iota(jnp.int32, (bs, LANES), 1)
    low = lane < SWA_HD

    q_loc = lax.broadcasted_iota(jnp.int32, (group * bs, bs), 0) & (bs - 1)
    from_cur = lax.broadcasted_iota(jnp.int32, (group * bs, bs), 1) <= q_loc
    prev_bias = jnp.where(blk > 0, 0.0, _NEG_INF)

    tiles = [_pair_rmsnorm(kc_ref[...].astype(_F32), kg_ref[...]),
             _pair_rmsnorm(kp_ref[...].astype(_F32), kg_ref[...]),
             vc_ref[...].astype(_F32), vp_ref[...].astype(_F32)]
    swapped = [pltpu.roll(x, SWA_HD, axis=1) for x in tiles]
    for half in range(2):
        kc, kp, vc, vp = [(jnp.where(low, x, xs) if half == 0 else jnp.where(low, xs, x)).astype(_BF16)
                          for x, xs in zip(tiles, swapped)]
        qs, sink = [], []
        for t in range(pairs):
            c0 = half * group * SWA_HD + t * LANES
            qn = _pair_rmsnorm(q_ref[:, c0:c0 + LANES].astype(_F32), qg_ref[...]) * (SWA_HD ** -0.5)
            qs.append(jnp.where(low, qn, 0.0))
            qs.append(jnp.where(low, 0.0, qn))
            head = (2 * p + half) * group + 2 * t
            sink.append(jnp.full((bs, 1), sinks_ref[head], _F32))
            sink.append(jnp.full((bs, 1), sinks_ref[head + 1], _F32))
        qstack = jnp.concatenate(qs, axis=0).astype(_BF16)
        sink = jnp.concatenate(sink, axis=0)
        s = jnp.where(from_cur, _nt(qstack, kc), _nt(qstack, kp) + prev_bias)
        m = jnp.maximum(jnp.max(s, axis=1, keepdims=True), sink)
        e = jnp.exp(s - m)
        denom = jnp.sum(e, axis=1, keepdims=True) + jnp.exp(sink - m)
        e_cur = jnp.where(from_cur, e, 0.0).astype(_BF16)
        e_prev = jnp.where(from_cur, 0.0, e).astype(_BF16)
        o = (jnp.dot(e_cur, vc, preferred_element_type=_F32)
             + jnp.dot(e_prev, vp, preferred_element_type=_F32)) / denom
        for t in range(pairs):
            c0 = half * group * SWA_HD + t * LANES
            oa = o[(2 * t) * bs:(2 * t + 1) * bs, :]
            ob = o[(2 * t + 1) * bs:(2 * t + 2) * bs, :]
            o_ref[:, c0:c0 + LANES] = jnp.where(low, oa, ob).astype(o_ref.dtype)
        parts[half]()


def _swa_side(sq, q_col, skv, k_col, sinks, q_gain, k_gain, batch, seq, ni):
    t = batch * seq
    nb = seq // SWA_BLOCK
    npairs = SWA_KV_HEADS // 2
    kvw = SWA_KV_HEADS * SWA_HD
    pw = 2 * (SWA_HEADS // SWA_KV_HEADS) * SWA_HD
    assert q_col % pw == 0 and k_col % LANES == 0 and kvw % LANES == 0

    def at(col, prev=False):
        def index(j, i):
            s = j * ni + i
            blk = s // npairs
            if prev:
                blk = blk - jnp.where(blk % nb > 0, 1, 0)
            return blk, col + s % npairs
        return index

    const = lambda j, i: (0, 0)
    kc, vc = k_col // LANES, (k_col + kvw) // LANES
    in_specs = [
        pl.BlockSpec(memory_space=pltpu.SMEM),
        pl.BlockSpec((SWA_BLOCK, pw), at(q_col // pw)),
        pl.BlockSpec((SWA_BLOCK, LANES), at(kc)),
        pl.BlockSpec((SWA_BLOCK, LANES), at(kc, prev=True)),
        pl.BlockSpec((SWA_BLOCK, LANES), at(vc)),
        pl.BlockSpec((SWA_BLOCK, LANES), at(vc, prev=True)),
        pl.BlockSpec((1, LANES), const),
        pl.BlockSpec((1, LANES), const),
    ]
    g2 = lambda g: jnp.concatenate([g, g]).reshape(1, LANES).astype(_F32)
    side = _Side(functools.partial(_swa_kernel, nb=nb), 2,
                 [sinks.astype(_F32), sq, skv, skv, skv, skv, g2(q_gain), g2(k_gain)], in_specs,
                 [pl.BlockSpec((SWA_BLOCK, pw), at(0))],
                 [jax.ShapeDtypeStruct((t, SWA_HEADS * SWA_HD), _BF16)], vmem=16 << 20)
    return side, batch * nb * npairs


def _memattn_kernel(q_ref, k_ref, v_ref, g_ref, o_ref):
    hd = q_ref.shape[1]
    q = q_ref[...].astype(_F32)
    ms = jnp.mean(q * q, axis=-1, keepdims=True)
    qn = (q * lax.rsqrt(ms + RMS_EPS) * g_ref[...] * (hd ** -0.5)).astype(_BF16)
    s = _nt(qn, k_ref[...])
    m = jnp.max(s, axis=-1, keepdims=True)
    e = jnp.exp(s - m)
    denom = jnp.sum(e, axis=-1, keepdims=True)
    o = jnp.dot(e.astype(_BF16), v_ref[...], preferred_element_type=_F32) / denom
    o_ref[...] = o.astype(o_ref.dtype)


def _memattn(mq, kc, vm, q_gain, batch, seq, mem_len, d, q_blk=0, v_blk=0, tm=1024):
    t = mq.shape[0]
    tm = _tile(seq, tm)
    ns = seq // tm
    hd = d // MEM_HEADS
    return pl.pallas_call(
        _memattn_kernel,
        grid=(batch, ns, MEM_HEADS),
        in_specs=[
            pl.BlockSpec((tm, hd), lambda b, i, h: (b * ns + i, q_blk + h)),
            pl.BlockSpec((mem_len, hd), lambda b, i, h: (b, h)),
            pl.BlockSpec((mem_len, hd), lambda b, i, h: (b, v_blk + h)),
            pl.BlockSpec((1, hd), lambda b, i, h: (0, 0)),
        ],
        out_specs=pl.BlockSpec((tm, hd), lambda b, i, h: (b * ns + i, h)),
        out_shape=jax.ShapeDtypeStruct((t, d), _BF16),
        compiler_params=_params(("parallel", "parallel", "arbitrary"),
                                4 * tm * hd * 2 + 4 * mem_len * hd * 2 + 6 * tm * hd * 4),
        name="memattn",
    )(mq, kc, vm, q_gain.reshape(1, hd).astype(_F32))


def _merge_kernel(a_ref, b_ref, c_ref, wa_ref, wb_ref, wc_ref, ga_ref, gb_ref, gc_ref, o_ref):
    def branch(x_ref, w_ref, g_ref):
        y = jnp.dot(x_ref[...], w_ref[...], preferred_element_type=_F32)
        return jax.nn.sigmoid(g_ref[...].astype(_F32)) * y

    o_ref[...] = (branch(a_ref, wa_ref, ga_ref) + branch(b_ref, wb_ref, gb_ref)
                  + branch(c_ref, wc_ref, gc_ref)).astype(o_ref.dtype)


def _merge(oa, ob, oc, wa, wb, wc, gate, gate_col=0, tm=512, tn=512):
    t, d = oa.shape
    n = wa.shape[1]
    tm, tn = _tile(t, tm), _tile(n, tn)
    nn = n // tn
    if gate_col % tn:
        gate, gate_col = gate[:, gate_col:gate_col + 3 * n], 0
    g0 = gate_col // tn
    act = lambda o: pl.BlockSpec((tm, o.shape[1]), lambda j, i: (i, 0))
    wsp = lambda w: pl.BlockSpec((w.shape[0], tn), lambda j, i: (0, j), pipeline_mode=pl.Buffered(1))
    gsp = lambda b: pl.BlockSpec((tm, tn), lambda j, i: (i, g0 + b * nn + j))
    vmem = 3 * (2 * tm * d * 2 + d * tn * 2 + 2 * tm * tn * 2) + 8 * tm * tn * 4
    return pl.pallas_call(
        _merge_kernel,
        grid=(nn, t // tm),
        in_specs=[act(oa), act(ob), act(oc), wsp(wa), wsp(wb), wsp(wc), gsp(0), gsp(1), gsp(2)],
        out_specs=pl.BlockSpec((tm, tn), lambda j, i: (i, j)),
        out_shape=jax.ShapeDtypeStruct((t, n), _BF16),
        compiler_params=_params(("parallel", "arbitrary"), vmem),
        name="merge",
    )(oa, ob, oc, wa, wb, wc, gate, gate, gate)


def _extract_topk(s, payload, kk):
    n = s.shape[0]
    pos = lax.broadcasted_iota(jnp.int32, s.shape, 0).astype(_F32)
    vals, sel = [], []
    for _ in range(kk):
        m = jnp.max(s, axis=0, keepdims=True)
        first = jnp.min(jnp.where(s == m, pos, float(n)), axis=0, keepdims=True)
        hit = pos == first
        vals.append(m)
        sel.append(first if payload is None else jnp.max(jnp.where(hit, payload, -1.0), axis=0, keepdims=True))
        s = jnp.where(hit, _NEG_INF, s)
    return jnp.concatenate(vals, axis=0), jnp.concatenate(sel, axis=0)


def _route_kernel(q_ref, keys_ref, i1_ref, i2_ref, g_ref):
    kk = PEER_TOPK
    half = PEER_DQ // 2
    experts, gates = [], []
    for h in range(PEER_HEADS):
        top = []
        for p in range(2):
            c0 = (2 * h + p) * half
            st = _nt(keys_ref[2 * h + p].astype(_BF16), q_ref[:, c0:c0 + half].astype(_BF16))
            top.append(_extract_topk(st, None, kk))
        (s0, i0), (s1, i1) = top
        cand, ids = [], []
        tail = kk // 2
        for a in range(tail):
            nb = kk // (a + 1)
            rows = -(-nb // 8) * 8
            c = s0[a:a + 1, :] + s1[0:rows, :]
            if rows != nb:
                c = jnp.where(lax.broadcasted_iota(jnp.int32, c.shape, 0) < nb, c, _NEG_INF)
            cand.append(c)
            ids.append(i0[a:a + 1, :] * PEER_NKEYS + i1[0:rows, :])
        cand.append(s0[tail:kk, :] + s1[0:1, :])
        ids.append(i0[tail:kk, :] * PEER_NKEYS + i1[0:1, :])
        best, e = _extract_topk(jnp.concatenate(cand, axis=0), jnp.concatenate(ids, axis=0), kk)
        ex = jnp.exp(best - best[0:1, :])
        gates.append(ex / jnp.sum(ex, axis=0, keepdims=True))
        experts.append(e)
    e = jnp.concatenate(experts, axis=0).T.astype(jnp.int32)
    i1_ref[...] = e >> _log2(PEER_NKEYS)
    i2_ref[...] = e & (PEER_NKEYS - 1)
    g_ref[...] = jnp.concatenate(gates, axis=0).T


def _route(pq, sub_keys, tt=256):
    t = pq.shape[0]
    tt = _tile(t, tt)
    j = PEER_HEADS * PEER_TOPK
    half = PEER_DQ // 2
    keys = sub_keys.reshape(PEER_HEADS * 2, PEER_NKEYS, half)
    out = pl.BlockSpec((tt, j), lambda i: (i, 0))
    return pl.pallas_call(
        _route_kernel,
        grid=(t // tt,),
        in_specs=[pl.BlockSpec((tt, PEER_HEADS * PEER_DQ), lambda i: (i, 0)),
                  pl.BlockSpec(keys.shape, lambda i: (0, 0, 0))],
        out_specs=[out, out, out],
        out_shape=[jax.ShapeDtypeStruct((t, j), jnp.int32), jax.ShapeDtypeStruct((t, j), jnp.int32),
                   jax.ShapeDtypeStruct((t, j), _F32)],
        compiler_params=_params(("parallel",), 24 << 20),
        name="peer_route",
    )(pq, keys)


def _gatemat_kernel(i1_ref, i2_ref, g_ref, w_ref):
    nk = PEER_NKEYS
    sub = lax.broadcasted_iota(jnp.int32, (nk, i1_ref.shape[1]), 0)

    def body(t, carry):
        a_t = jnp.where(sub == i1_ref[pl.ds(t, 1), :], 1.0, 0.0).astype(_BF16)
        b_t = jnp.where(sub == i2_ref[pl.ds(t, 1), :], g_ref[pl.ds(t, 1), :], 0.0).astype(_BF16)
        w_ref[t] = _nt(a_t, b_t)
        return carry

    lax.fori_loop(0, i1_ref.shape[0], body, 0, unroll=8)


def _gatemat(i1, i2, g, tt=64):
    t, j = i1.shape
    tt = _tile(t, tt)
    nk = PEER_NKEYS
    spec = pl.BlockSpec((tt, j), lambda i: (i, 0))
    return pl.pallas_call(
        _gatemat_kernel,
        grid=(t // tt,),
        in_specs=[spec, spec, spec],
        out_specs=pl.BlockSpec((tt, nk, nk), lambda i: (i, 0, 0)),
        out_shape=jax.ShapeDtypeStruct((t, nk, nk), _F32),
        compiler_params=_params(("parallel",), 2 * tt * nk * nk * 4 + (8 << 20)),
        name="peer_gatemat",
    )(i1, i2, g)


def _gelu(x):
    return 0.5 * x * (1.0 + lax.erf(x * (2.0 ** -0.5)))


def _peer_act_kernel(x_ref, u_ref, w_ref, s_ref):
    act = _gelu(_nt(x_ref[...], u_ref[...]))
    w = jnp.transpose(w_ref[...], (1, 0, 2))
    for a in range(w.shape[0]):
        cs = slice(a * PEER_NKEYS, (a + 1) * PEER_NKEYS)
        s_ref[:, cs] = (act[:, cs] * w[a]).astype(s_ref.dtype)


def _peer_act(xn, u, w, tm=512, rows_a=8):
    t, d = xn.shape
    ne = u.shape[0]
    nk = PEER_NKEYS
    tm = _tile(t, tm)
    eb = rows_a * nk
    vmem = 2 * (tm * d * 2 + eb * d * 2 + tm * rows_a * nk * 4 + tm * eb * 2) + 3 * tm * eb * 4
    return pl.pallas_call(
        _peer_act_kernel,
        grid=(ne // eb, t // tm),
        in_specs=[pl.BlockSpec((tm, d), lambda j, i: (i, 0)),
                  pl.BlockSpec((eb, d), lambda j, i: (j, 0)),
                  pl.BlockSpec((tm, rows_a, nk), lambda j, i: (i, j, 0))],
        out_specs=pl.BlockSpec((tm, eb), lambda j, i: (i, j)),
        out_shape=jax.ShapeDtypeStruct((t, ne), _BF16),
        compiler_params=_params(("parallel", "parallel"), vmem),
        name="peer_act",
    )(xn, u, w)


def _mm_acc_kernel(x_ref, w_ref, r_ref, o_ref):
    acc = jnp.dot(x_ref[...], w_ref[...], preferred_element_type=_F32)

    @pl.when(pl.program_id(2) == 0)
    def _():
        o_ref[...] = r_ref[...] + acc

    @pl.when(pl.program_id(2) > 0)
    def _():
        o_ref[...] += acc


def _matmul_acc(x, w, residual, tm=1024, tn=1024, tk=4096):
    m, k = x.shape
    n = w.shape[1]
    tm, tn, tk = _tile(m, tm), _tile(n, tn), _tile(k, tk)
    vmem = 2 * (tm * tk * 2 + tk * tn * 2 + 2 * tm * tn * 4) + tm * tn * 4
    return pl.pallas_call(
        _mm_acc_kernel,
        grid=(m // tm, n // tn, k // tk),
        in_specs=[pl.BlockSpec((tm, tk), lambda i, j, l: (i, l)),
                  pl.BlockSpec((tk, tn), lambda i, j, l: (l, j)),
                  pl.BlockSpec((tm, tn), lambda i, j, l: (i, j))],
        out_specs=pl.BlockSpec((tm, tn), lambda i, j, l: (i, j)),
        out_shape=jax.ShapeDtypeStruct((m, n), _F32),
        compiler_params=_params(("parallel", "parallel", "arbitrary"), vmem),
        name="peer_out",
    )(x, w, residual)


def _layer(h, mem, norm_mix, w_in, gla_w_gate_up, gla_gate_bias, gla_out_norm, swa_q_norm, swa_k_norm,
           swa_sinks, mem_norm, w_mem_kv, mem_q_norm, mem_k_norm, w_branch_gla, w_branch_swa,
           w_branch_mem, w_out, norm_ffn, peer_w_q, peer_sub_keys, peer_u, peer_v, batch, seq, l):
    t, d = h.shape
    mem_len = mem.shape[0] // batch
    qk = GLA_HEADS * GLA_DK
    gv = GLA_HEADS * GLA_DV
    swq = SWA_HEADS * SWA_HD
    skv = SWA_KV_HEADS * SWA_HD
    o_glr = 2 * qk + 2 * gv
    o_sq = o_glr + GLA_RANK
    o_sk = o_sq + swq
    o_mq = o_sk + 2 * skv
    o_gate = o_mq + d
    bf = lambda w: w.astype(_BF16)
    hd = d // MEM_HEADS

    xn = _rmsnorm(h, norm_mix, _BF16)
    qkvr = _matmul_w32(xn, w_in, l, [(0, o_glr)], _BF16, name="proj_gla")
    glr = _matmul_w32(xn, w_in, l, [(o_glr, LANES)], _BF16, name="proj_glr")
    skvp = _matmul_w32(xn, w_in, l, [(o_sk, 2 * skv)], _BF16, name="proj_swa_kv")

    tm_side = _tile(t, 512)
    ni = t // tm_side
    wg = bf(jnp.pad(gla_w_gate_up, ((0, LANES - GLA_RANK), (0, 0))))
    gla = _gla_side(qkvr, glr, wg, gla_gate_bias.reshape(1, qk).astype(_F32),
                    gla_out_norm.reshape(1, gv).astype(_F32), batch, seq, ni)
    sqmq, o_a = _matmul_w32(xn, w_in, l, [(o_sq, swq), (o_mq, d)], _BF16, tm=tm_side, side=gla,
                            name="proj_q_gla")
    swa = _swa_side(sqmq, 0, skvp, 0, swa_sinks, swa_q_norm, swa_k_norm, batch, seq, ni)
    gate, o_b = _matmul_w32(xn, w_in, l, [(o_gate, 3 * d)], _BF16, tm=tm_side, tn=768, side=swa,
                            name="proj_gate_swa")

    memn = _rmsnorm(mem, mem_norm, _BF16)
    kv_m = _matmul_w32(memn, w_mem_kv, l, [(0, 2 * d)], _BF16, name="mem_kv")
    k_c = _rmsnorm(kv_m[:, :d].reshape(-1, hd), mem_k_norm, _BF16).reshape(-1, d)
    mq, mq_blk = (sqmq, swq // hd) if swq % hd == 0 else (sqmq[:, swq:], 0)
    o_c = _memattn(mq, k_c, kv_m, mem_q_norm, batch, seq, mem_len, d, mq_blk, MEM_HEADS)

    mix = _merge(o_a, o_b, o_c, bf(w_branch_gla), bf(w_branch_swa), bf(w_branch_mem), gate)
    h = _matmul_w32(mix, w_out, l, [(0, d)], _F32, residual=h, name="out_proj")

    hn = _rmsnorm(h, norm_ffn, _BF16)
    pq = _matmul_w32(hn, peer_w_q, l, [(0, peer_w_q.shape[2])], _F32, name="peer_q")
    i1, i2, g = _route(pq, peer_sub_keys.astype(_F32))
    w = _gatemat(i1, i2, g)
    s = _peer_act(hn, bf(peer_u), w)
    return _matmul_acc(s, bf(peer_v), h)


def kernel(x, mem, norm_mix, w_in, gla_w_gate_up, gla_gate_bias, gla_out_norm, swa_q_norm, swa_k_norm,
           swa_sinks, mem_norm, w_mem_kv, mem_q_norm, mem_k_norm, w_branch_gla, w_branch_swa, w_branch_mem,
           w_out, norm_ffn, peer_w_q, peer_sub_keys, peer_u, peer_v):
    batch, seq, d = x.shape
    h = x.reshape(batch * seq, d)
    memf = mem.reshape(-1, d)
    for l in range(norm_mix.shape[0]):
        h = _layer(h, memf, norm_mix[l], w_in, gla_w_gate_up[l], gla_gate_bias[l], gla_out_norm[l],
                   swa_q_norm[l], swa_k_norm[l], swa_sinks[l], mem_norm[l], w_mem_kv, mem_q_norm[l],
                   mem_k_norm[l], w_branch_gla[l], w_branch_swa[l], w_branch_mem[l], w_out, norm_ffn[l],
                   peer_w_q, peer_sub_keys[l], peer_u[l], peer_v[l], batch, seq, l)
    return h.reshape(batch, seq, d)
```

```python
import functools

import jax
import jax.numpy as jnp
from jax import lax
from jax.experimental import pallas as pl
from jax.experimental.pallas import tpu as pltpu

_F32 = jnp.float32
_BF16 = jnp.bfloat16
_NEG_INF = float("-inf")

RMS_EPS = 1e-6
GLA_HEADS = 8
GLA_DK = 256
GLA_DV = 512
GLA_RANK = 16
GLA_TAU = 16.0
GLA_SUB = 8
GLA_STEP_HEADS = 2
_LOG2_E = 1.4426950408889634
SWA_HEADS = 64
SWA_KV_HEADS = 8
SWA_HD = 64
SWA_BLOCK = 128
MEM_HEADS = 4
PEER_HEADS = 8
PEER_NKEYS = 128
PEER_DQ = 256
PEER_TOPK = 16

LANES = 128
V7X_VMEM_BYTES = 64 * 1024 * 1024


def _tile(n, pref):
    t = min(n, pref)
    while n % t:
        t -= 1
    return t


def _params(semantics, vmem_bytes):
    limit = min(int(vmem_bytes) + (8 << 20), V7X_VMEM_BYTES - (4 << 20))
    return pltpu.CompilerParams(dimension_semantics=semantics, vmem_limit_bytes=limit)


def _nt(a, b):
    return lax.dot_general(a, b, (((1,), (1,)), ((), ())), preferred_element_type=_F32)


def _log2(n):
    assert n > 0 and n & (n - 1) == 0, n
    return n.bit_length() - 1


def _tn(a, b):
    return lax.dot_general(a, b, (((0,), (0,)), ((), ())), preferred_element_type=_F32)


def _rmsnorm_kernel(x_ref, g_ref, o_ref):
    x = x_ref[...].astype(_F32)
    ms = jnp.mean(x * x, axis=-1, keepdims=True)
    o_ref[...] = (x * lax.rsqrt(ms + RMS_EPS) * g_ref[...]).astype(o_ref.dtype)


def _rmsnorm(x, gain, out_dtype, rows=256):
    r, d = x.shape
    tr = _tile(r, rows)
    blk = tr * d * (x.dtype.itemsize + jnp.dtype(out_dtype).itemsize)
    return pl.pallas_call(
        _rmsnorm_kernel,
        grid=(r // tr,),
        in_specs=[pl.BlockSpec((tr, d), lambda i: (i, 0)), pl.BlockSpec((1, d), lambda i: (0, 0))],
        out_specs=pl.BlockSpec((tr, d), lambda i: (i, 0)),
        out_shape=jax.ShapeDtypeStruct((r, d), out_dtype),
        compiler_params=_params(("parallel",), 2 * blk + 4 * tr * d * 4),
        name="rmsnorm",
    )(x, gain.reshape(1, d).astype(_F32))


def _mm_kernel(x_ref, w_ref, o_ref):
    o_ref[...] = jnp.dot(x_ref[...], w_ref[...], preferred_element_type=_F32).astype(o_ref.dtype)


def _mm_res_kernel(x_ref, w_ref, r_ref, o_ref):
    acc = jnp.dot(x_ref[...], w_ref[...], preferred_element_type=_F32)
    o_ref[...] = (acc + r_ref[...].astype(_F32)).astype(o_ref.dtype)


def _matmul(x, w, out_dtype, tm=1024, tn=512, residual=None, name="matmul"):
    m, k = x.shape
    n = w.shape[1]
    tm, tn = _tile(m, tm), _tile(n, tn)
    osz = jnp.dtype(out_dtype).itemsize
    vmem = 2 * (tm * k * x.dtype.itemsize + k * tn * w.dtype.itemsize + tm * tn * osz) + tm * tn * 4
    in_specs = [pl.BlockSpec((tm, k), lambda j, i: (i, 0)), pl.BlockSpec((k, tn), lambda j, i: (0, j))]
    args = [x, w]
    kern = _mm_kernel
    if residual is not None:
        in_specs.append(pl.BlockSpec((tm, tn), lambda j, i: (i, j)))
        args.append(residual)
        kern = _mm_res_kernel
        vmem += 2 * tm * tn * residual.dtype.itemsize
    return pl.pallas_call(
        kern,
        grid=(n // tn, m // tm),
        in_specs=in_specs,
        out_specs=pl.BlockSpec((tm, tn), lambda j, i: (i, j)),
        out_shape=jax.ShapeDtypeStruct((m, n), out_dtype),
        compiler_params=_params(("parallel", "parallel"), vmem),
        name=name,
    )(*args)


class _Side:
    def __init__(self, kernel, parts, args, in_specs, out_specs, out_shape, scratch_shapes=(), vmem=0):
        self.kernel, self.parts, self.args, self.in_specs = kernel, parts, list(args), list(in_specs)
        self.out_specs, self.out_shape = list(out_specs), list(out_shape)
        self.scratch_shapes, self.vmem = list(scratch_shapes), vmem


def _mm_w32_kernel(*refs, shift, has_residual, side, transposed):
    n_in = 2 + bool(shift) + has_residual
    n_side_in = len(side.in_specs) if side else 0
    n_side_out = len(side.out_specs) if side else 0
    x_ref, w_ref = refs[0], refs[1]
    o_ref = refs[n_in + n_side_in]
    wbf_ref = refs[n_in + n_side_in + 1 + n_side_out]
    tn = o_ref.shape[1]

    @pl.when(pl.program_id(1) == 0)
    def _():
        rows = _tile(w_ref.shape[0], 128 if transposed else 512)
        for r0 in range(0, w_ref.shape[0], rows):
            rs = slice(r0, r0 + rows)
            if transposed:
                if r0 + rows + shift <= tn:
                    w = w_ref[r0 + shift:r0 + rows + shift, :]
                else:
                    w = jnp.concatenate([w_ref[r0 + shift:, :], refs[2][:shift, :]], axis=0)
                wbf_ref[:, rs] = w.T.astype(_BF16)
            else:
                w = w_ref[rs, :]
                if shift:
                    w = jnp.concatenate([w, refs[2][rs, :]], axis=1)
                    w = pltpu.roll(w, w.shape[1] - shift, axis=1)[:, :tn]
                wbf_ref[rs, :] = w.astype(_BF16)

    def matmul_rows(rs):
        acc = jnp.dot(x_ref[rs, :], wbf_ref[...], preferred_element_type=_F32)
        if has_residual:
            acc = acc + refs[n_in - 1][rs, :].astype(_F32)
        o_ref[rs, :] = acc.astype(o_ref.dtype)

    if side:
        rows = x_ref.shape[0] // side.parts
        parts = [functools.partial(matmul_rows, slice(r * rows, (r + 1) * rows)) for r in range(side.parts)]
        step = pl.program_id(0) * pl.num_programs(1) + pl.program_id(1)
        side.kernel(step, parts, *refs[n_in:n_in + n_side_in],
                    *refs[n_in + n_side_in + 1:n_in + n_side_in + 1 + n_side_out],
                    *refs[n_in + n_side_in + 2 + n_side_out:])
    else:
        matmul_rows(slice(None))


def _matmul_w32(x, w, layer, windows, out_dtype, tm=1024, tn=512, residual=None, side=None,
                transposed=False, name="matmul_w32"):
    m, k = x.shape
    shift = windows[0][0] % LANES
    assert all(c0 % LANES == shift for c0, _ in windows)
    tm = _tile(m, tm)
    ok = lambda t: all(n % t == 0 and (c0 - shift) % t == 0 for c0, n in windows)
    tn = max(t for t in range(LANES, tn + 1, LANES) if ok(t))
    n = sum(n for _, n in windows)

    def col_block(j, width):
        blk, start = 0, 0
        for c0, wn in windows:
            first = ((c0 - shift) // tn - start) * (tn // width)
            blk = jnp.where(j >= start, first + j * (tn // width), blk)
            start += wn // tn
        return blk

    osz = jnp.dtype(out_dtype).itemsize

    def wspec(width, block_of):
        if transposed:
            return pl.BlockSpec((None, width, k), lambda j, i: (layer, block_of(j), 0))
        return pl.BlockSpec((None, k, width), lambda j, i: (layer, 0, block_of(j)))

    in_specs = [pl.BlockSpec((tm, k), lambda j, i: (i, 0)), wspec(tn, lambda j: col_block(j, tn))]
    args = [x, w]
    vmem = 2 * (tm * k * 2 + k * tn * 4 + tm * tn * osz) + k * tn * 2 + tm * tn * 4
    if shift:
        in_specs.append(wspec(LANES, lambda j: col_block(j, LANES) + tn // LANES))
        args.append(w)
        vmem += 2 * k * LANES * 4
    if residual is not None:
        in_specs.append(pl.BlockSpec((tm, tn), lambda j, i: (i, j)))
        args.append(residual)
        vmem += 2 * tm * tn * residual.dtype.itemsize
    out_specs = [pl.BlockSpec((tm, tn), lambda j, i: (i, j))]
    out_shape = [jax.ShapeDtypeStruct((m, n), out_dtype)]
    scratch = [pltpu.VMEM((k, tn), _BF16)]
    alone = None
    if side:
        side, steps = side
        if steps != (n // tn) * (m // tm):
            alone, side = _run_side(side, steps, m // tm, name + "_side"), None
    if side:
        in_specs += side.in_specs
        args += side.args
        out_specs += side.out_specs
        out_shape += side.out_shape
        scratch += side.scratch_shapes
        vmem += side.vmem
    out = pl.pallas_call(
        functools.partial(_mm_w32_kernel, shift=shift, has_residual=residual is not None, side=side,
                          transposed=transposed),
        grid=(n // tn, m // tm),
        in_specs=in_specs,
        out_specs=out_specs,
        out_shape=out_shape,
        scratch_shapes=scratch,
        compiler_params=_params(("arbitrary", "arbitrary"), vmem),
        name=name,
    )(*args)
    if alone is not None:
        return [out[0], *alone]
    return out if side else out[0]


def _log_sigmoid(z):
    return jnp.minimum(z, 0.0) - jnp.log(1.0 + jnp.exp(-jnp.abs(z)))


def _cumsum_rows(x):
    c, n = x.shape
    row = lax.broadcasted_iota(jnp.int32, (c, c), 0)
    col = lax.broadcasted_iota(jnp.int32, (c, c), 1)
    tri = jnp.where(row >= col, 1.0, 0.0).astype(_BF16)
    hi = x.astype(_BF16)
    rest = x - hi.astype(_F32)
    mid = rest.astype(_BF16)
    lo = (rest - mid.astype(_F32)).astype(_BF16)
    parts = jnp.dot(tri, jnp.concatenate([hi, mid, lo], axis=1), preferred_element_type=_F32)
    return parts[:, :n] + parts[:, n:2 * n] + parts[:, 2 * n:]


def _gla_chunk(q, k, v, b, state, midway):
    c = q.shape[0]
    row = lax.broadcasted_iota(jnp.int32, (c, c), 0)
    col = lax.broadcasted_iota(jnp.int32, (c, c), 1)
    b_last = b[c - 1:c, :]

    attn = jnp.zeros((c, c), _F32)
    row1 = lax.broadcasted_iota(jnp.int32, (c, 1), 0)
    s = c // 2
    while s >= GLA_SUB:
        pieces = []
        for lo in range(0, c, 2 * s):
            mid = lo + s
            ref_row = b[mid - 1:mid, :]
            pieces.append(ref_row - b[lo:mid, :])
            pieces.append(b[mid:mid + s, :] - ref_row)
        x = jnp.exp2(jnp.concatenate(pieces, axis=0))
        upper = ((row1 >> _log2(s)) & 1) == 1
        qt = jnp.where(upper, q * x, 0.0).astype(_BF16)
        kt = jnp.where(upper, 0.0, k * x).astype(_BF16)
        same = (row >> _log2(2 * s)) == (col >> _log2(2 * s))
        attn = attn + jnp.where(same, _nt(qt, kt), 0.0)
        s //= 2

    midway()
    sub_row = lax.broadcasted_iota(jnp.int32, (GLA_SUB, 1), 0)
    lane = lax.broadcasted_iota(jnp.int32, (GLA_SUB, c), 1)
    diag = []
    for i0 in range(0, c, GLA_SUB):
        qi, ki, bi = q[i0:i0 + GLA_SUB, :], k[i0:i0 + GLA_SUB, :], b[i0:i0 + GLA_SUB, :]
        blk = jnp.zeros((GLA_SUB, c), _F32)
        for j in range(GLA_SUB):
            dec = jnp.exp2(jnp.where(sub_row >= j, bi - bi[j:j + 1, :], _NEG_INF))
            sc = jnp.sum(qi * dec * ki[j:j + 1, :], axis=1, keepdims=True)
            blk = jnp.where(lane == i0 + j, sc, blk)
        diag.append(blk)
    attn = attn + jnp.concatenate(diag, axis=0)

    qb = (q * jnp.exp2(b)).astype(_BF16)
    o = _nt(qb, state.astype(_BF16)) + jnp.dot(attn.astype(_BF16), v, preferred_element_type=_F32)
    kd = (k * jnp.exp2(b_last - b)).astype(_BF16)
    return o, state * jnp.exp2(b_last) + _tn(v, kd)


def _gla_kernel(step, parts, q_ref, k_ref, v_ref, r_ref, glr_ref, wg_ref, bias_ref, gain_ref, o_ref,
                state_ref, *, nc):
    @pl.when(step % nc == 0)
    def _():
        state_ref[...] = jnp.zeros_like(state_ref)

    z = jnp.dot(glr_ref[...], wg_ref[...], preferred_element_type=_F32) + bias_ref[...]
    b_all = _cumsum_rows(_log_sigmoid(z) * (_LOG2_E / GLA_TAU))
    for h in range(GLA_STEP_HEADS):
        parts[2 * h]()
        ks = slice(h * GLA_DK, (h + 1) * GLA_DK)
        vs = slice(h * GLA_DV, (h + 1) * GLA_DV)
        q = q_ref[:, ks].astype(_F32) * (GLA_DK ** -0.5)
        o, state_ref[h] = _gla_chunk(q, k_ref[:, ks].astype(_F32), v_ref[:, vs], b_all[:, ks], state_ref[h],
                                     parts[2 * h + 1])
        ms = jnp.mean(o * o, axis=-1, keepdims=True)
        on = o * lax.rsqrt(ms + RMS_EPS) * gain_ref[:, vs]
        r = r_ref[:, vs].astype(_F32)
        o_ref[:, vs] = (on * (r * jax.nn.sigmoid(r))).astype(o_ref.dtype)


def _run_side(side, steps, ni, name):
    def kern(*refs):
        side.kernel(pl.program_id(0) * pl.num_programs(1) + pl.program_id(1),
                    [lambda: None] * side.parts, *refs)

    return pl.pallas_call(
        kern, grid=(steps // ni, ni), in_specs=side.in_specs, out_specs=side.out_specs,
        out_shape=side.out_shape, scratch_shapes=side.scratch_shapes,
        compiler_params=_params(("arbitrary", "arbitrary"), side.vmem), name=name)(*side.args)


def _gla_side(qkvr, glr, wg, bias, gain, batch, seq, ni, chunk=128):
    t = batch * seq
    nc = seq // chunk
    hs = GLA_STEP_HEADS
    wk, wv = hs * GLA_DK, hs * GLA_DV
    groups = GLA_HEADS // hs
    kv = 2 * groups * wk // wv

    def at(col):
        def index(j, i):
            s = j * ni + i
            return (s // (groups * nc)) * nc + s % nc, col + (s // nc) % groups
        return index

    head = lambda j, i: (0, ((j * ni + i) // nc) % groups)
    in_specs = [
        pl.BlockSpec((chunk, wk), at(0)),
        pl.BlockSpec((chunk, wk), at(groups)),
        pl.BlockSpec((chunk, wv), at(kv)),
        pl.BlockSpec((chunk, wv), at(kv + groups)),
        pl.BlockSpec((chunk, LANES), lambda j, i: (at(0)(j, i)[0], 0)),
        pl.BlockSpec((LANES, wk), head),
        pl.BlockSpec((1, wk), head),
        pl.BlockSpec((1, wv), head),
    ]
    side = _Side(functools.partial(_gla_kernel, nc=nc), 2 * hs, [qkvr, qkvr, qkvr, qkvr, glr, wg, bias, gain], in_specs,
                 [pl.BlockSpec((chunk, wv), at(0))], [jax.ShapeDtypeStruct((t, GLA_HEADS * GLA_DV), _BF16)],
                 [pltpu.VMEM((hs, GLA_DV, GLA_DK), _F32)], vmem=16 << 20)
    return side, batch * groups * nc


def _pair_rmsnorm(x, gain2):
    lane = lax.broadcasted_iota(jnp.int32, x.shape, 1)
    low = lane < SWA_HD
    sq = x * x
    s_all = jnp.sum(sq, axis=1, keepdims=True)
    s_low = jnp.sum(jnp.where(low, sq, 0.0), axis=1, keepdims=True)
    ms = jnp.where(low, s_low, s_all - s_low) * (1.0 / SWA_HD)
    return x * lax.rsqrt(ms + RMS_EPS) * gain2


def _swa_kernel(step, parts, sinks_ref, q_ref, kc_ref, kp_ref, vc_ref, vp_ref, qg_ref, kg_ref, o_ref, *, nb):
    npairs = SWA_KV_HEADS // 2
    blk = (step // npairs) % nb
    p = step % npairs
    bs = SWA_BLOCK
    group = SWA_HEADS // SWA_KV_HEADS
    pairs = group // 2
    lane = lax.broadcasted_iota(jnp.int32, (bs, LANES), 1)
    low = lane < SWA_HD

    q_loc = lax.broadcasted_iota(jnp.int32, (group * bs, bs), 0) & (bs - 1)
    from_cur = lax.broadcasted_iota(jnp.int32, (group * bs, bs), 1) <= q_loc
    prev_bias = jnp.where(blk > 0, 0.0, _NEG_INF)

    tiles = [_pair_rmsnorm(kc_ref[...].astype(_F32), kg_ref[...]),
             _pair_rmsnorm(kp_ref[...].astype(_F32), kg_ref[...]),
             vc_ref[...].astype(_F32), vp_ref[...].astype(_F32)]
    swapped = [pltpu.roll(x, SWA_HD, axis=1) for x in tiles]
    for half in range(2):
        kc, kp, vc, vp = [(jnp.where(low, x, xs) if half == 0 else jnp.where(low, xs, x)).astype(_BF16)
                          for x, xs in zip(tiles, swapped)]
        qs, sink = [], []
        for t in range(pairs):
            c0 = half * group * SWA_HD + t * LANES
            qn = _pair_rmsnorm(q_ref[:, c0:c0 + LANES].astype(_F32), qg_ref[...]) * (SWA_HD ** -0.5)
            qs.append(jnp.where(low, qn, 0.0))
            qs.append(jnp.where(low, 0.0, qn))
            head = (2 * p + half) * group + 2 * t
            sink.append(jnp.full((bs, 1), sinks_ref[head], _F32))
            sink.append(jnp.full((bs, 1), sinks_ref[head + 1], _F32))
        qstack = jnp.concatenate(qs, axis=0).astype(_BF16)
        sink = jnp.concatenate(sink, axis=0)
        s = jnp.where(from_cur, _nt(qstack, kc), _nt(qstack, kp) + prev_bias)
        m = jnp.maximum(jnp.max(s, axis=1, keepdims=True), sink)
        e = jnp.exp(s - m)
        denom = jnp.sum(e, axis=1, keepdims=True) + jnp.exp(sink - m)
        e_cur = jnp.where(from_cur, e, 0.0).astype(_BF16)
        e_prev = jnp.where(from_cur, 0.0, e).astype(_BF16)
        o = (jnp.dot(e_cur, vc, preferred_element_type=_F32)
             + jnp.dot(e_prev, vp, preferred_element_type=_F32)) / denom
        for t in range(pairs):
            c0 = half * group * SWA_HD + t * LANES
            oa = o[(2 * t) * bs:(2 * t + 1) * bs, :]
            ob = o[(2 * t + 1) * bs:(2 * t + 2) * bs, :]
            o_ref[:, c0:c0 + LANES] = jnp.where(low, oa, ob).astype(o_ref.dtype)
        parts[half]()


def _swa_side(sq, q_col, skv, k_col, sinks, q_gain, k_gain, batch, seq, ni):
    t = batch * seq
    nb = seq // SWA_BLOCK
    npairs = SWA_KV_HEADS // 2
    kvw = SWA_KV_HEADS * SWA_HD
    pw = 2 * (SWA_HEADS // SWA_KV_HEADS) * SWA_HD
    assert q_col % pw == 0 and k_col % LANES == 0 and kvw % LANES == 0

    def at(col, prev=False):
        def index(j, i):
            s = j * ni + i
            blk = s // npairs
            if prev:
                blk = blk - jnp.where(blk % nb > 0, 1, 0)
            return blk, col + s % npairs
        return index

    const = lambda j, i: (0, 0)
    kc, vc = k_col // LANES, (k_col + kvw) // LANES
    in_specs = [
        pl.BlockSpec(memory_space=pltpu.SMEM),
        pl.BlockSpec((SWA_BLOCK, pw), at(q_col // pw)),
        pl.BlockSpec((SWA_BLOCK, LANES), at(kc)),
        pl.BlockSpec((SWA_BLOCK, LANES), at(kc, prev=True)),
        pl.BlockSpec((SWA_BLOCK, LANES), at(vc)),
        pl.BlockSpec((SWA_BLOCK, LANES), at(vc, prev=True)),
        pl.BlockSpec((1, LANES), const),
        pl.BlockSpec((1, LANES), const),
    ]
    g2 = lambda g: jnp.concatenate([g, g]).reshape(1, LANES).astype(_F32)
    side = _Side(functools.partial(_swa_kernel, nb=nb), 2,
                 [sinks.astype(_F32), sq, skv, skv, skv, skv, g2(q_gain), g2(k_gain)], in_specs,
                 [pl.BlockSpec((SWA_BLOCK, pw), at(0))],
                 [jax.ShapeDtypeStruct((t, SWA_HEADS * SWA_HD), _BF16)], vmem=16 << 20)
    return side, batch * nb * npairs


def _memattn_kernel(q_ref, k_ref, v_ref, g_ref, o_ref):
    hd = q_ref.shape[1]
    q = q_ref[...].astype(_F32)
    ms = jnp.mean(q * q, axis=-1, keepdims=True)
    qn = (q * lax.rsqrt(ms + RMS_EPS) * g_ref[...] * (hd ** -0.5)).astype(_BF16)
    s = _nt(qn, k_ref[...])
    m = jnp.max(s, axis=-1, keepdims=True)
    e = jnp.exp(s - m)
    denom = jnp.sum(e, axis=-1, keepdims=True)
    o = jnp.dot(e.astype(_BF16), v_ref[...], preferred_element_type=_F32) / denom
    o_ref[...] = o.astype(o_ref.dtype)


def _memattn(mq, kc, vm, q_gain, batch, seq, mem_len, d, q_blk=0, v_blk=0, tm=1024):
    t = mq.shape[0]
    tm = _tile(seq, tm)
    ns = seq // tm
    hd = d // MEM_HEADS
    return pl.pallas_call(
        _memattn_kernel,
        grid=(batch, ns, MEM_HEADS),
        in_specs=[
            pl.BlockSpec((tm, hd), lambda b, i, h: (b * ns + i, q_blk + h)),
            pl.BlockSpec((mem_len, hd), lambda b, i, h: (b, h)),
            pl.BlockSpec((mem_len, hd), lambda b, i, h: (b, v_blk + h)),
            pl.BlockSpec((1, hd), lambda b, i, h: (0, 0)),
        ],
        out_specs=pl.BlockSpec((tm, hd), lambda b, i, h: (b * ns + i, h)),
        out_shape=jax.ShapeDtypeStruct((t, d), _BF16),
        compiler_params=_params(("parallel", "parallel", "arbitrary"),
                                4 * tm * hd * 2 + 4 * mem_len * hd * 2 + 6 * tm * hd * 4),
        name="memattn",
    )(mq, kc, vm, q_gain.reshape(1, hd).astype(_F32))


def _merge_kernel(a_ref, b_ref, c_ref, wa_ref, wb_ref, wc_ref, ga_ref, gb_ref, gc_ref, o_ref):
    def branch(x_ref, w_ref, g_ref):
        y = jnp.dot(x_ref[...], w_ref[...], preferred_element_type=_F32)
        return jax.nn.sigmoid(g_ref[...].astype(_F32)) * y

    o_ref[...] = (branch(a_ref, wa_ref, ga_ref) + branch(b_ref, wb_ref, gb_ref)
                  + branch(c_ref, wc_ref, gc_ref)).astype(o_ref.dtype)


def _merge(oa, ob, oc, wa, wb, wc, gate, gate_col=0, tm=512, tn=512):
    t, d = oa.shape
    n = wa.shape[1]
    tm, tn = _tile(t, tm), _tile(n, tn)
    nn = n // tn
    if gate_col % tn:
        gate, gate_col = gate[:, gate_col:gate_col + 3 * n], 0
    g0 = gate_col // tn
    act = lambda o: pl.BlockSpec((tm, o.shape[1]), lambda j, i: (i, 0))
    wsp = lambda w: pl.BlockSpec((w.shape[0], tn), lambda j, i: (0, j), pipeline_mode=pl.Buffered(1))
    gsp = lambda b: pl.BlockSpec((tm, tn), lambda j, i: (i, g0 + b * nn + j))
    vmem = 3 * (2 * tm * d * 2 + d * tn * 2 + 2 * tm * tn * 2) + 8 * tm * tn * 4
    return pl.pallas_call(
        _merge_kernel,
        grid=(nn, t // tm),
        in_specs=[act(oa), act(ob), act(oc), wsp(wa), wsp(wb), wsp(wc), gsp(0), gsp(1), gsp(2)],
        out_specs=pl.BlockSpec((tm, tn), lambda j, i: (i, j)),
        out_shape=jax.ShapeDtypeStruct((t, n), _BF16),
        compiler_params=_params(("parallel", "arbitrary"), vmem),
        name="merge",
    )(oa, ob, oc, wa, wb, wc, gate, gate, gate)


def _extract_topk(s, payload, kk):
    n = s.shape[0]
    pos = lax.broadcasted_iota(jnp.int32, s.shape, 0).astype(_F32)
    vals, sel = [], []
    for _ in range(kk):
        m = jnp.max(s, axis=0, keepdims=True)
        first = jnp.min(jnp.where(s == m, pos, float(n)), axis=0, keepdims=True)
        hit = pos == first
        vals.append(m)
        sel.append(first if payload is None else jnp.max(jnp.where(hit, payload, -1.0), axis=0, keepdims=True))
        s = jnp.where(hit, _NEG_INF, s)
    return jnp.concatenate(vals, axis=0), jnp.concatenate(sel, axis=0)


def _route_kernel(q_ref, keys_ref, w_ref, i1_ref, i2_ref, g_ref):
    kk = PEER_TOPK
    half = PEER_DQ // 2
    nk = PEER_NKEYS
    step = pl.program_id(0)
    cur, prev = step % 2, (step + 1) % 2
    tokens = q_ref.shape[0] // PEER_HEADS

    @pl.when(step == 0)
    def _():
        i1_ref[1] = jnp.zeros(i1_ref.shape[1:], i1_ref.dtype)
        i2_ref[1] = jnp.zeros(i2_ref.shape[1:], i2_ref.dtype)
        g_ref[1] = jnp.zeros(g_ref.shape[1:], g_ref.dtype)

    sub = lax.broadcasted_iota(jnp.int32, (nk, i1_ref.shape[2]), 0)

    def gate_rows(t0):
        for t in range(t0, t0 + tokens):
            a_t = jnp.where(sub == i1_ref[prev, t:t + 1, :], 1.0, 0.0).astype(_BF16)
            b_t = jnp.where(sub == i2_ref[prev, t:t + 1, :], g_ref[prev, t:t + 1, :], 0.0).astype(_BF16)
            w_ref[t] = _nt(a_t, b_t)

    experts, gates = [], []
    for h in range(PEER_HEADS):
        top = []
        for p in range(2):
            c0 = (2 * h + p) * half
            st = _nt(keys_ref[2 * h + p].astype(_BF16), q_ref[:, c0:c0 + half].astype(_BF16))
            top.append(_extract_topk(st, None, kk))
        (s0, i0), (s1, i1) = top
        cand, ids = [], []
        tail = kk // 2
        for a in range(tail):
            nb = kk // (a + 1)
            rows = -(-nb // 8) * 8
            c = s0[a:a + 1, :] + s1[0:rows, :]
            if rows != nb:
                c = jnp.where(lax.broadcasted_iota(jnp.int32, c.shape, 0) < nb, c, _NEG_INF)
            cand.append(c)
            ids.append(i0[a:a + 1, :] * PEER_NKEYS + i1[0:rows, :])
        cand.append(s0[tail:kk, :] + s1[0:1, :])
        ids.append(i0[tail:kk, :] * PEER_NKEYS + i1[0:1, :])
        best, e = _extract_topk(jnp.concatenate(cand, axis=0), jnp.concatenate(ids, axis=0), kk)
        ex = jnp.exp(best - best[0:1, :])
        gates.append(ex / jnp.sum(ex, axis=0, keepdims=True))
        experts.append(e)
        gate_rows(h * tokens)
    e = jnp.concatenate(experts, axis=0).T.astype(jnp.int32)
    i1_ref[cur] = e >> _log2(PEER_NKEYS)
    i2_ref[cur] = e & (PEER_NKEYS - 1)
    g_ref[cur] = jnp.concatenate(gates, axis=0).T


def _route(pq, sub_keys, tt=256):
    t = pq.shape[0]
    tt = _tile(t, tt)
    n = t // tt
    j = PEER_HEADS * PEER_TOPK
    nk = PEER_NKEYS
    keys = sub_keys.reshape(PEER_HEADS * 2, nk, PEER_DQ // 2)
    return pl.pallas_call(
        _route_kernel,
        grid=(n + 1,),
        in_specs=[pl.BlockSpec((tt, PEER_HEADS * PEER_DQ), lambda s: (jnp.minimum(s, n - 1), 0)),
                  pl.BlockSpec(keys.shape, lambda s: (0, 0, 0))],
        out_specs=pl.BlockSpec((tt, nk, nk), lambda s: (jnp.maximum(s - 1, 0), 0, 0)),
        out_shape=jax.ShapeDtypeStruct((t, nk, nk), _F32),
        scratch_shapes=[pltpu.VMEM((2, tt, j), jnp.int32), pltpu.VMEM((2, tt, j), jnp.int32),
                        pltpu.VMEM((2, tt, j), _F32)],
        compiler_params=_params(("arbitrary",), 2 * tt * nk * nk * 4 + (16 << 20)),
        name="peer_route",
    )(pq, keys)


def _gelu(x):
    return 0.5 * x * (1.0 + lax.erf(x * (2.0 ** -0.5)))


def _peer_act_kernel(x_ref, u_ref, w_ref, s_ref):
    act = _gelu(_nt(x_ref[...], u_ref[...]))
    w = jnp.transpose(w_ref[...], (1, 0, 2))
    for a in range(w.shape[0]):
        cs = slice(a * PEER_NKEYS, (a + 1) * PEER_NKEYS)
        s_ref[:, cs] = (act[:, cs] * w[a]).astype(s_ref.dtype)


def _peer_act(xn, u, w, tm=1024, rows_a=8):
    t, d = xn.shape
    ne = u.shape[0]
    nk = PEER_NKEYS
    tm = _tile(t, tm)
    eb = rows_a * nk
    vmem = 2 * (tm * d * 2 + tm * rows_a * nk * 4 + tm * eb * 2) + eb * d * 2 + 3 * tm * eb * 4
    return pl.pallas_call(
        _peer_act_kernel,
        grid=(ne // eb, t // tm),
        in_specs=[pl.BlockSpec((tm, d), lambda j, i: (i, 0)),
                  pl.BlockSpec((eb, d), lambda j, i: (j, 0), pipeline_mode=pl.Buffered(1)),
                  pl.BlockSpec((tm, rows_a, nk), lambda j, i: (i, j, 0))],
        out_specs=pl.BlockSpec((tm, eb), lambda j, i: (i, j)),
        out_shape=jax.ShapeDtypeStruct((t, ne), _BF16),
        compiler_params=_params(("parallel", "parallel"), vmem),
        name="peer_act",
    )(xn, u, w)


def _mm_acc_kernel(x_ref, w_ref, r_ref, o_ref):
    acc = jnp.dot(x_ref[...], w_ref[...], preferred_element_type=_F32)

    @pl.when(pl.program_id(2) == 0)
    def _():
        o_ref[...] = r_ref[...] + acc

    @pl.when(pl.program_id(2) > 0)
    def _():
        o_ref[...] += acc


def _matmul_acc(x, w, residual, tm=1024, tn=1024, tk=4096):
    m, k = x.shape
    n = w.shape[1]
    tm, tn, tk = _tile(m, tm), _tile(n, tn), _tile(k, tk)
    vmem = 2 * (tm * tk * 2 + tk * tn * 2 + 2 * tm * tn * 4) + tm * tn * 4
    return pl.pallas_call(
        _mm_acc_kernel,
        grid=(m // tm, n // tn, k // tk),
        in_specs=[pl.BlockSpec((tm, tk), lambda i, j, l: (i, l)),
                  pl.BlockSpec((tk, tn), lambda i, j, l: (l, j)),
                  pl.BlockSpec((tm, tn), lambda i, j, l: (i, j))],
        out_specs=pl.BlockSpec((tm, tn), lambda i, j, l: (i, j)),
        out_shape=jax.ShapeDtypeStruct((m, n), _F32),
        compiler_params=_params(("parallel", "parallel", "arbitrary"), vmem),
        name="peer_out",
    )(x, w, residual)


def _layer(h, mem, norm_mix, w_in, gla_w_gate_up, gla_gate_bias, gla_out_norm, swa_q_norm, swa_k_norm,
           swa_sinks, mem_norm, w_mem_kv, mem_q_norm, mem_k_norm, w_branch_gla, w_branch_swa,
           w_branch_mem, w_out, norm_ffn, peer_w_q, peer_sub_keys, peer_u, peer_v, batch, seq, l):
    t, d = h.shape
    mem_len = mem.shape[0] // batch
    qk = GLA_HEADS * GLA_DK
    gv = GLA_HEADS * GLA_DV
    swq = SWA_HEADS * SWA_HD
    skv = SWA_KV_HEADS * SWA_HD
    o_glr = 2 * qk + 2 * gv
    o_sq = o_glr + GLA_RANK
    o_sk = o_sq + swq
    o_mq = o_sk + 2 * skv
    o_gate = o_mq + d
    bf = lambda w: w.astype(_BF16)
    hd = d // MEM_HEADS

    xn = _rmsnorm(h, norm_mix, _BF16)
    proj = functools.partial(_matmul_w32, xn, jnp.swapaxes(w_in, 1, 2), l, transposed=True)
    qkvr = proj([(0, o_glr)], _BF16, name="proj_gla")
    glr = proj([(o_glr, LANES)], _BF16, name="proj_glr")
    skvp = proj([(o_sk, 2 * skv)], _BF16, name="proj_swa_kv")

    tm_side = _tile(t, 512)
    ni = t // tm_side
    wg = bf(jnp.pad(gla_w_gate_up, ((0, LANES - GLA_RANK), (0, 0))))
    gla = _gla_side(qkvr, glr, wg, gla_gate_bias.reshape(1, qk).astype(_F32),
                    gla_out_norm.reshape(1, gv).astype(_F32), batch, seq, ni)
    sqmq, o_a = proj([(o_sq, swq), (o_mq, d)], _BF16, tm=tm_side, side=gla, name="proj_q_gla")
    swa = _swa_side(sqmq, 0, skvp, 0, swa_sinks, swa_q_norm, swa_k_norm, batch, seq, ni)
    gate, o_b = proj([(o_gate, 3 * d)], _BF16, tm=tm_side, tn=768, side=swa, name="proj_gate_swa")

    memn = _rmsnorm(mem, mem_norm, _BF16)
    kv_m = _matmul_w32(memn, w_mem_kv, l, [(0, 2 * d)], _BF16, name="mem_kv")
    k_c = _rmsnorm(kv_m[:, :d].reshape(-1, hd), mem_k_norm, _BF16).reshape(-1, d)
    mq, mq_blk = (sqmq, swq // hd) if swq % hd == 0 else (sqmq[:, swq:], 0)
    o_c = _memattn(mq, k_c, kv_m, mem_q_norm, batch, seq, mem_len, d, mq_blk, MEM_HEADS)

    mix = _merge(o_a, o_b, o_c, bf(w_branch_gla), bf(w_branch_swa), bf(w_branch_mem), gate)
    h = _matmul_w32(mix, w_out, l, [(0, d)], _F32, residual=h, name="out_proj")

    hn = _rmsnorm(h, norm_ffn, _BF16)
    pq = _matmul_w32(hn, peer_w_q, l, [(0, peer_w_q.shape[2])], _F32, name="peer_q")
    w = _route(pq, peer_sub_keys.astype(_F32))
    s = _peer_act(hn, bf(peer_u), w)
    return _matmul_acc(s, bf(peer_v), h)


def kernel(x, mem, norm_mix, w_in, gla_w_gate_up, gla_gate_bias, gla_out_norm, swa_q_norm, swa_k_norm,
           swa_sinks, mem_norm, w_mem_kv, mem_q_norm, mem_k_norm, w_branch_gla, w_branch_swa, w_branch_mem,
           w_out, norm_ffn, peer_w_q, peer_sub_keys, peer_u, peer_v):
    batch, seq, d = x.shape
    h = x.reshape(batch * seq, d)
    memf = mem.reshape(-1, d)
    for l in range(norm_mix.shape[0]):
        h = _layer(h, memf, norm_mix[l], w_in, gla_w_gate_up[l], gla_gate_bias[l], gla_out_norm[l],
                   swa_q_norm[l], swa_k_norm[l], swa_sinks[l], mem_norm[l], w_mem_kv, mem_q_norm[l],
                   mem_k_norm[l], w_branch_gla[l], w_branch_swa[l], w_branch_mem[l], w_out, norm_ffn[l],
                   peer_w_q, peer_sub_keys[l], peer_u[l], peer_v[l], batch, seq, l)
    return h.reshape(batch, seq, d)
```

```python
import functools

import jax
import jax.numpy as jnp
from jax import lax
from jax.experimental import pallas as pl
from jax.experimental.pallas import tpu as pltpu

_F32 = jnp.float32
_BF16 = jnp.bfloat16
_NEG_INF = float("-inf")

RMS_EPS = 1e-6
GLA_HEADS = 8
GLA_DK = 256
GLA_DV = 512
GLA_RANK = 16
GLA_TAU = 16.0
GLA_SUB = 8
GLA_STEP_HEADS = 2
GLA_STEP_CHUNKS = 2
_LOG2_E = 1.4426950408889634
SWA_HEADS = 64
SWA_KV_HEADS = 8
SWA_HD = 64
SWA_BLOCK = 128
MEM_HEADS = 4
PEER_HEADS = 8
PEER_NKEYS = 128
PEER_DQ = 256
PEER_TOPK = 16

LANES = 128
V7X_VMEM_BYTES = 64 * 1024 * 1024


def _tile(n, pref):
    t = min(n, pref)
    while n % t:
        t -= 1
    return t


def _params(semantics, vmem_bytes):
    limit = min(int(vmem_bytes) + (8 << 20), V7X_VMEM_BYTES - (4 << 20))
    return pltpu.CompilerParams(dimension_semantics=semantics, vmem_limit_bytes=limit)


def _nt(a, b):
    return lax.dot_general(a, b, (((1,), (1,)), ((), ())), preferred_element_type=_F32)


def _log2(n):
    assert n > 0 and n & (n - 1) == 0, n
    return n.bit_length() - 1


def _tn(a, b):
    return lax.dot_general(a, b, (((0,), (0,)), ((), ())), preferred_element_type=_F32)


def _rmsnorm_kernel(x_ref, g_ref, o_ref):
    x = x_ref[...].astype(_F32)
    ms = jnp.mean(x * x, axis=-1, keepdims=True)
    o_ref[...] = (x * lax.rsqrt(ms + RMS_EPS) * g_ref[...]).astype(o_ref.dtype)


def _rmsnorm(x, gain, out_dtype, rows=256):
    r, d = x.shape
    tr = _tile(r, rows)
    blk = tr * d * (x.dtype.itemsize + jnp.dtype(out_dtype).itemsize)
    return pl.pallas_call(
        _rmsnorm_kernel,
        grid=(r // tr,),
        in_specs=[pl.BlockSpec((tr, d), lambda i: (i, 0)), pl.BlockSpec((1, d), lambda i: (0, 0))],
        out_specs=pl.BlockSpec((tr, d), lambda i: (i, 0)),
        out_shape=jax.ShapeDtypeStruct((r, d), out_dtype),
        compiler_params=_params(("parallel",), 2 * blk + 4 * tr * d * 4),
        name="rmsnorm",
    )(x, gain.reshape(1, d).astype(_F32))


def _mm_kernel(x_ref, w_ref, o_ref):
    o_ref[...] = jnp.dot(x_ref[...], w_ref[...], preferred_element_type=_F32).astype(o_ref.dtype)


def _mm_res_kernel(x_ref, w_ref, r_ref, o_ref):
    acc = jnp.dot(x_ref[...], w_ref[...], preferred_element_type=_F32)
    o_ref[...] = (acc + r_ref[...].astype(_F32)).astype(o_ref.dtype)


def _matmul(x, w, out_dtype, tm=1024, tn=512, residual=None, name="matmul"):
    m, k = x.shape
    n = w.shape[1]
    tm, tn = _tile(m, tm), _tile(n, tn)
    osz = jnp.dtype(out_dtype).itemsize
    vmem = 2 * (tm * k * x.dtype.itemsize + k * tn * w.dtype.itemsize + tm * tn * osz) + tm * tn * 4
    in_specs = [pl.BlockSpec((tm, k), lambda j, i: (i, 0)), pl.BlockSpec((k, tn), lambda j, i: (0, j))]
    args = [x, w]
    kern = _mm_kernel
    if residual is not None:
        in_specs.append(pl.BlockSpec((tm, tn), lambda j, i: (i, j)))
        args.append(residual)
        kern = _mm_res_kernel
        vmem += 2 * tm * tn * residual.dtype.itemsize
    return pl.pallas_call(
        kern,
        grid=(n // tn, m // tm),
        in_specs=in_specs,
        out_specs=pl.BlockSpec((tm, tn), lambda j, i: (i, j)),
        out_shape=jax.ShapeDtypeStruct((m, n), out_dtype),
        compiler_params=_params(("parallel", "parallel"), vmem),
        name=name,
    )(*args)


class _Side:
    def __init__(self, kernel, parts, args, in_specs, out_specs, out_shape, scratch_shapes=(), vmem=0):
        self.kernel, self.parts, self.args, self.in_specs = kernel, parts, list(args), list(in_specs)
        self.out_specs, self.out_shape = list(out_specs), list(out_shape)
        self.scratch_shapes, self.vmem = list(scratch_shapes), vmem


def _mm_w32_kernel(*refs, shift, has_residual, side, transposed):
    n_in = 2 + bool(shift) + has_residual
    n_side_in = len(side.in_specs) if side else 0
    n_side_out = len(side.out_specs) if side else 0
    x_ref, w_ref = refs[0], refs[1]
    o_ref = refs[n_in + n_side_in]
    wbf_ref = refs[n_in + n_side_in + 1 + n_side_out]
    tn = o_ref.shape[1]

    @pl.when(pl.program_id(1) == 0)
    def _():
        rows = _tile(w_ref.shape[0], 128 if transposed else 512)
        for r0 in range(0, w_ref.shape[0], rows):
            rs = slice(r0, r0 + rows)
            if transposed:
                if r0 + rows + shift <= tn:
                    w = w_ref[r0 + shift:r0 + rows + shift, :]
                else:
                    w = jnp.concatenate([w_ref[r0 + shift:, :], refs[2][:shift, :]], axis=0)
                wbf_ref[:, rs] = w.T.astype(_BF16)
            else:
                w = w_ref[rs, :]
                if shift:
                    w = jnp.concatenate([w, refs[2][rs, :]], axis=1)
                    w = pltpu.roll(w, w.shape[1] - shift, axis=1)[:, :tn]
                wbf_ref[rs, :] = w.astype(_BF16)

    def matmul_rows(rs):
        acc = jnp.dot(x_ref[rs, :], wbf_ref[...], preferred_element_type=_F32)
        if has_residual:
            acc = acc + refs[n_in - 1][rs, :].astype(_F32)
        o_ref[rs, :] = acc.astype(o_ref.dtype)

    if side:
        rows = x_ref.shape[0] // side.parts
        parts = [functools.partial(matmul_rows, slice(r * rows, (r + 1) * rows)) for r in range(side.parts)]
        step = pl.program_id(0) * pl.num_programs(1) + pl.program_id(1)
        side.kernel(step, parts, *refs[n_in:n_in + n_side_in],
                    *refs[n_in + n_side_in + 1:n_in + n_side_in + 1 + n_side_out],
                    *refs[n_in + n_side_in + 2 + n_side_out:])
    else:
        matmul_rows(slice(None))


def _matmul_w32(x, w, layer, windows, out_dtype, tm=1024, tn=512, residual=None, side=None,
                transposed=False, name="matmul_w32"):
    m, k = x.shape
    shift = windows[0][0] % LANES
    assert all(c0 % LANES == shift for c0, _ in windows)
    tm = _tile(m, tm)
    ok = lambda t: all(n % t == 0 and (c0 - shift) % t == 0 for c0, n in windows)
    tn = max(t for t in range(LANES, tn + 1, LANES) if ok(t))
    n = sum(n for _, n in windows)

    def col_block(j, width):
        blk, start = 0, 0
        for c0, wn in windows:
            first = ((c0 - shift) // tn - start) * (tn // width)
            blk = jnp.where(j >= start, first + j * (tn // width), blk)
            start += wn // tn
        return blk

    osz = jnp.dtype(out_dtype).itemsize

    def wspec(width, block_of):
        if transposed:
            return pl.BlockSpec((None, width, k), lambda j, i: (layer, block_of(j), 0))
        return pl.BlockSpec((None, k, width), lambda j, i: (layer, 0, block_of(j)))

    in_specs = [pl.BlockSpec((tm, k), lambda j, i: (i, 0)), wspec(tn, lambda j: col_block(j, tn))]
    args = [x, w]
    vmem = 2 * (tm * k * 2 + k * tn * 4 + tm * tn * osz) + k * tn * 2 + tm * tn * 4
    if shift:
        in_specs.append(wspec(LANES, lambda j: col_block(j, LANES) + tn // LANES))
        args.append(w)
        vmem += 2 * k * LANES * 4
    if residual is not None:
        in_specs.append(pl.BlockSpec((tm, tn), lambda j, i: (i, j)))
        args.append(residual)
        vmem += 2 * tm * tn * residual.dtype.itemsize
    out_specs = [pl.BlockSpec((tm, tn), lambda j, i: (i, j))]
    out_shape = [jax.ShapeDtypeStruct((m, n), out_dtype)]
    scratch = [pltpu.VMEM((k, tn), _BF16)]
    alone = None
    if side:
        side, steps = side
        if steps != (n // tn) * (m // tm):
            alone, side = _run_side(side, steps, m // tm, name + "_side"), None
    if side:
        in_specs += side.in_specs
        args += side.args
        out_specs += side.out_specs
        out_shape += side.out_shape
        scratch += side.scratch_shapes
        vmem += side.vmem
    out = pl.pallas_call(
        functools.partial(_mm_w32_kernel, shift=shift, has_residual=residual is not None, side=side,
                          transposed=transposed),
        grid=(n // tn, m // tm),
        in_specs=in_specs,
        out_specs=out_specs,
        out_shape=out_shape,
        scratch_shapes=scratch,
        compiler_params=_params(("arbitrary", "arbitrary"), vmem),
        name=name,
    )(*args)
    if alone is not None:
        return [out[0], *alone]
    return out if side else out[0]


def _log_sigmoid(z):
    return jnp.minimum(z, 0.0) - jnp.log(1.0 + jnp.exp(-jnp.abs(z)))


def _cumsum_rows(x):
    c, n = x.shape
    row = lax.broadcasted_iota(jnp.int32, (c, c), 0)
    col = lax.broadcasted_iota(jnp.int32, (c, c), 1)
    tri = jnp.where(row >= col, 1.0, 0.0).astype(_BF16)
    hi = x.astype(_BF16)
    rest = x - hi.astype(_F32)
    mid = rest.astype(_BF16)
    lo = (rest - mid.astype(_F32)).astype(_BF16)
    parts = jnp.dot(tri, jnp.concatenate([hi, mid, lo], axis=1), preferred_element_type=_F32)
    return parts[:, :n] + parts[:, n:2 * n] + parts[:, 2 * n:]


def _gla_chunk(q, k, v, b, state, midway):
    c = q.shape[0]
    row = lax.broadcasted_iota(jnp.int32, (c, c), 0)
    col = lax.broadcasted_iota(jnp.int32, (c, c), 1)
    b_last = b[c - 1:c, :]

    attn = jnp.zeros((c, c), _F32)
    row1 = lax.broadcasted_iota(jnp.int32, (c, 1), 0)
    s = c // 2
    while s >= GLA_SUB:
        pieces = []
        for lo in range(0, c, 2 * s):
            mid = lo + s
            ref_row = b[mid - 1:mid, :]
            pieces.append(ref_row - b[lo:mid, :])
            pieces.append(b[mid:mid + s, :] - ref_row)
        x = jnp.exp2(jnp.concatenate(pieces, axis=0))
        upper = ((row1 >> _log2(s)) & 1) == 1
        qt = jnp.where(upper, q * x, 0.0).astype(_BF16)
        kt = jnp.where(upper, 0.0, k * x).astype(_BF16)
        same = (row >> _log2(2 * s)) == (col >> _log2(2 * s))
        attn = attn + jnp.where(same, _nt(qt, kt), 0.0)
        s //= 2

    midway()
    sub_row = lax.broadcasted_iota(jnp.int32, (GLA_SUB, 1), 0)
    lane = lax.broadcasted_iota(jnp.int32, (GLA_SUB, c), 1)
    diag = []
    for i0 in range(0, c, GLA_SUB):
        qi, ki, bi = q[i0:i0 + GLA_SUB, :], k[i0:i0 + GLA_SUB, :], b[i0:i0 + GLA_SUB, :]
        blk = jnp.zeros((GLA_SUB, c), _F32)
        for j in range(GLA_SUB):
            dec = jnp.exp2(jnp.where(sub_row >= j, bi - bi[j:j + 1, :], _NEG_INF))
            sc = jnp.sum(qi * dec * ki[j:j + 1, :], axis=1, keepdims=True)
            blk = jnp.where(lane == i0 + j, sc, blk)
        diag.append(blk)
    attn = attn + jnp.concatenate(diag, axis=0)

    qb = (q * jnp.exp2(b)).astype(_BF16)
    o = _nt(qb, state.astype(_BF16)) + jnp.dot(attn.astype(_BF16), v, preferred_element_type=_F32)
    kd = (k * jnp.exp2(b_last - b)).astype(_BF16)
    return o, state * jnp.exp2(b_last) + _tn(v, kd)


def _gla_kernel(step, parts, q_ref, k_ref, v_ref, r_ref, glr_ref, wg_ref, bias_ref, gain_ref, o_ref,
                state_ref, *, nc):
    @pl.when(step % nc == 0)
    def _():
        state_ref[...] = jnp.zeros_like(state_ref)

    c = q_ref.shape[0] // GLA_STEP_CHUNKS
    z = jnp.dot(glr_ref[...], wg_ref[...], preferred_element_type=_F32) + bias_ref[...]
    log_a = _log_sigmoid(z) * (_LOG2_E / GLA_TAU)
    b_all = [_cumsum_rows(log_a[s * c:(s + 1) * c, :]) for s in range(GLA_STEP_CHUNKS)]
    states = [state_ref[h] for h in range(GLA_STEP_HEADS)]
    parts = iter(parts)
    for s in range(GLA_STEP_CHUNKS):
        rows = slice(s * c, (s + 1) * c)
        for h in range(GLA_STEP_HEADS):
            next(parts)()
            ks = slice(h * GLA_DK, (h + 1) * GLA_DK)
            vs = slice(h * GLA_DV, (h + 1) * GLA_DV)
            q = q_ref[rows, ks].astype(_F32) * (GLA_DK ** -0.5)
            o, states[h] = _gla_chunk(q, k_ref[rows, ks].astype(_F32), v_ref[rows, vs], b_all[s][:, ks],
                                      states[h], next(parts))
            ms = jnp.mean(o * o, axis=-1, keepdims=True)
            on = o * lax.rsqrt(ms + RMS_EPS) * gain_ref[:, vs]
            r = r_ref[rows, vs].astype(_F32)
            o_ref[rows, vs] = (on * (r * jax.nn.sigmoid(r))).astype(o_ref.dtype)
    for h in range(GLA_STEP_HEADS):
        state_ref[h] = states[h]


def _run_side(side, steps, ni, name):
    def kern(*refs):
        side.kernel(pl.program_id(0) * pl.num_programs(1) + pl.program_id(1),
                    [lambda: None] * side.parts, *refs)

    return pl.pallas_call(
        kern, grid=(steps // ni, ni), in_specs=side.in_specs, out_specs=side.out_specs,
        out_shape=side.out_shape, scratch_shapes=side.scratch_shapes,
        compiler_params=_params(("arbitrary", "arbitrary"), side.vmem), name=name)(*side.args)


def _gla_side(qkvr, glr, wg, bias, gain, batch, seq, ni, chunk=128):
    t = batch * seq
    chunk = chunk * GLA_STEP_CHUNKS
    nc = seq // chunk
    hs = GLA_STEP_HEADS
    wk, wv = hs * GLA_DK, hs * GLA_DV
    groups = GLA_HEADS // hs
    kv = 2 * groups * wk // wv

    def at(col):
        def index(j, i):
            s = j * ni + i
            return (s // (groups * nc)) * nc + s % nc, col + (s // nc) % groups
        return index

    head = lambda j, i: (0, ((j * ni + i) // nc) % groups)
    in_specs = [
        pl.BlockSpec((chunk, wk), at(0)),
        pl.BlockSpec((chunk, wk), at(groups)),
        pl.BlockSpec((chunk, wv), at(kv)),
        pl.BlockSpec((chunk, wv), at(kv + groups)),
        pl.BlockSpec((chunk, LANES), lambda j, i: (at(0)(j, i)[0], 0)),
        pl.BlockSpec((LANES, wk), head),
        pl.BlockSpec((1, wk), head),
        pl.BlockSpec((1, wv), head),
    ]
    side = _Side(functools.partial(_gla_kernel, nc=nc), 2 * hs * GLA_STEP_CHUNKS,
                 [qkvr, qkvr, qkvr, qkvr, glr, wg, bias, gain], in_specs,
                 [pl.BlockSpec((chunk, wv), at(0))], [jax.ShapeDtypeStruct((t, GLA_HEADS * GLA_DV), _BF16)],
                 [pltpu.VMEM((hs, GLA_DV, GLA_DK), _F32)], vmem=16 << 20)
    return side, batch * groups * nc


def _pair_rmsnorm(x, gain2):
    lane = lax.broadcasted_iota(jnp.int32, x.shape, 1)
    low = lane < SWA_HD
    sq = x * x
    s_all = jnp.sum(sq, axis=1, keepdims=True)
    s_low = jnp.sum(jnp.where(low, sq, 0.0), axis=1, keepdims=True)
    ms = jnp.where(low, s_low, s_all - s_low) * (1.0 / SWA_HD)
    return x * lax.rsqrt(ms + RMS_EPS) * gain2


def _swa_kernel(step, parts, sinks_ref, q_ref, kc_ref, kp_ref, vc_ref, vp_ref, qg_ref, kg_ref, o_ref, *, nb):
    npairs = SWA_KV_HEADS // 2
    blk = (step // npairs) % nb
    p = step % npairs
    bs = SWA_BLOCK
    group = SWA_HEADS // SWA_KV_HEADS
    pairs = group // 2
    lane = lax.broadcasted_iota(jnp.int32, (bs, LANES), 1)
    low = lane < SWA_HD

    q_loc = lax.broadcasted_iota(jnp.int32, (group * bs, bs), 0) & (bs - 1)
    from_cur = lax.broadcasted_iota(jnp.int32, (group * bs, bs), 1) <= q_loc
    prev_bias = jnp.where(blk > 0, 0.0, _NEG_INF)

    tiles = [_pair_rmsnorm(kc_ref[...].astype(_F32), kg_ref[...]),
             _pair_rmsnorm(kp_ref[...].astype(_F32), kg_ref[...]),
             vc_ref[...].astype(_F32), vp_ref[...].astype(_F32)]
    swapped = [pltpu.roll(x, SWA_HD, axis=1) for x in tiles]
    for half in range(2):
        kc, kp, vc, vp = [(jnp.where(low, x, xs) if half == 0 else jnp.where(low, xs, x)).astype(_BF16)
                          for x, xs in zip(tiles, swapped)]
        qs, sink = [], []
        for t in range(pairs):
            c0 = half * group * SWA_HD + t * LANES
            qn = _pair_rmsnorm(q_ref[:, c0:c0 + LANES].astype(_F32), qg_ref[...]) * (SWA_HD ** -0.5)
            qs.append(jnp.where(low, qn, 0.0))
            qs.append(jnp.where(low, 0.0, qn))
            head = (2 * p + half) * group + 2 * t
            sink.append(jnp.full((bs, 1), sinks_ref[head], _F32))
            sink.append(jnp.full((bs, 1), sinks_ref[head + 1], _F32))
        qstack = jnp.concatenate(qs, axis=0).astype(_BF16)
        sink = jnp.concatenate(sink, axis=0)
        s = jnp.where(from_cur, _nt(qstack, kc), _nt(qstack, kp) + prev_bias)
        m = jnp.maximum(jnp.max(s, axis=1, keepdims=True), sink)
        e = jnp.exp(s - m)
        denom = jnp.sum(e, axis=1, keepdims=True) + jnp.exp(sink - m)
        e_cur = jnp.where(from_cur, e, 0.0).astype(_BF16)
        e_prev = jnp.where(from_cur, 0.0, e).astype(_BF16)
        o = (jnp.dot(e_cur, vc, preferred_element_type=_F32)
             + jnp.dot(e_prev, vp, preferred_element_type=_F32)) / denom
        for t in range(pairs):
            c0 = half * group * SWA_HD + t * LANES
            oa = o[(2 * t) * bs:(2 * t + 1) * bs, :]
            ob = o[(2 * t + 1) * bs:(2 * t + 2) * bs, :]
            o_ref[:, c0:c0 + LANES] = jnp.where(low, oa, ob).astype(o_ref.dtype)
        parts[half]()


def _swa_side(sq, q_col, skv, k_col, sinks, q_gain, k_gain, batch, seq, ni):
    t = batch * seq
    nb = seq // SWA_BLOCK
    npairs = SWA_KV_HEADS // 2
    kvw = SWA_KV_HEADS * SWA_HD
    pw = 2 * (SWA_HEADS // SWA_KV_HEADS) * SWA_HD
    assert q_col % pw == 0 and k_col % LANES == 0 and kvw % LANES == 0

    def at(col, prev=False):
        def index(j, i):
            s = j * ni + i
            blk = s // npairs
            if prev:
                blk = blk - jnp.where(blk % nb > 0, 1, 0)
            return blk, col + s % npairs
        return index

    const = lambda j, i: (0, 0)
    kc, vc = k_col // LANES, (k_col + kvw) // LANES
    in_specs = [
        pl.BlockSpec(memory_space=pltpu.SMEM),
        pl.BlockSpec((SWA_BLOCK, pw), at(q_col // pw)),
        pl.BlockSpec((SWA_BLOCK, LANES), at(kc)),
        pl.BlockSpec((SWA_BLOCK, LANES), at(kc, prev=True)),
        pl.BlockSpec((SWA_BLOCK, LANES), at(vc)),
        pl.BlockSpec((SWA_BLOCK, LANES), at(vc, prev=True)),
        pl.BlockSpec((1, LANES), const),
        pl.BlockSpec((1, LANES), const),
    ]
    g2 = lambda g: jnp.concatenate([g, g]).reshape(1, LANES).astype(_F32)
    side = _Side(functools.partial(_swa_kernel, nb=nb), 2,
                 [sinks.astype(_F32), sq, skv, skv, skv, skv, g2(q_gain), g2(k_gain)], in_specs,
                 [pl.BlockSpec((SWA_BLOCK, pw), at(0))],
                 [jax.ShapeDtypeStruct((t, SWA_HEADS * SWA_HD), _BF16)], vmem=16 << 20)
    return side, batch * nb * npairs


def _memattn_kernel(q_ref, k_ref, v_ref, g_ref, o_ref):
    hd = q_ref.shape[1]
    q = q_ref[...].astype(_F32)
    ms = jnp.mean(q * q, axis=-1, keepdims=True)
    qn = (q * lax.rsqrt(ms + RMS_EPS) * g_ref[...] * (hd ** -0.5)).astype(_BF16)
    s = _nt(qn, k_ref[...])
    m = jnp.max(s, axis=-1, keepdims=True)
    e = jnp.exp(s - m)
    denom = jnp.sum(e, axis=-1, keepdims=True)
    o = jnp.dot(e.astype(_BF16), v_ref[...], preferred_element_type=_F32) / denom
    o_ref[...] = o.astype(o_ref.dtype)


def _memattn(mq, kc, vm, q_gain, batch, seq, mem_len, d, q_blk=0, v_blk=0, tm=1024):
    t = mq.shape[0]
    tm = _tile(seq, tm)
    ns = seq // tm
    hd = d // MEM_HEADS
    return pl.pallas_call(
        _memattn_kernel,
        grid=(batch, ns, MEM_HEADS),
        in_specs=[
            pl.BlockSpec((tm, hd), lambda b, i, h: (b * ns + i, q_blk + h)),
            pl.BlockSpec((mem_len, hd), lambda b, i, h: (b, h)),
            pl.BlockSpec((mem_len, hd), lambda b, i, h: (b, v_blk + h)),
            pl.BlockSpec((1, hd), lambda b, i, h: (0, 0)),
        ],
        out_specs=pl.BlockSpec((tm, hd), lambda b, i, h: (b * ns + i, h)),
        out_shape=jax.ShapeDtypeStruct((t, d), _BF16),
        compiler_params=_params(("parallel", "parallel", "arbitrary"),
                                4 * tm * hd * 2 + 4 * mem_len * hd * 2 + 6 * tm * hd * 4),
        name="memattn",
    )(mq, kc, vm, q_gain.reshape(1, hd).astype(_F32))


def _merge_kernel(a_ref, b_ref, c_ref, wa_ref, wb_ref, wc_ref, ga_ref, gb_ref, gc_ref, o_ref):
    def branch(x_ref, w_ref, g_ref):
        y = jnp.dot(x_ref[...], w_ref[...], preferred_element_type=_F32)
        return jax.nn.sigmoid(g_ref[...].astype(_F32)) * y

    o_ref[...] = (branch(a_ref, wa_ref, ga_ref) + branch(b_ref, wb_ref, gb_ref)
                  + branch(c_ref, wc_ref, gc_ref)).astype(o_ref.dtype)


def _merge(oa, ob, oc, wa, wb, wc, gate, gate_col=0, tm=512, tn=512):
    t, d = oa.shape
    n = wa.shape[1]
    tm, tn = _tile(t, tm), _tile(n, tn)
    nn = n // tn
    if gate_col % tn:
        gate, gate_col = gate[:, gate_col:gate_col + 3 * n], 0
    g0 = gate_col // tn
    act = lambda o: pl.BlockSpec((tm, o.shape[1]), lambda j, i: (i, 0))
    wsp = lambda w: pl.BlockSpec((w.shape[0], tn), lambda j, i: (0, j), pipeline_mode=pl.Buffered(1))
    gsp = lambda b: pl.BlockSpec((tm, tn), lambda j, i: (i, g0 + b * nn + j))
    vmem = 3 * (2 * tm * d * 2 + d * tn * 2 + 2 * tm * tn * 2) + 8 * tm * tn * 4
    return pl.pallas_call(
        _merge_kernel,
        grid=(nn, t // tm),
        in_specs=[act(oa), act(ob), act(oc), wsp(wa), wsp(wb), wsp(wc), gsp(0), gsp(1), gsp(2)],
        out_specs=pl.BlockSpec((tm, tn), lambda j, i: (i, j)),
        out_shape=jax.ShapeDtypeStruct((t, n), _BF16),
        compiler_params=_params(("parallel", "arbitrary"), vmem),
        name="merge",
    )(oa, ob, oc, wa, wb, wc, gate, gate, gate)


def _extract_topk(s, payload, kk):
    n = s.shape[0]
    pos = lax.broadcasted_iota(jnp.int32, s.shape, 0).astype(_F32)
    vals, sel = [], []
    for _ in range(kk):
        m = jnp.max(s, axis=0, keepdims=True)
        first = jnp.min(jnp.where(s == m, pos, float(n)), axis=0, keepdims=True)
        hit = pos == first
        vals.append(m)
        sel.append(first if payload is None else jnp.max(jnp.where(hit, payload, -1.0), axis=0, keepdims=True))
        s = jnp.where(hit, _NEG_INF, s)
    return jnp.concatenate(vals, axis=0), jnp.concatenate(sel, axis=0)


def _route_kernel(h_ref, gain_ref, wq_ref, keys_ref, hn_ref, w_ref, q_ref, i1_ref, i2_ref, g_ref):
    kk = PEER_TOPK
    half = PEER_DQ // 2
    nk = PEER_NKEYS
    step = pl.program_id(0)
    cur, prev = step % 2, (step + 1) % 2
    tokens = h_ref.shape[0] // PEER_HEADS

    @pl.when(step == 0)
    def _():
        q_ref[1] = jnp.zeros(q_ref.shape[1:], q_ref.dtype)
        i1_ref[...] = jnp.zeros_like(i1_ref)
        i2_ref[...] = jnp.zeros_like(i2_ref)
        g_ref[...] = jnp.zeros_like(g_ref)

    x = h_ref[...]
    ms = jnp.mean(x * x, axis=-1, keepdims=True)
    hn = (x * lax.rsqrt(ms + RMS_EPS) * gain_ref[...]).astype(_BF16)
    hn_ref[...] = hn

    sub = lax.broadcasted_iota(jnp.int32, (nk, i1_ref.shape[2]), 0)

    def gate_rows(t0):
        for t in range(t0, t0 + tokens):
            a_t = jnp.where(sub == i1_ref[cur, t:t + 1, :], 1.0, 0.0).astype(_BF16)
            b_t = jnp.where(sub == i2_ref[cur, t:t + 1, :], g_ref[cur, t:t + 1, :], 0.0).astype(_BF16)
            w_ref[t] = _nt(a_t, b_t)

    experts, gates = [], []
    for h in range(PEER_HEADS):
        hs = slice(h * PEER_DQ, (h + 1) * PEER_DQ)
        q_ref[cur, :, hs] = jnp.dot(hn, wq_ref[:, hs], preferred_element_type=_F32)
        top = []
        for p in range(2):
            c0 = (2 * h + p) * half
            st = _nt(keys_ref[2 * h + p].astype(_BF16), q_ref[prev, :, c0:c0 + half].astype(_BF16))
            top.append(_extract_topk(st, None, kk))
        (s0, i0), (s1, i1) = top
        cand, ids = [], []
        tail = kk // 2
        for a in range(tail):
            nb = kk // (a + 1)
            rows = -(-nb // 8) * 8
            c = s0[a:a + 1, :] + s1[0:rows, :]
            if rows != nb:
                c = jnp.where(lax.broadcasted_iota(jnp.int32, c.shape, 0) < nb, c, _NEG_INF)
            cand.append(c)
            ids.append(i0[a:a + 1, :] * PEER_NKEYS + i1[0:rows, :])
        cand.append(s0[tail:kk, :] + s1[0:1, :])
        ids.append(i0[tail:kk, :] * PEER_NKEYS + i1[0:1, :])
        best, e = _extract_topk(jnp.concatenate(cand, axis=0), jnp.concatenate(ids, axis=0), kk)
        ex = jnp.exp(best - best[0:1, :])
        gates.append(ex / jnp.sum(ex, axis=0, keepdims=True))
        experts.append(e)
        gate_rows(h * tokens)
    e = jnp.concatenate(experts, axis=0).T.astype(jnp.int32)
    i1_ref[prev] = e >> _log2(PEER_NKEYS)
    i2_ref[prev] = e & (PEER_NKEYS - 1)
    g_ref[prev] = jnp.concatenate(gates, axis=0).T


def _route(h, gain, wq, sub_keys, tt=128):
    t, d = h.shape
    tt = _tile(t, tt)
    n = t // tt
    j = PEER_HEADS * PEER_TOPK
    nk = PEER_NKEYS
    nq = PEER_HEADS * PEER_DQ
    keys = sub_keys.reshape(PEER_HEADS * 2, nk, PEER_DQ // 2)
    tile = lambda s: (jnp.minimum(s, n - 1), 0)
    vmem = d * nq * 2 + 2 * (tt * nk * nk * 4 + tt * d * 6) + 2 * tt * nq * 4 + (12 << 20)
    return pl.pallas_call(
        _route_kernel,
        grid=(n + 2,),
        in_specs=[pl.BlockSpec((tt, d), tile),
                  pl.BlockSpec((1, d), lambda s: (0, 0)),
                  pl.BlockSpec((d, nq), lambda s: (0, 0), pipeline_mode=pl.Buffered(1)),
                  pl.BlockSpec(keys.shape, lambda s: (0, 0, 0))],
        out_specs=[pl.BlockSpec((tt, d), tile),
                   pl.BlockSpec((tt, nk, nk), lambda s: (jnp.maximum(s - 2, 0), 0, 0))],
        out_shape=[jax.ShapeDtypeStruct((t, d), _BF16), jax.ShapeDtypeStruct((t, nk, nk), _F32)],
        scratch_shapes=[pltpu.VMEM((2, tt, nq), _F32), pltpu.VMEM((2, tt, j), jnp.int32),
                        pltpu.VMEM((2, tt, j), jnp.int32), pltpu.VMEM((2, tt, j), _F32)],
        compiler_params=_params(("arbitrary",), vmem),
        name="peer_route",
    )(h, gain.reshape(1, d).astype(_F32), wq, keys)


def _gelu(x):
    return 0.5 * x * (1.0 + lax.erf(x * (2.0 ** -0.5)))


def _peer_act_kernel(x_ref, u_ref, w_ref, s_ref):
    act = _gelu(_nt(x_ref[...], u_ref[...]))
    w = jnp.transpose(w_ref[...], (1, 0, 2))
    for a in range(w.shape[0]):
        cs = slice(a * PEER_NKEYS, (a + 1) * PEER_NKEYS)
        s_ref[:, cs] = (act[:, cs] * w[a]).astype(s_ref.dtype)


def _peer_act(xn, u, w, tm=1024, rows_a=8):
    t, d = xn.shape
    ne = u.shape[0]
    nk = PEER_NKEYS
    tm = _tile(t, tm)
    eb = rows_a * nk
    vmem = 2 * (tm * d * 2 + tm * rows_a * nk * 4 + tm * eb * 2) + eb * d * 2 + 3 * tm * eb * 4
    return pl.pallas_call(
        _peer_act_kernel,
        grid=(ne // eb, t // tm),
        in_specs=[pl.BlockSpec((tm, d), lambda j, i: (i, 0)),
                  pl.BlockSpec((eb, d), lambda j, i: (j, 0), pipeline_mode=pl.Buffered(1)),
                  pl.BlockSpec((tm, rows_a, nk), lambda j, i: (i, j, 0))],
        out_specs=pl.BlockSpec((tm, eb), lambda j, i: (i, j)),
        out_shape=jax.ShapeDtypeStruct((t, ne), _BF16),
        compiler_params=_params(("parallel", "parallel"), vmem),
        name="peer_act",
    )(xn, u, w)


def _mm_acc_kernel(x_ref, w_ref, r_ref, o_ref):
    acc = jnp.dot(x_ref[...], w_ref[...], preferred_element_type=_F32)

    @pl.when(pl.program_id(2) == 0)
    def _():
        o_ref[...] = r_ref[...] + acc

    @pl.when(pl.program_id(2) > 0)
    def _():
        o_ref[...] += acc


def _matmul_acc(x, w, residual, tm=1024, tn=1024, tk=4096):
    m, k = x.shape
    n = w.shape[1]
    tm, tn, tk = _tile(m, tm), _tile(n, tn), _tile(k, tk)
    vmem = 2 * (tm * tk * 2 + tk * tn * 2 + 2 * tm * tn * 4) + tm * tn * 4
    return pl.pallas_call(
        _mm_acc_kernel,
        grid=(m // tm, n // tn, k // tk),
        in_specs=[pl.BlockSpec((tm, tk), lambda i, j, l: (i, l)),
                  pl.BlockSpec((tk, tn), lambda i, j, l: (l, j)),
                  pl.BlockSpec((tm, tn), lambda i, j, l: (i, j))],
        out_specs=pl.BlockSpec((tm, tn), lambda i, j, l: (i, j)),
        out_shape=jax.ShapeDtypeStruct((m, n), _F32),
        compiler_params=_params(("parallel", "parallel", "arbitrary"), vmem),
        name="peer_out",
    )(x, w, residual)


def _layer(h, mem, norm_mix, w_in, gla_w_gate_up, gla_gate_bias, gla_out_norm, swa_q_norm, swa_k_norm,
           swa_sinks, mem_norm, w_mem_kv, mem_q_norm, mem_k_norm, w_branch_gla, w_branch_swa,
           w_branch_mem, w_out, norm_ffn, peer_w_q, peer_sub_keys, peer_u, peer_v, batch, seq, l):
    t, d = h.shape
    mem_len = mem.shape[0] // batch
    qk = GLA_HEADS * GLA_DK
    gv = GLA_HEADS * GLA_DV
    swq = SWA_HEADS * SWA_HD
    skv = SWA_KV_HEADS * SWA_HD
    o_glr = 2 * qk + 2 * gv
    o_sq = o_glr + GLA_RANK
    o_sk = o_sq + swq
    o_mq = o_sk + 2 * skv
    o_gate = o_mq + d
    bf = lambda w: w.astype(_BF16)
    hd = d // MEM_HEADS

    xn = _rmsnorm(h, norm_mix, _BF16)
    proj = functools.partial(_matmul_w32, xn, jnp.swapaxes(w_in, 1, 2), l, transposed=True)
    qkvr = proj([(0, o_glr)], _BF16, name="proj_gla")
    glr = proj([(o_glr, LANES)], _BF16, name="proj_glr")
    skvp = proj([(o_sk, 2 * skv)], _BF16, name="proj_swa_kv")

    tm_gla, tm_swa = _tile(t, 512 * GLA_STEP_CHUNKS), _tile(t, 512)
    wg = bf(jnp.pad(gla_w_gate_up, ((0, LANES - GLA_RANK), (0, 0))))
    gla = _gla_side(qkvr, glr, wg, gla_gate_bias.reshape(1, qk).astype(_F32),
                    gla_out_norm.reshape(1, gv).astype(_F32), batch, seq, t // tm_gla)
    sqmq, o_a = proj([(o_sq, swq), (o_mq, d)], _BF16, tm=tm_gla, side=gla, name="proj_q_gla")
    swa = _swa_side(sqmq, 0, skvp, 0, swa_sinks, swa_q_norm, swa_k_norm, batch, seq, t // tm_swa)
    gate, o_b = proj([(o_gate, 3 * d)], _BF16, tm=tm_swa, tn=768, side=swa, name="proj_gate_swa")

    memn = _rmsnorm(mem, mem_norm, _BF16)
    kv_m = _matmul_w32(memn, w_mem_kv, l, [(0, 2 * d)], _BF16, name="mem_kv")
    k_c = _rmsnorm(kv_m[:, :d].reshape(-1, hd), mem_k_norm, _BF16).reshape(-1, d)
    mq, mq_blk = (sqmq, swq // hd) if swq % hd == 0 else (sqmq[:, swq:], 0)
    o_c = _memattn(mq, k_c, kv_m, mem_q_norm, batch, seq, mem_len, d, mq_blk, MEM_HEADS)

    mix = _merge(o_a, o_b, o_c, bf(w_branch_gla), bf(w_branch_swa), bf(w_branch_mem), gate)
    h = _matmul_w32(mix, w_out, l, [(0, d)], _F32, residual=h, name="out_proj")

    hn, w = _route(h, norm_ffn, bf(peer_w_q[l]), peer_sub_keys.astype(_F32))
    s = _peer_act(hn, bf(peer_u), w)
    return _matmul_acc(s, bf(peer_v), h)


def kernel(x, mem, norm_mix, w_in, gla_w_gate_up, gla_gate_bias, gla_out_norm, swa_q_norm, swa_k_norm,
           swa_sinks, mem_norm, w_mem_kv, mem_q_norm, mem_k_norm, w_branch_gla, w_branch_swa, w_branch_mem,
           w_out, norm_ffn, peer_w_q, peer_sub_keys, peer_u, peer_v):
    batch, seq, d = x.shape
    h = x.reshape(batch * seq, d)
    memf = mem.reshape(-1, d)
    for l in range(norm_mix.shape[0]):
        h = _layer(h, memf, norm_mix[l], w_in, gla_w_gate_up[l], gla_gate_bias[l], gla_out_norm[l],
                   swa_q_norm[l], swa_k_norm[l], swa_sinks[l], mem_norm[l], w_mem_kv, mem_q_norm[l],
                   mem_k_norm[l], w_branch_gla[l], w_branch_swa[l], w_branch_mem[l], w_out, norm_ffn[l],
                   peer_w_q, peer_sub_keys[l], peer_u[l], peer_v[l], batch, seq, l)
    return h.reshape(batch, seq, d)
```

```python
import functools

import jax
import jax.numpy as jnp
from jax import lax
from jax.experimental import pallas as pl
from jax.experimental.pallas import tpu as pltpu

_F32 = jnp.float32
_BF16 = jnp.bfloat16
_NEG_INF = float("-inf")

RMS_EPS = 1e-6
GLA_HEADS = 8
GLA_DK = 256
GLA_DV = 512
GLA_RANK = 16
GLA_TAU = 16.0
GLA_SUB = 8
GLA_STEP_HEADS = 2
GLA_STEP_CHUNKS = 2
_LOG2_E = 1.4426950408889634
SWA_HEADS = 64
SWA_KV_HEADS = 8
SWA_HD = 64
SWA_BLOCK = 128
SWA_STEP_PAIRS = 2
MEM_HEADS = 4
PEER_HEADS = 8
PEER_NKEYS = 128
PEER_DQ = 256
PEER_TOPK = 16

LANES = 128
V7X_VMEM_BYTES = 64 * 1024 * 1024


def _tile(n, pref):
    t = min(n, pref)
    while n % t:
        t -= 1
    return t


def _params(semantics, vmem_bytes):
    limit = min(int(vmem_bytes) + (8 << 20), V7X_VMEM_BYTES - (4 << 20))
    return pltpu.CompilerParams(dimension_semantics=semantics, vmem_limit_bytes=limit)


def _nt(a, b):
    return lax.dot_general(a, b, (((1,), (1,)), ((), ())), preferred_element_type=_F32)


def _log2(n):
    assert n > 0 and n & (n - 1) == 0, n
    return n.bit_length() - 1


def _tn(a, b):
    return lax.dot_general(a, b, (((0,), (0,)), ((), ())), preferred_element_type=_F32)


def _rmsnorm_kernel(x_ref, g_ref, o_ref):
    x = x_ref[...].astype(_F32)
    ms = jnp.mean(x * x, axis=-1, keepdims=True)
    o_ref[...] = (x * lax.rsqrt(ms + RMS_EPS) * g_ref[...]).astype(o_ref.dtype)


def _rmsnorm(x, gain, out_dtype, rows=256):
    r, d = x.shape
    tr = _tile(r, rows)
    blk = tr * d * (x.dtype.itemsize + jnp.dtype(out_dtype).itemsize)
    return pl.pallas_call(
        _rmsnorm_kernel,
        grid=(r // tr,),
        in_specs=[pl.BlockSpec((tr, d), lambda i: (i, 0)), pl.BlockSpec((1, d), lambda i: (0, 0))],
        out_specs=pl.BlockSpec((tr, d), lambda i: (i, 0)),
        out_shape=jax.ShapeDtypeStruct((r, d), out_dtype),
        compiler_params=_params(("parallel",), 2 * blk + 4 * tr * d * 4),
        name="rmsnorm",
    )(x, gain.reshape(1, d).astype(_F32))


def _mm_kernel(x_ref, w_ref, o_ref):
    o_ref[...] = jnp.dot(x_ref[...], w_ref[...], preferred_element_type=_F32).astype(o_ref.dtype)


def _mm_res_kernel(x_ref, w_ref, r_ref, o_ref):
    acc = jnp.dot(x_ref[...], w_ref[...], preferred_element_type=_F32)
    o_ref[...] = (acc + r_ref[...].astype(_F32)).astype(o_ref.dtype)


def _matmul(x, w, out_dtype, tm=1024, tn=512, residual=None, name="matmul"):
    m, k = x.shape
    n = w.shape[1]
    tm, tn = _tile(m, tm), _tile(n, tn)
    osz = jnp.dtype(out_dtype).itemsize
    vmem = 2 * (tm * k * x.dtype.itemsize + k * tn * w.dtype.itemsize + tm * tn * osz) + tm * tn * 4
    in_specs = [pl.BlockSpec((tm, k), lambda j, i: (i, 0)), pl.BlockSpec((k, tn), lambda j, i: (0, j))]
    args = [x, w]
    kern = _mm_kernel
    if residual is not None:
        in_specs.append(pl.BlockSpec((tm, tn), lambda j, i: (i, j)))
        args.append(residual)
        kern = _mm_res_kernel
        vmem += 2 * tm * tn * residual.dtype.itemsize
    return pl.pallas_call(
        kern,
        grid=(n // tn, m // tm),
        in_specs=in_specs,
        out_specs=pl.BlockSpec((tm, tn), lambda j, i: (i, j)),
        out_shape=jax.ShapeDtypeStruct((m, n), out_dtype),
        compiler_params=_params(("parallel", "parallel"), vmem),
        name=name,
    )(*args)


class _Side:
    def __init__(self, kernel, parts, args, in_specs, out_specs, out_shape, scratch_shapes=(), vmem=0):
        self.kernel, self.parts, self.args, self.in_specs = kernel, parts, list(args), list(in_specs)
        self.out_specs, self.out_shape = list(out_specs), list(out_shape)
        self.scratch_shapes, self.vmem = list(scratch_shapes), vmem


def _mm_w32_kernel(*refs, shift, has_residual, side, transposed):
    n_in = 2 + bool(shift) + has_residual
    n_side_in = len(side.in_specs) if side else 0
    n_side_out = len(side.out_specs) if side else 0
    x_ref, w_ref = refs[0], refs[1]
    o_ref = refs[n_in + n_side_in]
    wbf_ref = refs[n_in + n_side_in + 1 + n_side_out]
    tn = o_ref.shape[1]

    @pl.when(pl.program_id(1) == 0)
    def _():
        rows = _tile(w_ref.shape[0], 128 if transposed else 512)
        for r0 in range(0, w_ref.shape[0], rows):
            rs = slice(r0, r0 + rows)
            if transposed:
                if r0 + rows + shift <= tn:
                    w = w_ref[r0 + shift:r0 + rows + shift, :]
                else:
                    w = jnp.concatenate([w_ref[r0 + shift:, :], refs[2][:shift, :]], axis=0)
                wbf_ref[:, rs] = w.T.astype(_BF16)
            else:
                w = w_ref[rs, :]
                if shift:
                    w = jnp.concatenate([w, refs[2][rs, :]], axis=1)
                    w = pltpu.roll(w, w.shape[1] - shift, axis=1)[:, :tn]
                wbf_ref[rs, :] = w.astype(_BF16)

    def matmul_rows(rs):
        acc = jnp.dot(x_ref[rs, :], wbf_ref[...], preferred_element_type=_F32)
        if has_residual:
            acc = acc + refs[n_in - 1][rs, :].astype(_F32)
        o_ref[rs, :] = acc.astype(o_ref.dtype)

    if side:
        rows = x_ref.shape[0] // side.parts
        parts = [functools.partial(matmul_rows, slice(r * rows, (r + 1) * rows)) for r in range(side.parts)]
        step = pl.program_id(0) * pl.num_programs(1) + pl.program_id(1)
        side.kernel(step, parts, *refs[n_in:n_in + n_side_in],
                    *refs[n_in + n_side_in + 1:n_in + n_side_in + 1 + n_side_out],
                    *refs[n_in + n_side_in + 2 + n_side_out:])
    else:
        matmul_rows(slice(None))


def _matmul_w32(x, w, layer, windows, out_dtype, tm=1024, tn=512, residual=None, side=None,
                transposed=False, weight_buffers=2, name="matmul_w32"):
    m, k = x.shape
    shift = windows[0][0] % LANES
    assert all(c0 % LANES == shift for c0, _ in windows)
    tm = _tile(m, tm)
    ok = lambda t: all(n % t == 0 and (c0 - shift) % t == 0 for c0, n in windows)
    tn = max(t for t in range(LANES, tn + 1, LANES) if ok(t))
    n = sum(n for _, n in windows)

    def col_block(j, width):
        blk, start = 0, 0
        for c0, wn in windows:
            first = ((c0 - shift) // tn - start) * (tn // width)
            blk = jnp.where(j >= start, first + j * (tn // width), blk)
            start += wn // tn
        return blk

    osz = jnp.dtype(out_dtype).itemsize

    def wspec(width, block_of):
        mode = {} if weight_buffers == 2 else dict(pipeline_mode=pl.Buffered(weight_buffers))
        if transposed:
            return pl.BlockSpec((None, width, k), lambda j, i: (layer, block_of(j), 0), **mode)
        return pl.BlockSpec((None, k, width), lambda j, i: (layer, 0, block_of(j)), **mode)

    in_specs = [pl.BlockSpec((tm, k), lambda j, i: (i, 0)), wspec(tn, lambda j: col_block(j, tn))]
    args = [x, w]
    vmem = 2 * (tm * k * 2 + tm * tn * osz) + weight_buffers * k * tn * 4 + k * tn * 2 + tm * tn * 4
    if shift:
        in_specs.append(wspec(LANES, lambda j: col_block(j, LANES) + tn // LANES))
        args.append(w)
        vmem += weight_buffers * k * LANES * 4
    if residual is not None:
        in_specs.append(pl.BlockSpec((tm, tn), lambda j, i: (i, j)))
        args.append(residual)
        vmem += 2 * tm * tn * residual.dtype.itemsize
    out_specs = [pl.BlockSpec((tm, tn), lambda j, i: (i, j))]
    out_shape = [jax.ShapeDtypeStruct((m, n), out_dtype)]
    scratch = [pltpu.VMEM((k, tn), _BF16)]
    alone = None
    if side:
        side, steps = side
        if steps != (n // tn) * (m // tm):
            alone, side = _run_side(side, steps, m // tm, name + "_side"), None
    if side:
        in_specs += side.in_specs
        args += side.args
        out_specs += side.out_specs
        out_shape += side.out_shape
        scratch += side.scratch_shapes
        vmem += side.vmem
    out = pl.pallas_call(
        functools.partial(_mm_w32_kernel, shift=shift, has_residual=residual is not None, side=side,
                          transposed=transposed),
        grid=(n // tn, m // tm),
        in_specs=in_specs,
        out_specs=out_specs,
        out_shape=out_shape,
        scratch_shapes=scratch,
        compiler_params=_params(("arbitrary", "arbitrary"), vmem),
        name=name,
    )(*args)
    if alone is not None:
        return [out[0], *alone]
    return out if side else out[0]


def _log_sigmoid(z):
    return jnp.minimum(z, 0.0) - jnp.log(1.0 + jnp.exp(-jnp.abs(z)))


def _cumsum_rows(x):
    c, n = x.shape
    row = lax.broadcasted_iota(jnp.int32, (c, c), 0)
    col = lax.broadcasted_iota(jnp.int32, (c, c), 1)
    tri = jnp.where(row >= col, 1.0, 0.0).astype(_BF16)
    hi = x.astype(_BF16)
    rest = x - hi.astype(_F32)
    mid = rest.astype(_BF16)
    lo = (rest - mid.astype(_F32)).astype(_BF16)
    parts = jnp.dot(tri, jnp.concatenate([hi, mid, lo], axis=1), preferred_element_type=_F32)
    return parts[:, :n] + parts[:, n:2 * n] + parts[:, 2 * n:]


def _gla_chunk(q, k, v, b, state, midway):
    c = q.shape[0]
    row = lax.broadcasted_iota(jnp.int32, (c, c), 0)
    col = lax.broadcasted_iota(jnp.int32, (c, c), 1)
    b_last = b[c - 1:c, :]

    attn = jnp.zeros((c, c), _F32)
    row1 = lax.broadcasted_iota(jnp.int32, (c, 1), 0)
    s = c // 2
    while s >= GLA_SUB:
        pieces = []
        for lo in range(0, c, 2 * s):
            mid = lo + s
            ref_row = b[mid - 1:mid, :]
            pieces.append(ref_row - b[lo:mid, :])
            pieces.append(b[mid:mid + s, :] - ref_row)
        x = jnp.exp2(jnp.concatenate(pieces, axis=0))
        upper = ((row1 >> _log2(s)) & 1) == 1
        qt = jnp.where(upper, q * x, 0.0).astype(_BF16)
        kt = jnp.where(upper, 0.0, k * x).astype(_BF16)
        same = (row >> _log2(2 * s)) == (col >> _log2(2 * s))
        attn = attn + jnp.where(same, _nt(qt, kt), 0.0)
        s //= 2

    midway()
    sub_row = lax.broadcasted_iota(jnp.int32, (GLA_SUB, 1), 0)
    lane = lax.broadcasted_iota(jnp.int32, (GLA_SUB, c), 1)
    diag = []
    for i0 in range(0, c, GLA_SUB):
        qi, ki, bi = q[i0:i0 + GLA_SUB, :], k[i0:i0 + GLA_SUB, :], b[i0:i0 + GLA_SUB, :]
        blk = jnp.zeros((GLA_SUB, c), _F32)
        for j in range(GLA_SUB):
            dec = jnp.exp2(jnp.where(sub_row >= j, bi - bi[j:j + 1, :], _NEG_INF))
            sc = jnp.sum(qi * dec * ki[j:j + 1, :], axis=1, keepdims=True)
            blk = jnp.where(lane == i0 + j, sc, blk)
        diag.append(blk)
    attn = attn + jnp.concatenate(diag, axis=0)

    qb = (q * jnp.exp2(b)).astype(_BF16)
    o = _nt(qb, state.astype(_BF16)) + jnp.dot(attn.astype(_BF16), v, preferred_element_type=_F32)
    kd = (k * jnp.exp2(b_last - b)).astype(_BF16)
    return o, state * jnp.exp2(b_last) + _tn(v, kd)


def _gla_kernel(step, parts, q_ref, k_ref, v_ref, r_ref, glr_ref, wg_ref, bias_ref, gain_ref, o_ref,
                state_ref, *, nc):
    @pl.when(step % nc == 0)
    def _():
        state_ref[...] = jnp.zeros_like(state_ref)

    c = q_ref.shape[0] // GLA_STEP_CHUNKS
    z = jnp.dot(glr_ref[...], wg_ref[...], preferred_element_type=_F32) + bias_ref[...]
    log_a = _log_sigmoid(z) * (_LOG2_E / GLA_TAU)
    b_all = [_cumsum_rows(log_a[s * c:(s + 1) * c, :]) for s in range(GLA_STEP_CHUNKS)]
    states = [state_ref[h] for h in range(GLA_STEP_HEADS)]
    parts = iter(parts)
    for s in range(GLA_STEP_CHUNKS):
        rows = slice(s * c, (s + 1) * c)
        for h in range(GLA_STEP_HEADS):
            next(parts)()
            ks = slice(h * GLA_DK, (h + 1) * GLA_DK)
            vs = slice(h * GLA_DV, (h + 1) * GLA_DV)
            q = q_ref[rows, ks].astype(_F32) * (GLA_DK ** -0.5)
            o, states[h] = _gla_chunk(q, k_ref[rows, ks].astype(_F32), v_ref[rows, vs], b_all[s][:, ks],
                                      states[h], next(parts))
            ms = jnp.mean(o * o, axis=-1, keepdims=True)
            on = o * lax.rsqrt(ms + RMS_EPS) * gain_ref[:, vs]
            r = r_ref[rows, vs].astype(_F32)
            o_ref[rows, vs] = (on * (r * jax.nn.sigmoid(r))).astype(o_ref.dtype)
    for h in range(GLA_STEP_HEADS):
        state_ref[h] = states[h]


def _run_side(side, steps, ni, name):
    def kern(*refs):
        side.kernel(pl.program_id(0) * pl.num_programs(1) + pl.program_id(1),
                    [lambda: None] * side.parts, *refs)

    return pl.pallas_call(
        kern, grid=(steps // ni, ni), in_specs=side.in_specs, out_specs=side.out_specs,
        out_shape=side.out_shape, scratch_shapes=side.scratch_shapes,
        compiler_params=_params(("arbitrary", "arbitrary"), side.vmem), name=name)(*side.args)


def _gla_side(qkvr, glr, wg, bias, gain, batch, seq, ni, chunk=128):
    t = batch * seq
    chunk = chunk * GLA_STEP_CHUNKS
    nc = seq // chunk
    hs = GLA_STEP_HEADS
    wk, wv = hs * GLA_DK, hs * GLA_DV
    groups = GLA_HEADS // hs
    kv = 2 * groups * wk // wv

    def at(col):
        def index(j, i):
            s = j * ni + i
            return (s // (groups * nc)) * nc + s % nc, col + (s // nc) % groups
        return index

    head = lambda j, i: (0, ((j * ni + i) // nc) % groups)
    in_specs = [
        pl.BlockSpec((chunk, wk), at(0)),
        pl.BlockSpec((chunk, wk), at(groups)),
        pl.BlockSpec((chunk, wv), at(kv)),
        pl.BlockSpec((chunk, wv), at(kv + groups)),
        pl.BlockSpec((chunk, LANES), lambda j, i: (at(0)(j, i)[0], 0)),
        pl.BlockSpec((LANES, wk), head),
        pl.BlockSpec((1, wk), head),
        pl.BlockSpec((1, wv), head),
    ]
    side = _Side(functools.partial(_gla_kernel, nc=nc), 2 * hs * GLA_STEP_CHUNKS,
                 [qkvr, qkvr, qkvr, qkvr, glr, wg, bias, gain], in_specs,
                 [pl.BlockSpec((chunk, wv), at(0))], [jax.ShapeDtypeStruct((t, GLA_HEADS * GLA_DV), _BF16)],
                 [pltpu.VMEM((hs, GLA_DV, GLA_DK), _F32)], vmem=16 << 20)
    return side, batch * groups * nc


def _pair_rmsnorm(x, gain2):
    lane = lax.broadcasted_iota(jnp.int32, x.shape, 1)
    low = lane < SWA_HD
    sq = x * x
    s_all = jnp.sum(sq, axis=1, keepdims=True)
    s_low = jnp.sum(jnp.where(low, sq, 0.0), axis=1, keepdims=True)
    ms = jnp.where(low, s_low, s_all - s_low) * (1.0 / SWA_HD)
    return x * lax.rsqrt(ms + RMS_EPS) * gain2


def _swa_kernel(step, parts, sinks_ref, q_ref, kc_ref, kp_ref, vc_ref, vp_ref, qg_ref, kg_ref, o_ref, *, nb, spp):
    pair_groups = SWA_KV_HEADS // 2 // spp
    blk = (step // pair_groups) % nb
    bs = SWA_BLOCK
    group = SWA_HEADS // SWA_KV_HEADS
    pairs = group // 2
    pw = 2 * group * SWA_HD
    lane = lax.broadcasted_iota(jnp.int32, (bs, LANES), 1)
    low = lane < SWA_HD

    q_loc = lax.broadcasted_iota(jnp.int32, (group * bs, bs), 0) & (bs - 1)
    from_cur = lax.broadcasted_iota(jnp.int32, (group * bs, bs), 1) <= q_loc
    prev_bias = jnp.where(blk > 0, 0.0, _NEG_INF)

    for pp in range(spp):
        p = (step % pair_groups) * spp + pp
        ls = slice(pp * LANES, (pp + 1) * LANES)
        tiles = [_pair_rmsnorm(kc_ref[:, ls].astype(_F32), kg_ref[...]),
                 _pair_rmsnorm(kp_ref[:, ls].astype(_F32), kg_ref[...]),
                 vc_ref[:, ls].astype(_F32), vp_ref[:, ls].astype(_F32)]
        swapped = [pltpu.roll(x, SWA_HD, axis=1) for x in tiles]
        for half in range(2):
            kc, kp, vc, vp = [(jnp.where(low, x, xs) if half == 0 else jnp.where(low, xs, x)).astype(_BF16)
                              for x, xs in zip(tiles, swapped)]
            qs, sink = [], []
            for t in range(pairs):
                c0 = pp * pw + half * group * SWA_HD + t * LANES
                qn = _pair_rmsnorm(q_ref[:, c0:c0 + LANES].astype(_F32), qg_ref[...]) * (SWA_HD ** -0.5)
                qs.append(jnp.where(low, qn, 0.0))
                qs.append(jnp.where(low, 0.0, qn))
                head = (2 * p + half) * group + 2 * t
                sink.append(jnp.full((bs, 1), sinks_ref[head], _F32))
                sink.append(jnp.full((bs, 1), sinks_ref[head + 1], _F32))
            qstack = jnp.concatenate(qs, axis=0).astype(_BF16)
            sink = jnp.concatenate(sink, axis=0)
            s = jnp.where(from_cur, _nt(qstack, kc), _nt(qstack, kp) + prev_bias)
            m = jnp.maximum(jnp.max(s, axis=1, keepdims=True), sink)
            e = jnp.exp(s - m)
            denom = jnp.sum(e, axis=1, keepdims=True) + jnp.exp(sink - m)
            e_cur = jnp.where(from_cur, e, 0.0).astype(_BF16)
            e_prev = jnp.where(from_cur, 0.0, e).astype(_BF16)
            o = (jnp.dot(e_cur, vc, preferred_element_type=_F32)
                 + jnp.dot(e_prev, vp, preferred_element_type=_F32)) / denom
            for t in range(pairs):
                c0 = pp * pw + half * group * SWA_HD + t * LANES
                oa = o[(2 * t) * bs:(2 * t + 1) * bs, :]
                ob = o[(2 * t + 1) * bs:(2 * t + 2) * bs, :]
                o_ref[:, c0:c0 + LANES] = jnp.where(low, oa, ob).astype(o_ref.dtype)
            parts[2 * pp + half]()


def _swa_side(sq, q_col, skv, k_col, sinks, q_gain, k_gain, batch, seq, ni):
    t = batch * seq
    nb = seq // SWA_BLOCK
    npairs = SWA_KV_HEADS // 2
    spp = min(SWA_STEP_PAIRS, npairs)
    pair_groups = npairs // spp
    kvw = SWA_KV_HEADS * SWA_HD
    qw, kw = spp * 2 * (SWA_HEADS // SWA_KV_HEADS) * SWA_HD, spp * LANES
    assert q_col % qw == 0 and k_col % kw == 0 and kvw % kw == 0

    def at(col, prev=False):
        def index(j, i):
            s = j * ni + i
            blk = s // pair_groups
            if prev:
                blk = blk - jnp.where(blk % nb > 0, 1, 0)
            return blk, col + s % pair_groups
        return index

    const = lambda j, i: (0, 0)
    kc, vc = k_col // kw, (k_col + kvw) // kw
    in_specs = [
        pl.BlockSpec(memory_space=pltpu.SMEM),
        pl.BlockSpec((SWA_BLOCK, qw), at(q_col // qw)),
        pl.BlockSpec((SWA_BLOCK, kw), at(kc)),
        pl.BlockSpec((SWA_BLOCK, kw), at(kc, prev=True)),
        pl.BlockSpec((SWA_BLOCK, kw), at(vc)),
        pl.BlockSpec((SWA_BLOCK, kw), at(vc, prev=True)),
        pl.BlockSpec((1, LANES), const),
        pl.BlockSpec((1, LANES), const),
    ]
    g2 = lambda g: jnp.concatenate([g, g]).reshape(1, LANES).astype(_F32)
    side = _Side(functools.partial(_swa_kernel, nb=nb, spp=spp), 2 * spp,
                 [sinks.astype(_F32), sq, skv, skv, skv, skv, g2(q_gain), g2(k_gain)], in_specs,
                 [pl.BlockSpec((SWA_BLOCK, qw), at(0))],
                 [jax.ShapeDtypeStruct((t, SWA_HEADS * SWA_HD), _BF16)], vmem=16 << 20)
    return side, batch * nb * pair_groups


def _memattn_kernel(q_ref, k_ref, v_ref, g_ref, o_ref):
    hd = q_ref.shape[1]
    q = q_ref[...].astype(_F32)
    ms = jnp.mean(q * q, axis=-1, keepdims=True)
    qn = (q * lax.rsqrt(ms + RMS_EPS) * g_ref[...] * (hd ** -0.5)).astype(_BF16)
    s = _nt(qn, k_ref[...])
    m = jnp.max(s, axis=-1, keepdims=True)
    e = jnp.exp(s - m)
    denom = jnp.sum(e, axis=-1, keepdims=True)
    o = jnp.dot(e.astype(_BF16), v_ref[...], preferred_element_type=_F32) / denom
    o_ref[...] = o.astype(o_ref.dtype)


def _memattn(mq, kc, vm, q_gain, batch, seq, mem_len, d, q_blk=0, v_blk=0, tm=1024):
    t = mq.shape[0]
    tm = _tile(seq, tm)
    ns = seq // tm
    hd = d // MEM_HEADS
    return pl.pallas_call(
        _memattn_kernel,
        grid=(batch, ns, MEM_HEADS),
        in_specs=[
            pl.BlockSpec((tm, hd), lambda b, i, h: (b * ns + i, q_blk + h)),
            pl.BlockSpec((mem_len, hd), lambda b, i, h: (b, h)),
            pl.BlockSpec((mem_len, hd), lambda b, i, h: (b, v_blk + h)),
            pl.BlockSpec((1, hd), lambda b, i, h: (0, 0)),
        ],
        out_specs=pl.BlockSpec((tm, hd), lambda b, i, h: (b * ns + i, h)),
        out_shape=jax.ShapeDtypeStruct((t, d), _BF16),
        compiler_params=_params(("parallel", "parallel", "arbitrary"),
                                4 * tm * hd * 2 + 4 * mem_len * hd * 2 + 6 * tm * hd * 4),
        name="memattn",
    )(mq, kc, vm, q_gain.reshape(1, hd).astype(_F32))


def _merge_kernel(a_ref, b_ref, c_ref, wa_ref, wb_ref, wc_ref, ga_ref, gb_ref, gc_ref, o_ref):
    def branch(x_ref, w_ref, g_ref):
        y = jnp.dot(x_ref[...], w_ref[...], preferred_element_type=_F32)
        return jax.nn.sigmoid(g_ref[...].astype(_F32)) * y

    o_ref[...] = (branch(a_ref, wa_ref, ga_ref) + branch(b_ref, wb_ref, gb_ref)
                  + branch(c_ref, wc_ref, gc_ref)).astype(o_ref.dtype)


def _merge(oa, ob, oc, wa, wb, wc, gate, gate_col=0, tm=512, tn=512):
    t, d = oa.shape
    n = wa.shape[1]
    tm, tn = _tile(t, tm), _tile(n, tn)
    nn = n // tn
    if gate_col % tn:
        gate, gate_col = gate[:, gate_col:gate_col + 3 * n], 0
    g0 = gate_col // tn
    act = lambda o: pl.BlockSpec((tm, o.shape[1]), lambda j, i: (i, 0))
    wsp = lambda w: pl.BlockSpec((w.shape[0], tn), lambda j, i: (0, j), pipeline_mode=pl.Buffered(1))
    gsp = lambda b: pl.BlockSpec((tm, tn), lambda j, i: (i, g0 + b * nn + j))
    vmem = 3 * (2 * tm * d * 2 + d * tn * 2 + 2 * tm * tn * 2) + 8 * tm * tn * 4
    return pl.pallas_call(
        _merge_kernel,
        grid=(nn, t // tm),
        in_specs=[act(oa), act(ob), act(oc), wsp(wa), wsp(wb), wsp(wc), gsp(0), gsp(1), gsp(2)],
        out_specs=pl.BlockSpec((tm, tn), lambda j, i: (i, j)),
        out_shape=jax.ShapeDtypeStruct((t, n), _BF16),
        compiler_params=_params(("parallel", "arbitrary"), vmem),
        name="merge",
    )(oa, ob, oc, wa, wb, wc, gate, gate, gate)


def _extract_topk(s, payload, kk):
    n = s.shape[0]
    pos = lax.broadcasted_iota(jnp.int32, s.shape, 0).astype(_F32)
    vals, sel = [], []
    for _ in range(kk):
        m = jnp.max(s, axis=0, keepdims=True)
        first = jnp.min(jnp.where(s == m, pos, float(n)), axis=0, keepdims=True)
        hit = pos == first
        vals.append(m)
        sel.append(first if payload is None else jnp.max(jnp.where(hit, payload, -1.0), axis=0, keepdims=True))
        s = jnp.where(hit, _NEG_INF, s)
    return jnp.concatenate(vals, axis=0), jnp.concatenate(sel, axis=0)


def _route_kernel(h_ref, gain_ref, wq_ref, keys_ref, hn_ref, w_ref, q_ref, i1_ref, i2_ref, g_ref):
    kk = PEER_TOPK
    half = PEER_DQ // 2
    nk = PEER_NKEYS
    step = pl.program_id(0)
    cur, prev = step % 2, (step + 1) % 2
    tokens = h_ref.shape[0] // PEER_HEADS

    @pl.when(step == 0)
    def _():
        q_ref[1] = jnp.zeros(q_ref.shape[1:], q_ref.dtype)
        i1_ref[...] = jnp.zeros_like(i1_ref)
        i2_ref[...] = jnp.zeros_like(i2_ref)
        g_ref[...] = jnp.zeros_like(g_ref)

    x = h_ref[...]
    ms = jnp.mean(x * x, axis=-1, keepdims=True)
    hn = (x * lax.rsqrt(ms + RMS_EPS) * gain_ref[...]).astype(_BF16)
    hn_ref[...] = hn

    sub = lax.broadcasted_iota(jnp.int32, (nk, i1_ref.shape[2]), 0)

    def gate_rows(t0):
        for t in range(t0, t0 + tokens):
            a_t = jnp.where(sub == i1_ref[cur, t:t + 1, :], 1.0, 0.0).astype(_BF16)
            b_t = jnp.where(sub == i2_ref[cur, t:t + 1, :], g_ref[cur, t:t + 1, :], 0.0).astype(_BF16)
            w_ref[t] = _nt(a_t, b_t)

    experts, gates = [], []
    for h in range(PEER_HEADS):
        hs = slice(h * PEER_DQ, (h + 1) * PEER_DQ)
        q_ref[cur, :, hs] = jnp.dot(hn, wq_ref[:, hs], preferred_element_type=_F32)
        top = []
        for p in range(2):
            c0 = (2 * h + p) * half
            st = _nt(keys_ref[2 * h + p].astype(_BF16), q_ref[prev, :, c0:c0 + half].astype(_BF16))
            top.append(_extract_topk(st, None, kk))
        (s0, i0), (s1, i1) = top
        cand, ids = [], []
        tail = kk // 2
        for a in range(tail):
            nb = kk // (a + 1)
            rows = -(-nb // 8) * 8
            c = s0[a:a + 1, :] + s1[0:rows, :]
            if rows != nb:
                c = jnp.where(lax.broadcasted_iota(jnp.int32, c.shape, 0) < nb, c, _NEG_INF)
            cand.append(c)
            ids.append(i0[a:a + 1, :] * PEER_NKEYS + i1[0:rows, :])
        cand.append(s0[tail:kk, :] + s1[0:1, :])
        ids.append(i0[tail:kk, :] * PEER_NKEYS + i1[0:1, :])
        best, e = _extract_topk(jnp.concatenate(cand, axis=0), jnp.concatenate(ids, axis=0), kk)
        ex = jnp.exp(best - best[0:1, :])
        gates.append(ex / jnp.sum(ex, axis=0, keepdims=True))
        experts.append(e)
        gate_rows(h * tokens)
    e = jnp.concatenate(experts, axis=0).T.astype(jnp.int32)
    i1_ref[prev] = e >> _log2(PEER_NKEYS)
    i2_ref[prev] = e & (PEER_NKEYS - 1)
    g_ref[prev] = jnp.concatenate(gates, axis=0).T


def _route(h, gain, wq, sub_keys, tt=128):
    t, d = h.shape
    tt = _tile(t, tt)
    n = t // tt
    j = PEER_HEADS * PEER_TOPK
    nk = PEER_NKEYS
    nq = PEER_HEADS * PEER_DQ
    keys = sub_keys.reshape(PEER_HEADS * 2, nk, PEER_DQ // 2)
    tile = lambda s: (jnp.minimum(s, n - 1), 0)
    vmem = d * nq * 2 + 2 * (tt * nk * nk * 4 + tt * d * 6) + 2 * tt * nq * 4 + (12 << 20)
    return pl.pallas_call(
        _route_kernel,
        grid=(n + 2,),
        in_specs=[pl.BlockSpec((tt, d), tile),
                  pl.BlockSpec((1, d), lambda s: (0, 0)),
                  pl.BlockSpec((d, nq), lambda s: (0, 0), pipeline_mode=pl.Buffered(1)),
                  pl.BlockSpec(keys.shape, lambda s: (0, 0, 0))],
        out_specs=[pl.BlockSpec((tt, d), tile),
                   pl.BlockSpec((tt, nk, nk), lambda s: (jnp.maximum(s - 2, 0), 0, 0))],
        out_shape=[jax.ShapeDtypeStruct((t, d), _BF16), jax.ShapeDtypeStruct((t, nk, nk), _F32)],
        scratch_shapes=[pltpu.VMEM((2, tt, nq), _F32), pltpu.VMEM((2, tt, j), jnp.int32),
                        pltpu.VMEM((2, tt, j), jnp.int32), pltpu.VMEM((2, tt, j), _F32)],
        compiler_params=_params(("arbitrary",), vmem),
        name="peer_route",
    )(h, gain.reshape(1, d).astype(_F32), wq, keys)


def _gelu(x):
    return 0.5 * x * (1.0 + lax.erf(x * (2.0 ** -0.5)))


def _peer_act_kernel(x_ref, u_ref, w_ref, s_ref):
    act = _gelu(_nt(x_ref[...], u_ref[...]))
    w = jnp.transpose(w_ref[...], (1, 0, 2))
    for a in range(w.shape[0]):
        cs = slice(a * PEER_NKEYS, (a + 1) * PEER_NKEYS)
        s_ref[:, cs] = (act[:, cs] * w[a]).astype(s_ref.dtype)


def _peer_act(xn, u, w, tm=1024, rows_a=8):
    t, d = xn.shape
    ne = u.shape[0]
    nk = PEER_NKEYS
    tm = _tile(t, tm)
    eb = rows_a * nk
    vmem = 2 * (tm * d * 2 + tm * rows_a * nk * 4 + tm * eb * 2) + eb * d * 2 + 3 * tm * eb * 4
    return pl.pallas_call(
        _peer_act_kernel,
        grid=(ne // eb, t // tm),
        in_specs=[pl.BlockSpec((tm, d), lambda j, i: (i, 0)),
                  pl.BlockSpec((eb, d), lambda j, i: (j, 0), pipeline_mode=pl.Buffered(1)),
                  pl.BlockSpec((tm, rows_a, nk), lambda j, i: (i, j, 0))],
        out_specs=pl.BlockSpec((tm, eb), lambda j, i: (i, j)),
        out_shape=jax.ShapeDtypeStruct((t, ne), _BF16),
        compiler_params=_params(("parallel", "parallel"), vmem),
        name="peer_act",
    )(xn, u, w)


def _mm_acc_kernel(x_ref, w_ref, r_ref, o_ref):
    acc = jnp.dot(x_ref[...], w_ref[...], preferred_element_type=_F32)

    @pl.when(pl.program_id(2) == 0)
    def _():
        o_ref[...] = r_ref[...] + acc

    @pl.when(pl.program_id(2) > 0)
    def _():
        o_ref[...] += acc


def _matmul_acc(x, w, residual, tm=1024, tn=1024, tk=4096):
    m, k = x.shape
    n = w.shape[1]
    tm, tn, tk = _tile(m, tm), _tile(n, tn), _tile(k, tk)
    vmem = 2 * (tm * tk * 2 + tk * tn * 2 + 2 * tm * tn * 4) + tm * tn * 4
    return pl.pallas_call(
        _mm_acc_kernel,
        grid=(m // tm, n // tn, k // tk),
        in_specs=[pl.BlockSpec((tm, tk), lambda i, j, l: (i, l)),
                  pl.BlockSpec((tk, tn), lambda i, j, l: (l, j)),
                  pl.BlockSpec((tm, tn), lambda i, j, l: (i, j))],
        out_specs=pl.BlockSpec((tm, tn), lambda i, j, l: (i, j)),
        out_shape=jax.ShapeDtypeStruct((m, n), _F32),
        compiler_params=_params(("parallel", "parallel", "arbitrary"), vmem),
        name="peer_out",
    )(x, w, residual)


def _layer(h, mem, norm_mix, w_in, gla_w_gate_up, gla_gate_bias, gla_out_norm, swa_q_norm, swa_k_norm,
           swa_sinks, mem_norm, w_mem_kv, mem_q_norm, mem_k_norm, w_branch_gla, w_branch_swa,
           w_branch_mem, w_out, norm_ffn, peer_w_q, peer_sub_keys, peer_u, peer_v, batch, seq, l):
    t, d = h.shape
    mem_len = mem.shape[0] // batch
    qk = GLA_HEADS * GLA_DK
    gv = GLA_HEADS * GLA_DV
    swq = SWA_HEADS * SWA_HD
    skv = SWA_KV_HEADS * SWA_HD
    o_glr = 2 * qk + 2 * gv
    o_sq = o_glr + GLA_RANK
    o_sk = o_sq + swq
    o_mq = o_sk + 2 * skv
    o_gate = o_mq + d
    bf = lambda w: w.astype(_BF16)
    hd = d // MEM_HEADS

    xn = _rmsnorm(h, norm_mix, _BF16)
    proj = functools.partial(_matmul_w32, xn, jnp.swapaxes(w_in, 1, 2), l, transposed=True)
    qkvr = proj([(0, o_glr)], _BF16, name="proj_gla")
    glr = proj([(o_glr, LANES)], _BF16, name="proj_glr")
    skvp = proj([(o_sk, 2 * skv)], _BF16, name="proj_swa_kv")

    tm_gla = _tile(t, 512 * GLA_STEP_CHUNKS)
    tm_swa = _tile(t, 512 * min(SWA_STEP_PAIRS, SWA_KV_HEADS // 2))
    wg = bf(jnp.pad(gla_w_gate_up, ((0, LANES - GLA_RANK), (0, 0))))
    gla = _gla_side(qkvr, glr, wg, gla_gate_bias.reshape(1, qk).astype(_F32),
                    gla_out_norm.reshape(1, gv).astype(_F32), batch, seq, t // tm_gla)
    sqmq, o_a = proj([(o_sq, swq), (o_mq, d)], _BF16, tm=tm_gla, side=gla, name="proj_q_gla")
    swa = _swa_side(sqmq, 0, skvp, 0, swa_sinks, swa_q_norm, swa_k_norm, batch, seq, t // tm_swa)
    gate, o_b = proj([(o_gate, 3 * d)], _BF16, tm=tm_swa, tn=768, side=swa, weight_buffers=1,
                     name="proj_gate_swa")

    memn = _rmsnorm(mem, mem_norm, _BF16)
    kv_m = _matmul_w32(memn, w_mem_kv, l, [(0, 2 * d)], _BF16, name="mem_kv")
    k_c = _rmsnorm(kv_m[:, :d].reshape(-1, hd), mem_k_norm, _BF16).reshape(-1, d)
    mq, mq_blk = (sqmq, swq // hd) if swq % hd == 0 else (sqmq[:, swq:], 0)
    o_c = _memattn(mq, k_c, kv_m, mem_q_norm, batch, seq, mem_len, d, mq_blk, MEM_HEADS)

    mix = _merge(o_a, o_b, o_c, bf(w_branch_gla), bf(w_branch_swa), bf(w_branch_mem), gate)
    h = _matmul_w32(mix, w_out, l, [(0, d)], _F32, residual=h, name="out_proj")

    hn, w = _route(h, norm_ffn, bf(peer_w_q[l]), peer_sub_keys.astype(_F32))
    s = _peer_act(hn, bf(peer_u), w)
    return _matmul_acc(s, bf(peer_v), h)


def kernel(x, mem, norm_mix, w_in, gla_w_gate_up, gla_gate_bias, gla_out_norm, swa_q_norm, swa_k_norm,
           swa_sinks, mem_norm, w_mem_kv, mem_q_norm, mem_k_norm, w_branch_gla, w_branch_swa, w_branch_mem,
           w_out, norm_ffn, peer_w_q, peer_sub_keys, peer_u, peer_v):
    batch, seq, d = x.shape
    h = x.reshape(batch * seq, d)
    memf = mem.reshape(-1, d)
    for l in range(norm_mix.shape[0]):
        h = _layer(h, memf, norm_mix[l], w_in, gla_w_gate_up[l], gla_gate_bias[l], gla_out_norm[l],
                   swa_q_norm[l], swa_k_norm[l], swa_sinks[l], mem_norm[l], w_mem_kv, mem_q_norm[l],
                   mem_k_norm[l], w_branch_gla[l], w_branch_swa[l], w_branch_mem[l], w_out, norm_ffn[l],
                   peer_w_q, peer_sub_keys[l], peer_u[l], peer_v[l], batch, seq, l)
    return h.reshape(batch, seq, d)
```

```python
import functools

import jax
import jax.numpy as jnp
from jax import lax
from jax.experimental import pallas as pl
from jax.experimental.pallas import tpu as pltpu

_F32 = jnp.float32
_BF16 = jnp.bfloat16
_NEG_INF = float("-inf")

RMS_EPS = 1e-6
GLA_HEADS = 8
GLA_DK = 256
GLA_DV = 512
GLA_RANK = 16
GLA_TAU = 16.0
GLA_SUB = 8
GLA_STEP_HEADS = 2
GLA_STEP_CHUNKS = 2
_LOG2_E = 1.4426950408889634
SWA_HEADS = 64
SWA_KV_HEADS = 8
SWA_HD = 64
SWA_BLOCK = 128
SWA_STEP_PAIRS = 2
MEM_HEADS = 4
PEER_HEADS = 8
PEER_NKEYS = 128
PEER_DQ = 256
PEER_TOPK = 16

LANES = 128
V7X_VMEM_BYTES = 64 * 1024 * 1024


def _tile(n, pref):
    t = min(n, pref)
    while n % t:
        t -= 1
    return t


def _params(semantics, vmem_bytes):
    limit = min(int(vmem_bytes) + (8 << 20), V7X_VMEM_BYTES - (4 << 20))
    return pltpu.CompilerParams(dimension_semantics=semantics, vmem_limit_bytes=limit)


def _nt(a, b):
    return lax.dot_general(a, b, (((1,), (1,)), ((), ())), preferred_element_type=_F32)


def _log2(n):
    assert n > 0 and n & (n - 1) == 0, n
    return n.bit_length() - 1


def _tn(a, b):
    return lax.dot_general(a, b, (((0,), (0,)), ((), ())), preferred_element_type=_F32)


def _rmsnorm_kernel(x_ref, g_ref, o_ref):
    x = x_ref[...].astype(_F32)
    ms = jnp.mean(x * x, axis=-1, keepdims=True)
    o_ref[...] = (x * lax.rsqrt(ms + RMS_EPS) * g_ref[...]).astype(o_ref.dtype)


def _rmsnorm(x, gain, out_dtype, rows=256):
    r, d = x.shape
    tr = _tile(r, rows)
    blk = tr * d * (x.dtype.itemsize + jnp.dtype(out_dtype).itemsize)
    return pl.pallas_call(
        _rmsnorm_kernel,
        grid=(r // tr,),
        in_specs=[pl.BlockSpec((tr, d), lambda i: (i, 0)), pl.BlockSpec((1, d), lambda i: (0, 0))],
        out_specs=pl.BlockSpec((tr, d), lambda i: (i, 0)),
        out_shape=jax.ShapeDtypeStruct((r, d), out_dtype),
        compiler_params=_params(("parallel",), 2 * blk + 4 * tr * d * 4),
        name="rmsnorm",
    )(x, gain.reshape(1, d).astype(_F32))


class _Side:
    def __init__(self, kernel, parts, args, in_specs, out_specs, out_shape, scratch_shapes=(), vmem=0):
        self.kernel, self.parts, self.args, self.in_specs = kernel, parts, list(args), list(in_specs)
        self.out_specs, self.out_shape = list(out_specs), list(out_shape)
        self.scratch_shapes, self.vmem = list(scratch_shapes), vmem


def _mm_w32_kernel(*refs, shift, has_residual, side, transposed):
    n_in = 2 + bool(shift) + has_residual
    n_side_in = len(side.in_specs) if side else 0
    n_side_out = len(side.out_specs) if side else 0
    x_ref, w_ref = refs[0], refs[1]
    o_ref = refs[n_in + n_side_in]
    wbf_ref = refs[n_in + n_side_in + 1 + n_side_out]
    tn = o_ref.shape[1]

    @pl.when(pl.program_id(1) == 0)
    def _():
        rows = _tile(w_ref.shape[0], 128 if transposed else 512)
        for r0 in range(0, w_ref.shape[0], rows):
            rs = slice(r0, r0 + rows)
            if transposed:
                if r0 + rows + shift <= tn:
                    w = w_ref[r0 + shift:r0 + rows + shift, :]
                else:
                    w = jnp.concatenate([w_ref[r0 + shift:, :], refs[2][:shift, :]], axis=0)
                wbf_ref[:, rs] = w.T.astype(_BF16)
            else:
                w = w_ref[rs, :]
                if shift:
                    w = jnp.concatenate([w, refs[2][rs, :]], axis=1)
                    w = pltpu.roll(w, w.shape[1] - shift, axis=1)[:, :tn]
                wbf_ref[rs, :] = w.astype(_BF16)

    def matmul_rows(rs):
        acc = jnp.dot(x_ref[rs, :], wbf_ref[...], preferred_element_type=_F32)
        if has_residual:
            acc = acc + refs[n_in - 1][rs, :].astype(_F32)
        o_ref[rs, :] = acc.astype(o_ref.dtype)

    if side:
        rows = x_ref.shape[0] // side.parts
        parts = [functools.partial(matmul_rows, slice(r * rows, (r + 1) * rows)) for r in range(side.parts)]
        step = pl.program_id(0) * pl.num_programs(1) + pl.program_id(1)
        side.kernel(step, parts, *refs[n_in:n_in + n_side_in],
                    *refs[n_in + n_side_in + 1:n_in + n_side_in + 1 + n_side_out],
                    *refs[n_in + n_side_in + 2 + n_side_out:])
    else:
        matmul_rows(slice(None))


def _matmul_w32(x, w, layer, windows, out_dtype, tm=1024, tn=512, residual=None, side=None,
                transposed=False, weight_buffers=2, name="matmul_w32"):
    m, k = x.shape
    shift = windows[0][0] % LANES
    assert all(c0 % LANES == shift for c0, _ in windows)
    tm = _tile(m, tm)
    ok = lambda t: all(n % t == 0 and (c0 - shift) % t == 0 for c0, n in windows)
    tn = max(t for t in range(LANES, tn + 1, LANES) if ok(t))
    n = sum(n for _, n in windows)

    def col_block(j, width):
        blk, start = 0, 0
        for c0, wn in windows:
            first = ((c0 - shift) // tn - start) * (tn // width)
            blk = jnp.where(j >= start, first + j * (tn // width), blk)
            start += wn // tn
        return blk

    osz = jnp.dtype(out_dtype).itemsize

    def wspec(width, block_of):
        mode = {} if weight_buffers == 2 else dict(pipeline_mode=pl.Buffered(weight_buffers))
        if transposed:
            return pl.BlockSpec((None, width, k), lambda j, i: (layer, block_of(j), 0), **mode)
        return pl.BlockSpec((None, k, width), lambda j, i: (layer, 0, block_of(j)), **mode)

    in_specs = [pl.BlockSpec((tm, k), lambda j, i: (i, 0)), wspec(tn, lambda j: col_block(j, tn))]
    args = [x, w]
    vmem = 2 * (tm * k * 2 + tm * tn * osz) + weight_buffers * k * tn * 4 + k * tn * 2 + tm * tn * 4
    if shift:
        in_specs.append(wspec(LANES, lambda j: col_block(j, LANES) + tn // LANES))
        args.append(w)
        vmem += weight_buffers * k * LANES * 4
    if residual is not None:
        in_specs.append(pl.BlockSpec((tm, tn), lambda j, i: (i, j)))
        args.append(residual)
        vmem += 2 * tm * tn * residual.dtype.itemsize
    out_specs = [pl.BlockSpec((tm, tn), lambda j, i: (i, j))]
    out_shape = [jax.ShapeDtypeStruct((m, n), out_dtype)]
    scratch = [pltpu.VMEM((k, tn), _BF16)]
    alone = None
    if side:
        side, steps = side
        if steps != (n // tn) * (m // tm):
            alone, side = _run_side(side, steps, m // tm, name + "_side"), None
    if side:
        in_specs += side.in_specs
        args += side.args
        out_specs += side.out_specs
        out_shape += side.out_shape
        scratch += side.scratch_shapes
        vmem += side.vmem
    out = pl.pallas_call(
        functools.partial(_mm_w32_kernel, shift=shift, has_residual=residual is not None, side=side,
                          transposed=transposed),
        grid=(n // tn, m // tm),
        in_specs=in_specs,
        out_specs=out_specs,
        out_shape=out_shape,
        scratch_shapes=scratch,
        compiler_params=_params(("arbitrary", "arbitrary"), vmem),
        name=name,
    )(*args)
    if alone is not None:
        return [out[0], *alone]
    return out if side else out[0]


def _log_sigmoid(z):
    return jnp.minimum(z, 0.0) - jnp.log(1.0 + jnp.exp(-jnp.abs(z)))


def _cumsum_rows(x):
    c, n = x.shape
    row = lax.broadcasted_iota(jnp.int32, (c, c), 0)
    col = lax.broadcasted_iota(jnp.int32, (c, c), 1)
    tri = jnp.where(row >= col, 1.0, 0.0).astype(_BF16)
    hi = x.astype(_BF16)
    rest = x - hi.astype(_F32)
    mid = rest.astype(_BF16)
    lo = (rest - mid.astype(_F32)).astype(_BF16)
    parts = jnp.dot(tri, jnp.concatenate([hi, mid, lo], axis=1), preferred_element_type=_F32)
    return parts[:, :n] + parts[:, n:2 * n] + parts[:, 2 * n:]


def _gla_chunk(q, k, v, b, state, midway):
    c = q.shape[0]
    row = lax.broadcasted_iota(jnp.int32, (c, c), 0)
    col = lax.broadcasted_iota(jnp.int32, (c, c), 1)
    b_last = b[c - 1:c, :]

    attn = jnp.zeros((c, c), _F32)
    row1 = lax.broadcasted_iota(jnp.int32, (c, 1), 0)
    s = c // 2
    while s >= GLA_SUB:
        pieces = []
        for lo in range(0, c, 2 * s):
            mid = lo + s
            ref_row = b[mid - 1:mid, :]
            pieces.append(ref_row - b[lo:mid, :])
            pieces.append(b[mid:mid + s, :] - ref_row)
        x = jnp.exp2(jnp.concatenate(pieces, axis=0))
        upper = ((row1 >> _log2(s)) & 1) == 1
        qt = jnp.where(upper, q * x, 0.0).astype(_BF16)
        kt = jnp.where(upper, 0.0, k * x).astype(_BF16)
        same = (row >> _log2(2 * s)) == (col >> _log2(2 * s))
        attn = attn + jnp.where(same, _nt(qt, kt), 0.0)
        s //= 2

    midway()
    sub_row = lax.broadcasted_iota(jnp.int32, (GLA_SUB, 1), 0)
    lane = lax.broadcasted_iota(jnp.int32, (GLA_SUB, c), 1)
    diag = []
    for i0 in range(0, c, GLA_SUB):
        qi, ki, bi = q[i0:i0 + GLA_SUB, :], k[i0:i0 + GLA_SUB, :], b[i0:i0 + GLA_SUB, :]
        blk = jnp.zeros((GLA_SUB, c), _F32)
        for j in range(GLA_SUB):
            dec = jnp.exp2(jnp.where(sub_row >= j, bi - bi[j:j + 1, :], _NEG_INF))
            sc = jnp.sum(qi * dec * ki[j:j + 1, :], axis=1, keepdims=True)
            blk = jnp.where(lane == i0 + j, sc, blk)
        diag.append(blk)
    attn = attn + jnp.concatenate(diag, axis=0)

    qb = (q * jnp.exp2(b)).astype(_BF16)
    o = _nt(qb, state.astype(_BF16)) + jnp.dot(attn.astype(_BF16), v, preferred_element_type=_F32)
    kd = (k * jnp.exp2(b_last - b)).astype(_BF16)
    return o, state * jnp.exp2(b_last) + _tn(v, kd)


def _gla_kernel(step, parts, q_ref, k_ref, v_ref, r_ref, glr_ref, wg_ref, bias_ref, gain_ref, o_ref,
                state_ref, *, nc):
    @pl.when(step % nc == 0)
    def _():
        state_ref[...] = jnp.zeros_like(state_ref)

    c = q_ref.shape[0] // GLA_STEP_CHUNKS
    z = jnp.dot(glr_ref[...], wg_ref[...], preferred_element_type=_F32) + bias_ref[...]
    log_a = _log_sigmoid(z) * (_LOG2_E / GLA_TAU)
    b_all = [_cumsum_rows(log_a[s * c:(s + 1) * c, :]) for s in range(GLA_STEP_CHUNKS)]
    states = [state_ref[h] for h in range(GLA_STEP_HEADS)]
    parts = iter(parts)
    for s in range(GLA_STEP_CHUNKS):
        rows = slice(s * c, (s + 1) * c)
        for h in range(GLA_STEP_HEADS):
            next(parts)()
            ks = slice(h * GLA_DK, (h + 1) * GLA_DK)
            vs = slice(h * GLA_DV, (h + 1) * GLA_DV)
            q = q_ref[rows, ks].astype(_F32) * (GLA_DK ** -0.5)
            o, states[h] = _gla_chunk(q, k_ref[rows, ks].astype(_F32), v_ref[rows, vs], b_all[s][:, ks],
                                      states[h], next(parts))
            ms = jnp.mean(o * o, axis=-1, keepdims=True)
            on = o * lax.rsqrt(ms + RMS_EPS) * gain_ref[:, vs]
            r = r_ref[rows, vs].astype(_F32)
            o_ref[rows, vs] = (on * (r * jax.nn.sigmoid(r))).astype(o_ref.dtype)
    for h in range(GLA_STEP_HEADS):
        state_ref[h] = states[h]


def _run_side(side, steps, ni, name):
    def kern(*refs):
        side.kernel(pl.program_id(0) * pl.num_programs(1) + pl.program_id(1),
                    [lambda: None] * side.parts, *refs)

    return pl.pallas_call(
        kern, grid=(steps // ni, ni), in_specs=side.in_specs, out_specs=side.out_specs,
        out_shape=side.out_shape, scratch_shapes=side.scratch_shapes,
        compiler_params=_params(("arbitrary", "arbitrary"), side.vmem), name=name)(*side.args)


def _gla_side(qkvr, glr, wg, bias, gain, batch, seq, ni, chunk=128):
    t = batch * seq
    chunk = chunk * GLA_STEP_CHUNKS
    nc = seq // chunk
    hs = GLA_STEP_HEADS
    wk, wv = hs * GLA_DK, hs * GLA_DV
    groups = GLA_HEADS // hs
    kv = 2 * groups * wk // wv

    def at(col):
        def index(j, i):
            s = j * ni + i
            return (s // (groups * nc)) * nc + s % nc, col + (s // nc) % groups
        return index

    head = lambda j, i: (0, ((j * ni + i) // nc) % groups)
    in_specs = [
        pl.BlockSpec((chunk, wk), at(0)),
        pl.BlockSpec((chunk, wk), at(groups)),
        pl.BlockSpec((chunk, wv), at(kv)),
        pl.BlockSpec((chunk, wv), at(kv + groups)),
        pl.BlockSpec((chunk, LANES), lambda j, i: (at(0)(j, i)[0], 0)),
        pl.BlockSpec((LANES, wk), head),
        pl.BlockSpec((1, wk), head),
        pl.BlockSpec((1, wv), head),
    ]
    side = _Side(functools.partial(_gla_kernel, nc=nc), 2 * hs * GLA_STEP_CHUNKS,
                 [qkvr, qkvr, qkvr, qkvr, glr, wg, bias, gain], in_specs,
                 [pl.BlockSpec((chunk, wv), at(0))], [jax.ShapeDtypeStruct((t, GLA_HEADS * GLA_DV), _BF16)],
                 [pltpu.VMEM((hs, GLA_DV, GLA_DK), _F32)], vmem=16 << 20)
    return side, batch * groups * nc


def _pair_rmsnorm(x, gain2):
    lane = lax.broadcasted_iota(jnp.int32, x.shape, 1)
    low = lane < SWA_HD
    sq = x * x
    s_all = jnp.sum(sq, axis=1, keepdims=True)
    s_low = jnp.sum(jnp.where(low, sq, 0.0), axis=1, keepdims=True)
    ms = jnp.where(low, s_low, s_all - s_low) * (1.0 / SWA_HD)
    return x * lax.rsqrt(ms + RMS_EPS) * gain2


def _swa_kernel(step, parts, sinks_ref, q_ref, kc_ref, kp_ref, vc_ref, vp_ref, qg_ref, kg_ref, o_ref, *, nb, spp):
    pair_groups = SWA_KV_HEADS // 2 // spp
    blk = (step // pair_groups) % nb
    bs = SWA_BLOCK
    group = SWA_HEADS // SWA_KV_HEADS
    pairs = group // 2
    pw = 2 * group * SWA_HD
    lane = lax.broadcasted_iota(jnp.int32, (bs, LANES), 1)
    low = lane < SWA_HD

    q_loc = lax.broadcasted_iota(jnp.int32, (group * bs, bs), 0) & (bs - 1)
    from_cur = lax.broadcasted_iota(jnp.int32, (group * bs, bs), 1) <= q_loc
    prev_bias = jnp.where(blk > 0, 0.0, _NEG_INF)

    for pp in range(spp):
        p = (step % pair_groups) * spp + pp
        ls = slice(pp * LANES, (pp + 1) * LANES)
        tiles = [_pair_rmsnorm(kc_ref[:, ls].astype(_F32), kg_ref[...]),
                 _pair_rmsnorm(kp_ref[:, ls].astype(_F32), kg_ref[...]),
                 vc_ref[:, ls].astype(_F32), vp_ref[:, ls].astype(_F32)]
        swapped = [pltpu.roll(x, SWA_HD, axis=1) for x in tiles]
        for half in range(2):
            kc, kp, vc, vp = [(jnp.where(low, x, xs) if half == 0 else jnp.where(low, xs, x)).astype(_BF16)
                              for x, xs in zip(tiles, swapped)]
            qs, sink = [], []
            for t in range(pairs):
                c0 = pp * pw + half * group * SWA_HD + t * LANES
                qn = _pair_rmsnorm(q_ref[:, c0:c0 + LANES].astype(_F32), qg_ref[...]) * (SWA_HD ** -0.5)
                qs.append(jnp.where(low, qn, 0.0))
                qs.append(jnp.where(low, 0.0, qn))
                head = (2 * p + half) * group + 2 * t
                sink.append(jnp.full((bs, 1), sinks_ref[head], _F32))
                sink.append(jnp.full((bs, 1), sinks_ref[head + 1], _F32))
            qstack = jnp.concatenate(qs, axis=0).astype(_BF16)
            sink = jnp.concatenate(sink, axis=0)
            s = jnp.where(from_cur, _nt(qstack, kc), _nt(qstack, kp) + prev_bias)
            m = jnp.maximum(jnp.max(s, axis=1, keepdims=True), sink)
            e = jnp.exp(s - m)
            denom = jnp.sum(e, axis=1, keepdims=True) + jnp.exp(sink - m)
            e_cur = jnp.where(from_cur, e, 0.0).astype(_BF16)
            e_prev = jnp.where(from_cur, 0.0, e).astype(_BF16)
            o = (jnp.dot(e_cur, vc, preferred_element_type=_F32)
                 + jnp.dot(e_prev, vp, preferred_element_type=_F32)) / denom
            for t in range(pairs):
                c0 = pp * pw + half * group * SWA_HD + t * LANES
                oa = o[(2 * t) * bs:(2 * t + 1) * bs, :]
                ob = o[(2 * t + 1) * bs:(2 * t + 2) * bs, :]
                o_ref[:, c0:c0 + LANES] = jnp.where(low, oa, ob).astype(o_ref.dtype)
            parts[2 * pp + half]()


def _swa_side(sq, q_col, skv, k_col, sinks, q_gain, k_gain, batch, seq, ni):
    t = batch * seq
    nb = seq // SWA_BLOCK
    npairs = SWA_KV_HEADS // 2
    spp = min(SWA_STEP_PAIRS, npairs)
    pair_groups = npairs // spp
    kvw = SWA_KV_HEADS * SWA_HD
    qw, kw = spp * 2 * (SWA_HEADS // SWA_KV_HEADS) * SWA_HD, spp * LANES
    assert q_col % qw == 0 and k_col % kw == 0 and kvw % kw == 0

    def at(col, prev=False):
        def index(j, i):
            s = j * ni + i
            blk = s // pair_groups
            if prev:
                blk = blk - jnp.where(blk % nb > 0, 1, 0)
            return blk, col + s % pair_groups
        return index

    const = lambda j, i: (0, 0)
    kc, vc = k_col // kw, (k_col + kvw) // kw
    in_specs = [
        pl.BlockSpec(memory_space=pltpu.SMEM),
        pl.BlockSpec((SWA_BLOCK, qw), at(q_col // qw)),
        pl.BlockSpec((SWA_BLOCK, kw), at(kc)),
        pl.BlockSpec((SWA_BLOCK, kw), at(kc, prev=True)),
        pl.BlockSpec((SWA_BLOCK, kw), at(vc)),
        pl.BlockSpec((SWA_BLOCK, kw), at(vc, prev=True)),
        pl.BlockSpec((1, LANES), const),
        pl.BlockSpec((1, LANES), const),
    ]
    g2 = lambda g: jnp.concatenate([g, g]).reshape(1, LANES).astype(_F32)
    side = _Side(functools.partial(_swa_kernel, nb=nb, spp=spp), 2 * spp,
                 [sinks.astype(_F32), sq, skv, skv, skv, skv, g2(q_gain), g2(k_gain)], in_specs,
                 [pl.BlockSpec((SWA_BLOCK, qw), at(0))],
                 [jax.ShapeDtypeStruct((t, SWA_HEADS * SWA_HD), _BF16)], vmem=16 << 20)
    return side, batch * nb * pair_groups


def _memattn_kernel(q_ref, k_ref, v_ref, g_ref, o_ref):
    hd = q_ref.shape[1]
    q = q_ref[...].astype(_F32)
    ms = jnp.mean(q * q, axis=-1, keepdims=True)
    qn = (q * lax.rsqrt(ms + RMS_EPS) * g_ref[...] * (hd ** -0.5)).astype(_BF16)
    s = _nt(qn, k_ref[...])
    m = jnp.max(s, axis=-1, keepdims=True)
    e = jnp.exp(s - m)
    denom = jnp.sum(e, axis=-1, keepdims=True)
    o = jnp.dot(e.astype(_BF16), v_ref[...], preferred_element_type=_F32) / denom
    o_ref[...] = o.astype(o_ref.dtype)


def _memattn(mq, kc, vm, q_gain, batch, seq, mem_len, d, q_blk=0, v_blk=0, tm=1024):
    t = mq.shape[0]
    tm = _tile(seq, tm)
    ns = seq // tm
    hd = d // MEM_HEADS
    return pl.pallas_call(
        _memattn_kernel,
        grid=(batch, ns, MEM_HEADS),
        in_specs=[
            pl.BlockSpec((tm, hd), lambda b, i, h: (b * ns + i, q_blk + h)),
            pl.BlockSpec((mem_len, hd), lambda b, i, h: (b, h)),
            pl.BlockSpec((mem_len, hd), lambda b, i, h: (b, v_blk + h)),
            pl.BlockSpec((1, hd), lambda b, i, h: (0, 0)),
        ],
        out_specs=pl.BlockSpec((tm, hd), lambda b, i, h: (b * ns + i, h)),
        out_shape=jax.ShapeDtypeStruct((t, d), _BF16),
        compiler_params=_params(("parallel", "parallel", "arbitrary"),
                                4 * tm * hd * 2 + 4 * mem_len * hd * 2 + 6 * tm * hd * 4),
        name="memattn",
    )(mq, kc, vm, q_gain.reshape(1, hd).astype(_F32))


def _merge_kernel(a_ref, b_ref, c_ref, wa_ref, wb_ref, wc_ref, ga_ref, gb_ref, gc_ref, o_ref):
    def branch(x_ref, w_ref, g_ref):
        y = jnp.dot(x_ref[...], w_ref[...], preferred_element_type=_F32)
        return jax.nn.sigmoid(g_ref[...].astype(_F32)) * y

    o_ref[...] = (branch(a_ref, wa_ref, ga_ref) + branch(b_ref, wb_ref, gb_ref)
                  + branch(c_ref, wc_ref, gc_ref)).astype(o_ref.dtype)


def _merge(oa, ob, oc, wa, wb, wc, gate, gate_col=0, tm=512, tn=512):
    t, d = oa.shape
    n = wa.shape[1]
    tm, tn = _tile(t, tm), _tile(n, tn)
    nn = n // tn
    if gate_col % tn:
        gate, gate_col = gate[:, gate_col:gate_col + 3 * n], 0
    g0 = gate_col // tn
    act = lambda o: pl.BlockSpec((tm, o.shape[1]), lambda j, i: (i, 0))
    wsp = lambda w: pl.BlockSpec((w.shape[0], tn), lambda j, i: (0, j), pipeline_mode=pl.Buffered(1))
    gsp = lambda b: pl.BlockSpec((tm, tn), lambda j, i: (i, g0 + b * nn + j))
    vmem = 3 * (2 * tm * d * 2 + d * tn * 2 + 2 * tm * tn * 2) + 8 * tm * tn * 4
    return pl.pallas_call(
        _merge_kernel,
        grid=(nn, t // tm),
        in_specs=[act(oa), act(ob), act(oc), wsp(wa), wsp(wb), wsp(wc), gsp(0), gsp(1), gsp(2)],
        out_specs=pl.BlockSpec((tm, tn), lambda j, i: (i, j)),
        out_shape=jax.ShapeDtypeStruct((t, n), _BF16),
        compiler_params=_params(("parallel", "arbitrary"), vmem),
        name="merge",
    )(oa, ob, oc, wa, wb, wc, gate, gate, gate)


def _extract_topk(s, payload, kk):
    n = s.shape[0]
    pos = lax.broadcasted_iota(jnp.int32, s.shape, 0).astype(_F32)
    vals, sel = [], []
    for r in range(kk):
        m = jnp.max(s, axis=0, keepdims=True)
        first = jnp.min(jnp.where(s == m, pos, float(n)), axis=0, keepdims=True)
        hit = pos == first
        vals.append(m)
        sel.append(first if payload is None else jnp.max(jnp.where(hit, payload, -1.0), axis=0, keepdims=True))
        if r + 1 < kk:
            s = jnp.where(hit, _NEG_INF, s)
    return jnp.concatenate(vals, axis=0), jnp.concatenate(sel, axis=0)


def _route_kernel(h_ref, gain_ref, wq_ref, keys_ref, hn_ref, w_ref, q_ref, i1_ref, i2_ref, g_ref):
    kk = PEER_TOPK
    half = PEER_DQ // 2
    nk = PEER_NKEYS
    step = pl.program_id(0)
    cur, prev = step % 2, (step + 1) % 2
    tokens = h_ref.shape[0] // PEER_HEADS

    @pl.when(step == 0)
    def _():
        q_ref[1] = jnp.zeros(q_ref.shape[1:], q_ref.dtype)
        i1_ref[...] = jnp.zeros_like(i1_ref)
        i2_ref[...] = jnp.zeros_like(i2_ref)
        g_ref[...] = jnp.zeros_like(g_ref)

    x = h_ref[...]
    ms = jnp.mean(x * x, axis=-1, keepdims=True)
    hn = (x * lax.rsqrt(ms + RMS_EPS) * gain_ref[...]).astype(_BF16)
    hn_ref[...] = hn

    sub = lax.broadcasted_iota(jnp.int32, (nk, i1_ref.shape[2]), 0)

    def gate_rows(t0):
        for t in range(t0, t0 + tokens):
            a_t = jnp.where(sub == i1_ref[cur, t:t + 1, :], 1.0, 0.0).astype(_BF16)
            b_t = jnp.where(sub == i2_ref[cur, t:t + 1, :], g_ref[cur, t:t + 1, :], 0.0).astype(_BF16)
            w_ref[t] = _nt(a_t, b_t)

    experts, gates = [], []
    for h in range(PEER_HEADS):
        hs = slice(h * PEER_DQ, (h + 1) * PEER_DQ)
        q_ref[cur, :, hs] = jnp.dot(hn, wq_ref[:, hs], preferred_element_type=_F32)
        top = []
        for p in range(2):
            c0 = (2 * h + p) * half
            st = _nt(keys_ref[2 * h + p].astype(_BF16), q_ref[prev, :, c0:c0 + half].astype(_BF16))
            top.append(_extract_topk(st, None, kk))
        (s0, i0), (s1, i1) = top
        cand, ids = [], []
        tail = kk // 2
        for a in range(tail):
            nb = kk // (a + 1)
            rows = -(-nb // 8) * 8
            c = s0[a:a + 1, :] + s1[0:rows, :]
            if rows != nb:
                c = jnp.where(lax.broadcasted_iota(jnp.int32, c.shape, 0) < nb, c, _NEG_INF)
            cand.append(c)
            ids.append(i0[a:a + 1, :] * PEER_NKEYS + i1[0:rows, :])
        cand.append(s0[tail:kk, :] + s1[0:1, :])
        ids.append(i0[tail:kk, :] * PEER_NKEYS + i1[0:1, :])
        best, e = _extract_topk(jnp.concatenate(cand, axis=0), jnp.concatenate(ids, axis=0), kk)
        ex = jnp.exp(best - best[0:1, :])
        gates.append(ex / jnp.sum(ex, axis=0, keepdims=True))
        experts.append(e)
        gate_rows(h * tokens)
    e = jnp.concatenate(experts, axis=0).T.astype(jnp.int32)
    i1_ref[prev] = e >> _log2(PEER_NKEYS)
    i2_ref[prev] = e & (PEER_NKEYS - 1)
    g_ref[prev] = jnp.concatenate(gates, axis=0).T


def _route(h, gain, wq, sub_keys, tt=128):
    t, d = h.shape
    tt = _tile(t, tt)
    n = t // tt
    j = PEER_HEADS * PEER_TOPK
    nk = PEER_NKEYS
    nq = PEER_HEADS * PEER_DQ
    keys = sub_keys.reshape(PEER_HEADS * 2, nk, PEER_DQ // 2)
    tile = lambda s: (jnp.minimum(s, n - 1), 0)
    vmem = d * nq * 2 + 2 * (tt * nk * nk * 4 + tt * d * 6) + 2 * tt * nq * 4 + (12 << 20)
    return pl.pallas_call(
        _route_kernel,
        grid=(n + 2,),
        in_specs=[pl.BlockSpec((tt, d), tile),
                  pl.BlockSpec((1, d), lambda s: (0, 0)),
                  pl.BlockSpec((d, nq), lambda s: (0, 0), pipeline_mode=pl.Buffered(1)),
                  pl.BlockSpec(keys.shape, lambda s: (0, 0, 0))],
        out_specs=[pl.BlockSpec((tt, d), tile),
                   pl.BlockSpec((tt, nk, nk), lambda s: (jnp.maximum(s - 2, 0), 0, 0))],
        out_shape=[jax.ShapeDtypeStruct((t, d), _BF16), jax.ShapeDtypeStruct((t, nk, nk), _F32)],
        scratch_shapes=[pltpu.VMEM((2, tt, nq), _F32), pltpu.VMEM((2, tt, j), jnp.int32),
                        pltpu.VMEM((2, tt, j), jnp.int32), pltpu.VMEM((2, tt, j), _F32)],
        compiler_params=_params(("arbitrary",), vmem),
        name="peer_route",
    )(h, gain.reshape(1, d).astype(_F32), wq, keys)


def _gelu(x):
    return 0.5 * x * (1.0 + lax.erf(x * (2.0 ** -0.5)))


def _peer_act_kernel(x_ref, u_ref, w_ref, s_ref):
    act = _gelu(_nt(x_ref[...], u_ref[...]))
    w = jnp.transpose(w_ref[...], (1, 0, 2))
    for a in range(w.shape[0]):
        cs = slice(a * PEER_NKEYS, (a + 1) * PEER_NKEYS)
        s_ref[:, cs] = (act[:, cs] * w[a]).astype(s_ref.dtype)


def _peer_act(xn, u, w, tm=1024, rows_a=8):
    t, d = xn.shape
    ne = u.shape[0]
    nk = PEER_NKEYS
    tm = _tile(t, tm)
    eb = rows_a * nk
    vmem = 2 * (tm * d * 2 + tm * rows_a * nk * 4 + tm * eb * 2) + eb * d * 2 + 3 * tm * eb * 4
    return pl.pallas_call(
        _peer_act_kernel,
        grid=(ne // eb, t // tm),
        in_specs=[pl.BlockSpec((tm, d), lambda j, i: (i, 0)),
                  pl.BlockSpec((eb, d), lambda j, i: (j, 0), pipeline_mode=pl.Buffered(1)),
                  pl.BlockSpec((tm, rows_a, nk), lambda j, i: (i, j, 0))],
        out_specs=pl.BlockSpec((tm, eb), lambda j, i: (i, j)),
        out_shape=jax.ShapeDtypeStruct((t, ne), _BF16),
        compiler_params=_params(("parallel", "parallel"), vmem),
        name="peer_act",
    )(xn, u, w)


def _mm_acc_kernel(x_ref, w_ref, r_ref, o_ref):
    acc = jnp.dot(x_ref[...], w_ref[...], preferred_element_type=_F32)

    @pl.when(pl.program_id(2) == 0)
    def _():
        o_ref[...] = r_ref[...] + acc

    @pl.when(pl.program_id(2) > 0)
    def _():
        o_ref[...] += acc


def _matmul_acc(x, w, residual, tm=1024, tn=1024, tk=4096):
    m, k = x.shape
    n = w.shape[1]
    tm, tn, tk = _tile(m, tm), _tile(n, tn), _tile(k, tk)
    vmem = 2 * (tm * tk * 2 + tk * tn * 2 + 2 * tm * tn * 4) + tm * tn * 4
    return pl.pallas_call(
        _mm_acc_kernel,
        grid=(m // tm, n // tn, k // tk),
        in_specs=[pl.BlockSpec((tm, tk), lambda i, j, l: (i, l)),
                  pl.BlockSpec((tk, tn), lambda i, j, l: (l, j)),
                  pl.BlockSpec((tm, tn), lambda i, j, l: (i, j))],
        out_specs=pl.BlockSpec((tm, tn), lambda i, j, l: (i, j)),
        out_shape=jax.ShapeDtypeStruct((m, n), _F32),
        compiler_params=_params(("parallel", "parallel", "arbitrary"), vmem),
        name="peer_out",
    )(x, w, residual)


def _layer(h, mem, norm_mix, w_in, gla_w_gate_up, gla_gate_bias, gla_out_norm, swa_q_norm, swa_k_norm,
           swa_sinks, mem_norm, w_mem_kv, mem_q_norm, mem_k_norm, w_branch_gla, w_branch_swa,
           w_branch_mem, w_out, norm_ffn, peer_w_q, peer_sub_keys, peer_u, peer_v, batch, seq, l):
    t, d = h.shape
    mem_len = mem.shape[0] // batch
    qk = GLA_HEADS * GLA_DK
    gv = GLA_HEADS * GLA_DV
    swq = SWA_HEADS * SWA_HD
    skv = SWA_KV_HEADS * SWA_HD
    o_glr = 2 * qk + 2 * gv
    o_sq = o_glr + GLA_RANK
    o_sk = o_sq + swq
    o_mq = o_sk + 2 * skv
    o_gate = o_mq + d
    bf = lambda w: w.astype(_BF16)
    hd = d // MEM_HEADS

    xn = _rmsnorm(h, norm_mix, _BF16)
    proj = functools.partial(_matmul_w32, xn, jnp.swapaxes(w_in, 1, 2), l, transposed=True)
    qkvr = proj([(0, o_glr)], _BF16, name="proj_gla")
    glr = proj([(o_glr, LANES)], _BF16, name="proj_glr")
    skvp = proj([(o_sk, 2 * skv)], _BF16, name="proj_swa_kv")

    tm_gla = _tile(t, 512 * GLA_STEP_CHUNKS)
    tm_swa = _tile(t, 512 * min(SWA_STEP_PAIRS, SWA_KV_HEADS // 2))
    wg = bf(jnp.pad(gla_w_gate_up, ((0, LANES - GLA_RANK), (0, 0))))
    gla = _gla_side(qkvr, glr, wg, gla_gate_bias.reshape(1, qk).astype(_F32),
                    gla_out_norm.reshape(1, gv).astype(_F32), batch, seq, t // tm_gla)
    sqmq, o_a = proj([(o_sq, swq), (o_mq, d)], _BF16, tm=tm_gla, side=gla, name="proj_q_gla")
    swa = _swa_side(sqmq, 0, skvp, 0, swa_sinks, swa_q_norm, swa_k_norm, batch, seq, t // tm_swa)
    gate, o_b = proj([(o_gate, 3 * d)], _BF16, tm=tm_swa, tn=768, side=swa, weight_buffers=1,
                     name="proj_gate_swa")

    memn = _rmsnorm(mem, mem_norm, _BF16)
    kv_m = _matmul_w32(memn, w_mem_kv, l, [(0, 2 * d)], _BF16, name="mem_kv")
    k_c = _rmsnorm(kv_m[:, :d].reshape(-1, hd), mem_k_norm, _BF16).reshape(-1, d)
    mq, mq_blk = (sqmq, swq // hd) if swq % hd == 0 else (sqmq[:, swq:], 0)
    o_c = _memattn(mq, k_c, kv_m, mem_q_norm, batch, seq, mem_len, d, mq_blk, MEM_HEADS)

    mix = _merge(o_a, o_b, o_c, bf(w_branch_gla), bf(w_branch_swa), bf(w_branch_mem), gate)
    h = _matmul_w32(mix, w_out, l, [(0, d)], _F32, residual=h, name="out_proj")

    hn, w = _route(h, norm_ffn, bf(peer_w_q[l]), peer_sub_keys.astype(_F32))
    s = _peer_act(hn, bf(peer_u), w)
    return _matmul_acc(s, bf(peer_v), h)


def kernel(x, mem, norm_mix, w_in, gla_w_gate_up, gla_gate_bias, gla_out_norm, swa_q_norm, swa_k_norm,
           swa_sinks, mem_norm, w_mem_kv, mem_q_norm, mem_k_norm, w_branch_gla, w_branch_swa, w_branch_mem,
           w_out, norm_ffn, peer_w_q, peer_sub_keys, peer_u, peer_v):
    batch, seq, d = x.shape
    h = x.reshape(batch * seq, d)
    memf = mem.reshape(-1, d)
    for l in range(norm_mix.shape[0]):
        h = _layer(h, memf, norm_mix[l], w_in, gla_w_gate_up[l], gla_gate_bias[l], gla_out_norm[l],
                   swa_q_norm[l], swa_k_norm[l], swa_sinks[l], mem_norm[l], w_mem_kv, mem_q_norm[l],
                   mem_k_norm[l], w_branch_gla[l], w_branch_swa[l], w_branch_mem[l], w_out, norm_ffn[l],
                   peer_w_q, peer_sub_keys[l], peer_u[l], peer_v[l], batch, seq, l)
    return h.reshape(batch, seq, d)
```

```python
import functools

import jax
import jax.numpy as jnp
from jax import lax
from jax.experimental import pallas as pl
from jax.experimental.pallas import tpu as pltpu

_F32 = jnp.float32
_BF16 = jnp.bfloat16
_NEG_INF = float("-inf")

RMS_EPS = 1e-6
GLA_HEADS = 8
GLA_DK = 256
GLA_DV = 512
GLA_RANK = 16
GLA_TAU = 16.0
GLA_SUB = 8
GLA_STEP_HEADS = 2
GLA_STEP_CHUNKS = 2
_LOG2_E = 1.4426950408889634
SWA_HEADS = 64
SWA_KV_HEADS = 8
SWA_HD = 64
SWA_BLOCK = 128
SWA_STEP_PAIRS = 2
MEM_HEADS = 4
PEER_HEADS = 8
PEER_NKEYS = 128
PEER_DQ = 256
PEER_TOPK = 16

LANES = 128
V7X_VMEM_BYTES = 64 * 1024 * 1024


def _tile(n, pref):
    t = min(n, pref)
    while n % t:
        t -= 1
    return t


def _params(semantics, vmem_bytes):
    limit = min(int(vmem_bytes) + (8 << 20), V7X_VMEM_BYTES - (4 << 20))
    return pltpu.CompilerParams(dimension_semantics=semantics, vmem_limit_bytes=limit)


def _nt(a, b):
    return lax.dot_general(a, b, (((1,), (1,)), ((), ())), preferred_element_type=_F32)


def _log2(n):
    assert n > 0 and n & (n - 1) == 0, n
    return n.bit_length() - 1


def _tn(a, b):
    return lax.dot_general(a, b, (((0,), (0,)), ((), ())), preferred_element_type=_F32)


def _rmsnorm_kernel(x_ref, g_ref, o_ref):
    x = x_ref[...].astype(_F32)
    ms = jnp.mean(x * x, axis=-1, keepdims=True)
    o_ref[...] = (x * lax.rsqrt(ms + RMS_EPS) * g_ref[...]).astype(o_ref.dtype)


def _rmsnorm(x, gain, out_dtype, rows=256):
    r, d = x.shape
    tr = _tile(r, rows)
    blk = tr * d * (x.dtype.itemsize + jnp.dtype(out_dtype).itemsize)
    return pl.pallas_call(
        _rmsnorm_kernel,
        grid=(r // tr,),
        in_specs=[pl.BlockSpec((tr, d), lambda i: (i, 0)), pl.BlockSpec((1, d), lambda i: (0, 0))],
        out_specs=pl.BlockSpec((tr, d), lambda i: (i, 0)),
        out_shape=jax.ShapeDtypeStruct((r, d), out_dtype),
        compiler_params=_params(("parallel",), 2 * blk + 4 * tr * d * 4),
        name="rmsnorm",
    )(x, gain.reshape(1, d).astype(_F32))


class _Side:
    def __init__(self, kernel, parts, args, in_specs, out_specs, out_shape, scratch_shapes=(), vmem=0):
        self.kernel, self.parts, self.args, self.in_specs = kernel, parts, list(args), list(in_specs)
        self.out_specs, self.out_shape = list(out_specs), list(out_shape)
        self.scratch_shapes, self.vmem = list(scratch_shapes), vmem


def _cast_kernel(step, parts, *refs):
    parts[0]()
    n = len(refs) // 2
    for src, dst in zip(refs[:n], refs[n:]):
        dst[...] = src[...].astype(dst.dtype)


def _cast_side(arrays, steps, ni):
    in_specs, out_specs, out_shape = [], [], []
    for a in arrays:
        r, c = a.shape
        rows = next(b for b in range(16, r + 1, 16) if r % b == 0 and r // b <= steps)
        index = lambda j, i, nblk=r // rows: (jnp.minimum(j * ni + i, nblk - 1), 0)
        in_specs.append(pl.BlockSpec((rows, c), index))
        out_specs.append(pl.BlockSpec((rows, c), index))
        out_shape.append(jax.ShapeDtypeStruct((r, c), _BF16))
    vmem = sum(2 * 6 * s.block_shape[0] * s.block_shape[1] for s in in_specs)
    return _Side(_cast_kernel, 1, list(arrays), in_specs, out_specs, out_shape, vmem=vmem), steps


def _mm_w32_kernel(*refs, shift, has_residual, side, transposed):
    n_in = 2 + bool(shift) + has_residual
    n_side_in = len(side.in_specs) if side else 0
    n_side_out = len(side.out_specs) if side else 0
    x_ref, w_ref = refs[0], refs[1]
    o_ref = refs[n_in + n_side_in]
    wbf_ref = refs[n_in + n_side_in + 1 + n_side_out]
    tn = o_ref.shape[1]

    @pl.when(pl.program_id(1) == 0)
    def _():
        rows = _tile(w_ref.shape[0], 128 if transposed else 512)
        for r0 in range(0, w_ref.shape[0], rows):
            rs = slice(r0, r0 + rows)
            if transposed:
                if r0 + rows + shift <= tn:
                    w = w_ref[r0 + shift:r0 + rows + shift, :]
                else:
                    w = jnp.concatenate([w_ref[r0 + shift:, :], refs[2][:shift, :]], axis=0)
                wbf_ref[:, rs] = w.T.astype(_BF16)
            else:
                w = w_ref[rs, :]
                if shift:
                    w = jnp.concatenate([w, refs[2][rs, :]], axis=1)
                    w = pltpu.roll(w, w.shape[1] - shift, axis=1)[:, :tn]
                wbf_ref[rs, :] = w.astype(_BF16)

    def matmul_rows(rs):
        acc = jnp.dot(x_ref[rs, :], wbf_ref[...], preferred_element_type=_F32)
        if has_residual:
            acc = acc + refs[n_in - 1][rs, :].astype(_F32)
        o_ref[rs, :] = acc.astype(o_ref.dtype)

    if side:
        rows = x_ref.shape[0] // side.parts
        parts = [functools.partial(matmul_rows, slice(r * rows, (r + 1) * rows)) for r in range(side.parts)]
        step = pl.program_id(0) * pl.num_programs(1) + pl.program_id(1)
        side.kernel(step, parts, *refs[n_in:n_in + n_side_in],
                    *refs[n_in + n_side_in + 1:n_in + n_side_in + 1 + n_side_out],
                    *refs[n_in + n_side_in + 2 + n_side_out:])
    else:
        matmul_rows(slice(None))


def _matmul_w32(x, w, layer, windows, out_dtype, tm=1024, tn=512, residual=None, side=None,
                transposed=False, weight_buffers=2, name="matmul_w32"):
    m, k = x.shape
    shift = windows[0][0] % LANES
    assert all(c0 % LANES == shift for c0, _ in windows)
    tm = _tile(m, tm)
    ok = lambda t: all(n % t == 0 and (c0 - shift) % t == 0 for c0, n in windows)
    tn = max(t for t in range(LANES, tn + 1, LANES) if ok(t))
    n = sum(n for _, n in windows)

    def col_block(j, width):
        blk, start = 0, 0
        for c0, wn in windows:
            first = ((c0 - shift) // tn - start) * (tn // width)
            blk = jnp.where(j >= start, first + j * (tn // width), blk)
            start += wn // tn
        return blk

    osz = jnp.dtype(out_dtype).itemsize

    def wspec(width, block_of):
        mode = {} if weight_buffers == 2 else dict(pipeline_mode=pl.Buffered(weight_buffers))
        if transposed:
            return pl.BlockSpec((None, width, k), lambda j, i: (layer, block_of(j), 0), **mode)
        return pl.BlockSpec((None, k, width), lambda j, i: (layer, 0, block_of(j)), **mode)

    in_specs = [pl.BlockSpec((tm, k), lambda j, i: (i, 0)), wspec(tn, lambda j: col_block(j, tn))]
    args = [x, w]
    vmem = 2 * (tm * k * 2 + tm * tn * osz) + weight_buffers * k * tn * 4 + k * tn * 2 + tm * tn * 4
    if shift:
        in_specs.append(wspec(LANES, lambda j: col_block(j, LANES) + tn // LANES))
        args.append(w)
        vmem += weight_buffers * k * LANES * 4
    if residual is not None:
        in_specs.append(pl.BlockSpec((tm, tn), lambda j, i: (i, j)))
        args.append(residual)
        vmem += 2 * tm * tn * residual.dtype.itemsize
    out_specs = [pl.BlockSpec((tm, tn), lambda j, i: (i, j))]
    out_shape = [jax.ShapeDtypeStruct((m, n), out_dtype)]
    scratch = [pltpu.VMEM((k, tn), _BF16)]
    alone = None
    if callable(side):
        side = side((n // tn) * (m // tm), m // tm)
    if side:
        side, steps = side
        if steps != (n // tn) * (m // tm):
            alone, side = _run_side(side, steps, m // tm, name + "_side"), None
    if side:
        in_specs += side.in_specs
        args += side.args
        out_specs += side.out_specs
        out_shape += side.out_shape
        scratch += side.scratch_shapes
        vmem += side.vmem
    out = pl.pallas_call(
        functools.partial(_mm_w32_kernel, shift=shift, has_residual=residual is not None, side=side,
                          transposed=transposed),
        grid=(n // tn, m // tm),
        in_specs=in_specs,
        out_specs=out_specs,
        out_shape=out_shape,
        scratch_shapes=scratch,
        compiler_params=_params(("arbitrary", "arbitrary"), vmem),
        name=name,
    )(*args)
    if alone is not None:
        return [out[0], *alone]
    return out if side else out[0]


def _log_sigmoid(z):
    return jnp.minimum(z, 0.0) - jnp.log(1.0 + jnp.exp(-jnp.abs(z)))


def _cumsum_rows(x):
    c, n = x.shape
    row = lax.broadcasted_iota(jnp.int32, (c, c), 0)
    col = lax.broadcasted_iota(jnp.int32, (c, c), 1)
    tri = jnp.where(row >= col, 1.0, 0.0).astype(_BF16)
    hi = x.astype(_BF16)
    rest = x - hi.astype(_F32)
    mid = rest.astype(_BF16)
    lo = (rest - mid.astype(_F32)).astype(_BF16)
    parts = jnp.dot(tri, jnp.concatenate([hi, mid, lo], axis=1), preferred_element_type=_F32)
    return parts[:, :n] + parts[:, n:2 * n] + parts[:, 2 * n:]


def _gla_chunk(q, k, v, b, state, midway):
    c = q.shape[0]
    row = lax.broadcasted_iota(jnp.int32, (c, c), 0)
    col = lax.broadcasted_iota(jnp.int32, (c, c), 1)
    b_last = b[c - 1:c, :]

    attn = jnp.zeros((c, c), _F32)
    row1 = lax.broadcasted_iota(jnp.int32, (c, 1), 0)
    s = c // 2
    while s >= GLA_SUB:
        pieces = []
        for lo in range(0, c, 2 * s):
            mid = lo + s
            ref_row = b[mid - 1:mid, :]
            pieces.append(ref_row - b[lo:mid, :])
            pieces.append(b[mid:mid + s, :] - ref_row)
        x = jnp.exp2(jnp.concatenate(pieces, axis=0))
        upper = ((row1 >> _log2(s)) & 1) == 1
        qt = jnp.where(upper, q * x, 0.0).astype(_BF16)
        kt = jnp.where(upper, 0.0, k * x).astype(_BF16)
        same = (row >> _log2(2 * s)) == (col >> _log2(2 * s))
        attn = attn + jnp.where(same, _nt(qt, kt), 0.0)
        s //= 2

    midway()
    sub_row = lax.broadcasted_iota(jnp.int32, (GLA_SUB, 1), 0)
    lane = lax.broadcasted_iota(jnp.int32, (GLA_SUB, c), 1)
    diag = []
    for i0 in range(0, c, GLA_SUB):
        qi, ki, bi = q[i0:i0 + GLA_SUB, :], k[i0:i0 + GLA_SUB, :], b[i0:i0 + GLA_SUB, :]
        blk = jnp.zeros((GLA_SUB, c), _F32)
        for j in range(GLA_SUB):
            dec = jnp.exp2(jnp.where(sub_row >= j, bi - bi[j:j + 1, :], _NEG_INF))
            sc = jnp.sum(qi * dec * ki[j:j + 1, :], axis=1, keepdims=True)
            blk = jnp.where(lane == i0 + j, sc, blk)
        diag.append(blk)
    attn = attn + jnp.concatenate(diag, axis=0)

    qb = (q * jnp.exp2(b)).astype(_BF16)
    o = _nt(qb, state.astype(_BF16)) + jnp.dot(attn.astype(_BF16), v, preferred_element_type=_F32)
    kd = (k * jnp.exp2(b_last - b)).astype(_BF16)
    return o, state * jnp.exp2(b_last) + _tn(v, kd)


def _gla_kernel(step, parts, q_ref, k_ref, v_ref, r_ref, glr_ref, wg_ref, bias_ref, gain_ref, o_ref,
                state_ref, *, nc):
    @pl.when(step % nc == 0)
    def _():
        state_ref[...] = jnp.zeros_like(state_ref)

    c = q_ref.shape[0] // GLA_STEP_CHUNKS
    z = jnp.dot(glr_ref[...], wg_ref[...], preferred_element_type=_F32) + bias_ref[...]
    log_a = _log_sigmoid(z) * (_LOG2_E / GLA_TAU)
    b_all = [_cumsum_rows(log_a[s * c:(s + 1) * c, :]) for s in range(GLA_STEP_CHUNKS)]
    states = [state_ref[h] for h in range(GLA_STEP_HEADS)]
    parts = iter(parts)
    for s in range(GLA_STEP_CHUNKS):
        rows = slice(s * c, (s + 1) * c)
        for h in range(GLA_STEP_HEADS):
            next(parts)()
            ks = slice(h * GLA_DK, (h + 1) * GLA_DK)
            vs = slice(h * GLA_DV, (h + 1) * GLA_DV)
            q = q_ref[rows, ks].astype(_F32) * (GLA_DK ** -0.5)
            o, states[h] = _gla_chunk(q, k_ref[rows, ks].astype(_F32), v_ref[rows, vs], b_all[s][:, ks],
                                      states[h], next(parts))
            ms = jnp.mean(o * o, axis=-1, keepdims=True)
            on = o * lax.rsqrt(ms + RMS_EPS) * gain_ref[:, vs]
            r = r_ref[rows, vs].astype(_F32)
            o_ref[rows, vs] = (on * (r * jax.nn.sigmoid(r))).astype(o_ref.dtype)
    for h in range(GLA_STEP_HEADS):
        state_ref[h] = states[h]


def _run_side(side, steps, ni, name):
    def kern(*refs):
        side.kernel(pl.program_id(0) * pl.num_programs(1) + pl.program_id(1),
                    [lambda: None] * side.parts, *refs)

    return pl.pallas_call(
        kern, grid=(steps // ni, ni), in_specs=side.in_specs, out_specs=side.out_specs,
        out_shape=side.out_shape, scratch_shapes=side.scratch_shapes,
        compiler_params=_params(("arbitrary", "arbitrary"), side.vmem), name=name)(*side.args)


def _gla_side(qkvr, glr, wg, bias, gain, batch, seq, ni, chunk=128):
    t = batch * seq
    chunk = chunk * GLA_STEP_CHUNKS
    nc = seq // chunk
    hs = GLA_STEP_HEADS
    wk, wv = hs * GLA_DK, hs * GLA_DV
    groups = GLA_HEADS // hs
    kv = 2 * groups * wk // wv

    def at(col):
        def index(j, i):
            s = j * ni + i
            return (s // (groups * nc)) * nc + s % nc, col + (s // nc) % groups
        return index

    head = lambda j, i: (0, ((j * ni + i) // nc) % groups)
    in_specs = [
        pl.BlockSpec((chunk, wk), at(0)),
        pl.BlockSpec((chunk, wk), at(groups)),
        pl.BlockSpec((chunk, wv), at(kv)),
        pl.BlockSpec((chunk, wv), at(kv + groups)),
        pl.BlockSpec((chunk, LANES), lambda j, i: (at(0)(j, i)[0], 0)),
        pl.BlockSpec((LANES, wk), head),
        pl.BlockSpec((1, wk), head),
        pl.BlockSpec((1, wv), head),
    ]
    side = _Side(functools.partial(_gla_kernel, nc=nc), 2 * hs * GLA_STEP_CHUNKS,
                 [qkvr, qkvr, qkvr, qkvr, glr, wg, bias, gain], in_specs,
                 [pl.BlockSpec((chunk, wv), at(0))], [jax.ShapeDtypeStruct((t, GLA_HEADS * GLA_DV), _BF16)],
                 [pltpu.VMEM((hs, GLA_DV, GLA_DK), _F32)], vmem=16 << 20)
    return side, batch * groups * nc


def _pair_rmsnorm(x, gain2):
    lane = lax.broadcasted_iota(jnp.int32, x.shape, 1)
    low = lane < SWA_HD
    sq = x * x
    s_all = jnp.sum(sq, axis=1, keepdims=True)
    s_low = jnp.sum(jnp.where(low, sq, 0.0), axis=1, keepdims=True)
    ms = jnp.where(low, s_low, s_all - s_low) * (1.0 / SWA_HD)
    return x * lax.rsqrt(ms + RMS_EPS) * gain2


def _swa_kernel(step, parts, sinks_ref, q_ref, kc_ref, kp_ref, vc_ref, vp_ref, qg_ref, kg_ref, o_ref, *, nb, spp):
    pair_groups = SWA_KV_HEADS // 2 // spp
    blk = (step // pair_groups) % nb
    bs = SWA_BLOCK
    group = SWA_HEADS // SWA_KV_HEADS
    pairs = group // 2
    pw = 2 * group * SWA_HD
    lane = lax.broadcasted_iota(jnp.int32, (bs, LANES), 1)
    low = lane < SWA_HD

    q_loc = lax.broadcasted_iota(jnp.int32, (group * bs, bs), 0) & (bs - 1)
    from_cur = lax.broadcasted_iota(jnp.int32, (group * bs, bs), 1) <= q_loc
    prev_bias = jnp.where(blk > 0, 0.0, _NEG_INF)

    for pp in range(spp):
        p = (step % pair_groups) * spp + pp
        ls = slice(pp * LANES, (pp + 1) * LANES)
        tiles = [_pair_rmsnorm(kc_ref[:, ls].astype(_F32), kg_ref[...]),
                 _pair_rmsnorm(kp_ref[:, ls].astype(_F32), kg_ref[...]),
                 vc_ref[:, ls].astype(_F32), vp_ref[:, ls].astype(_F32)]
        swapped = [pltpu.roll(x, SWA_HD, axis=1) for x in tiles]
        for half in range(2):
            kc, kp, vc, vp = [(jnp.where(low, x, xs) if half == 0 else jnp.where(low, xs, x)).astype(_BF16)
                              for x, xs in zip(tiles, swapped)]
            qs, sink = [], []
            for t in range(pairs):
                c0 = pp * pw + half * group * SWA_HD + t * LANES
                qn = _pair_rmsnorm(q_ref[:, c0:c0 + LANES].astype(_F32), qg_ref[...]) * (SWA_HD ** -0.5)
                qs.append(jnp.where(low, qn, 0.0))
                qs.append(jnp.where(low, 0.0, qn))
                head = (2 * p + half) * group + 2 * t
                sink.append(jnp.full((bs, 1), sinks_ref[head], _F32))
                sink.append(jnp.full((bs, 1), sinks_ref[head + 1], _F32))
            qstack = jnp.concatenate(qs, axis=0).astype(_BF16)
            sink = jnp.concatenate(sink, axis=0)
            s = jnp.where(from_cur, _nt(qstack, kc), _nt(qstack, kp) + prev_bias)
            m = jnp.maximum(jnp.max(s, axis=1, keepdims=True), sink)
            e = jnp.exp(s - m)
            denom = jnp.sum(e, axis=1, keepdims=True) + jnp.exp(sink - m)
            e_cur = jnp.where(from_cur, e, 0.0).astype(_BF16)
            e_prev = jnp.where(from_cur, 0.0, e).astype(_BF16)
            o = (jnp.dot(e_cur, vc, preferred_element_type=_F32)
                 + jnp.dot(e_prev, vp, preferred_element_type=_F32)) / denom
            for t in range(pairs):
                c0 = pp * pw + half * group * SWA_HD + t * LANES
                oa = o[(2 * t) * bs:(2 * t + 1) * bs, :]
                ob = o[(2 * t + 1) * bs:(2 * t + 2) * bs, :]
                o_ref[:, c0:c0 + LANES] = jnp.where(low, oa, ob).astype(o_ref.dtype)
            parts[2 * pp + half]()


def _swa_side(sq, q_col, skv, k_col, sinks, q_gain, k_gain, batch, seq, ni):
    t = batch * seq
    nb = seq // SWA_BLOCK
    npairs = SWA_KV_HEADS // 2
    spp = min(SWA_STEP_PAIRS, npairs)
    pair_groups = npairs // spp
    kvw = SWA_KV_HEADS * SWA_HD
    qw, kw = spp * 2 * (SWA_HEADS // SWA_KV_HEADS) * SWA_HD, spp * LANES
    assert q_col % qw == 0 and k_col % kw == 0 and kvw % kw == 0

    def at(col, prev=False):
        def index(j, i):
            s = j * ni + i
            blk = s // pair_groups
            if prev:
                blk = blk - jnp.where(blk % nb > 0, 1, 0)
            return blk, col + s % pair_groups
        return index

    const = lambda j, i: (0, 0)
    kc, vc = k_col // kw, (k_col + kvw) // kw
    in_specs = [
        pl.BlockSpec(memory_space=pltpu.SMEM),
        pl.BlockSpec((SWA_BLOCK, qw), at(q_col // qw)),
        pl.BlockSpec((SWA_BLOCK, kw), at(kc)),
        pl.BlockSpec((SWA_BLOCK, kw), at(kc, prev=True)),
        pl.BlockSpec((SWA_BLOCK, kw), at(vc)),
        pl.BlockSpec((SWA_BLOCK, kw), at(vc, prev=True)),
        pl.BlockSpec((1, LANES), const),
        pl.BlockSpec((1, LANES), const),
    ]
    g2 = lambda g: jnp.concatenate([g, g]).reshape(1, LANES).astype(_F32)
    side = _Side(functools.partial(_swa_kernel, nb=nb, spp=spp), 2 * spp,
                 [sinks.astype(_F32), sq, skv, skv, skv, skv, g2(q_gain), g2(k_gain)], in_specs,
                 [pl.BlockSpec((SWA_BLOCK, qw), at(0))],
                 [jax.ShapeDtypeStruct((t, SWA_HEADS * SWA_HD), _BF16)], vmem=16 << 20)
    return side, batch * nb * pair_groups


def _memattn_kernel(q_ref, k_ref, v_ref, g_ref, o_ref):
    hd = q_ref.shape[1]
    q = q_ref[...].astype(_F32)
    ms = jnp.mean(q * q, axis=-1, keepdims=True)
    qn = (q * lax.rsqrt(ms + RMS_EPS) * g_ref[...] * (hd ** -0.5)).astype(_BF16)
    s = _nt(qn, k_ref[...])
    m = jnp.max(s, axis=-1, keepdims=True)
    e = jnp.exp(s - m)
    denom = jnp.sum(e, axis=-1, keepdims=True)
    o = jnp.dot(e.astype(_BF16), v_ref[...], preferred_element_type=_F32) / denom
    o_ref[...] = o.astype(o_ref.dtype)


def _memattn(mq, kc, vm, q_gain, batch, seq, mem_len, d, q_blk=0, v_blk=0, tm=1024):
    t = mq.shape[0]
    tm = _tile(seq, tm)
    ns = seq // tm
    hd = d // MEM_HEADS
    return pl.pallas_call(
        _memattn_kernel,
        grid=(batch, ns, MEM_HEADS),
        in_specs=[
            pl.BlockSpec((tm, hd), lambda b, i, h: (b * ns + i, q_blk + h)),
            pl.BlockSpec((mem_len, hd), lambda b, i, h: (b, h)),
            pl.BlockSpec((mem_len, hd), lambda b, i, h: (b, v_blk + h)),
            pl.BlockSpec((1, hd), lambda b, i, h: (0, 0)),
        ],
        out_specs=pl.BlockSpec((tm, hd), lambda b, i, h: (b * ns + i, h)),
        out_shape=jax.ShapeDtypeStruct((t, d), _BF16),
        compiler_params=_params(("parallel", "parallel", "arbitrary"),
                                4 * tm * hd * 2 + 4 * mem_len * hd * 2 + 6 * tm * hd * 4),
        name="memattn",
    )(mq, kc, vm, q_gain.reshape(1, hd).astype(_F32))


def _merge_kernel(a_ref, b_ref, c_ref, wa_ref, wb_ref, wc_ref, ga_ref, gb_ref, gc_ref, *rest):
    def branch(x_ref, w_ref, g_ref):
        y = jnp.dot(x_ref[...], w_ref[...], preferred_element_type=_F32)
        return jax.nn.sigmoid(g_ref[...].astype(_F32)) * y

    ncast = (len(rest) - 1) // 2
    o_ref = rest[ncast]
    o_ref[...] = (branch(a_ref, wa_ref, ga_ref) + branch(b_ref, wb_ref, gb_ref)
                  + branch(c_ref, wc_ref, gc_ref)).astype(o_ref.dtype)
    _cast_kernel(None, [lambda: None], *rest[:ncast], *rest[ncast + 1:])


def _merge(oa, ob, oc, wa, wb, wc, gate, gate_col=0, cast=(), tm=512, tn=512):
    t, d = oa.shape
    n = wa.shape[1]
    tm, tn = _tile(t, tm), _tile(n, tn)
    nn = n // tn
    ni = t // tm
    if gate_col % tn:
        gate, gate_col = gate[:, gate_col:gate_col + 3 * n], 0
    g0 = gate_col // tn
    act = lambda o: pl.BlockSpec((tm, o.shape[1]), lambda j, i: (i, 0))
    wsp = lambda w: pl.BlockSpec((w.shape[0], tn), lambda j, i: (0, j), pipeline_mode=pl.Buffered(1))
    gsp = lambda b: pl.BlockSpec((tm, tn), lambda j, i: (i, g0 + b * nn + j))
    casts, _ = _cast_side(cast, nn * ni, ni)
    vmem = 3 * (2 * tm * d * 2 + d * tn * 2 + 2 * tm * tn * 2) + 8 * tm * tn * 4 + casts.vmem
    out = pl.pallas_call(
        _merge_kernel,
        grid=(nn, ni),
        in_specs=[act(oa), act(ob), act(oc), wsp(wa), wsp(wb), wsp(wc), gsp(0), gsp(1), gsp(2)] + casts.in_specs,
        out_specs=[pl.BlockSpec((tm, tn), lambda j, i: (i, j))] + casts.out_specs,
        out_shape=[jax.ShapeDtypeStruct((t, n), _BF16)] + casts.out_shape,
        compiler_params=_params(("arbitrary", "arbitrary"), vmem),
        name="merge",
    )(oa, ob, oc, wa, wb, wc, gate, gate, gate, *casts.args)
    return out


def _extract_topk(s, payload, kk):
    n = s.shape[0]
    pos = lax.broadcasted_iota(jnp.int32, s.shape, 0).astype(_F32)
    vals, sel = [], []
    for r in range(kk):
        m = jnp.max(s, axis=0, keepdims=True)
        first = jnp.min(jnp.where(s == m, pos, float(n)), axis=0, keepdims=True)
        hit = pos == first
        vals.append(m)
        sel.append(first if payload is None else jnp.max(jnp.where(hit, payload, -1.0), axis=0, keepdims=True))
        if r + 1 < kk:
            s = jnp.where(hit, _NEG_INF, s)
    return jnp.concatenate(vals, axis=0), jnp.concatenate(sel, axis=0)


def _route_kernel(h_ref, gain_ref, wq_ref, keys_ref, hn_ref, w_ref, q_ref, i1_ref, i2_ref, g_ref):
    kk = PEER_TOPK
    half = PEER_DQ // 2
    nk = PEER_NKEYS
    step = pl.program_id(0)
    cur, prev = step % 2, (step + 1) % 2
    tokens = h_ref.shape[0] // PEER_HEADS

    @pl.when(step == 0)
    def _():
        q_ref[1] = jnp.zeros(q_ref.shape[1:], q_ref.dtype)
        i1_ref[...] = jnp.zeros_like(i1_ref)
        i2_ref[...] = jnp.zeros_like(i2_ref)
        g_ref[...] = jnp.zeros_like(g_ref)

    x = h_ref[...]
    ms = jnp.mean(x * x, axis=-1, keepdims=True)
    hn = (x * lax.rsqrt(ms + RMS_EPS) * gain_ref[...]).astype(_BF16)
    hn_ref[...] = hn

    sub = lax.broadcasted_iota(jnp.int32, (nk, i1_ref.shape[2]), 0)

    def gate_rows(t0):
        for t in range(t0, t0 + tokens):
            a_t = jnp.where(sub == i1_ref[cur, t:t + 1, :], 1.0, 0.0).astype(_BF16)
            b_t = jnp.where(sub == i2_ref[cur, t:t + 1, :], g_ref[cur, t:t + 1, :], 0.0).astype(_BF16)
            w_ref[t] = _nt(a_t, b_t)

    experts, gates = [], []
    for h in range(PEER_HEADS):
        hs = slice(h * PEER_DQ, (h + 1) * PEER_DQ)
        q_ref[cur, :, hs] = jnp.dot(hn, wq_ref[:, hs], preferred_element_type=_F32)
        top = []
        for p in range(2):
            c0 = (2 * h + p) * half
            st = _nt(keys_ref[2 * h + p].astype(_BF16), q_ref[prev, :, c0:c0 + half].astype(_BF16))
            top.append(_extract_topk(st, None, kk))
        (s0, i0), (s1, i1) = top
        cand, ids = [], []
        tail = kk // 2
        for a in range(tail):
            nb = kk // (a + 1)
            rows = -(-nb // 8) * 8
            c = s0[a:a + 1, :] + s1[0:rows, :]
            if rows != nb:
                c = jnp.where(lax.broadcasted_iota(jnp.int32, c.shape, 0) < nb, c, _NEG_INF)
            cand.append(c)
            ids.append(i0[a:a + 1, :] * PEER_NKEYS + i1[0:rows, :])
        cand.append(s0[tail:kk, :] + s1[0:1, :])
        ids.append(i0[tail:kk, :] * PEER_NKEYS + i1[0:1, :])
        best, e = _extract_topk(jnp.concatenate(cand, axis=0), jnp.concatenate(ids, axis=0), kk)
        ex = jnp.exp(best - best[0:1, :])
        gates.append(ex / jnp.sum(ex, axis=0, keepdims=True))
        experts.append(e)
        gate_rows(h * tokens)
    e = jnp.concatenate(experts, axis=0).T.astype(jnp.int32)
    i1_ref[prev] = e >> _log2(PEER_NKEYS)
    i2_ref[prev] = e & (PEER_NKEYS - 1)
    g_ref[prev] = jnp.concatenate(gates, axis=0).T


def _route(h, gain, wq, sub_keys, tt=128):
    t, d = h.shape
    tt = _tile(t, tt)
    n = t // tt
    j = PEER_HEADS * PEER_TOPK
    nk = PEER_NKEYS
    nq = PEER_HEADS * PEER_DQ
    keys = sub_keys.reshape(PEER_HEADS * 2, nk, PEER_DQ // 2)
    tile = lambda s: (jnp.minimum(s, n - 1), 0)
    vmem = d * nq * 2 + 2 * (tt * nk * nk * 4 + tt * d * 6) + 2 * tt * nq * 4 + (12 << 20)
    return pl.pallas_call(
        _route_kernel,
        grid=(n + 2,),
        in_specs=[pl.BlockSpec((tt, d), tile),
                  pl.BlockSpec((1, d), lambda s: (0, 0)),
                  pl.BlockSpec((d, nq), lambda s: (0, 0), pipeline_mode=pl.Buffered(1)),
                  pl.BlockSpec(keys.shape, lambda s: (0, 0, 0))],
        out_specs=[pl.BlockSpec((tt, d), tile),
                   pl.BlockSpec((tt, nk, nk), lambda s: (jnp.maximum(s - 2, 0), 0, 0))],
        out_shape=[jax.ShapeDtypeStruct((t, d), _BF16), jax.ShapeDtypeStruct((t, nk, nk), _F32)],
        scratch_shapes=[pltpu.VMEM((2, tt, nq), _F32), pltpu.VMEM((2, tt, j), jnp.int32),
                        pltpu.VMEM((2, tt, j), jnp.int32), pltpu.VMEM((2, tt, j), _F32)],
        compiler_params=_params(("arbitrary",), vmem),
        name="peer_route",
    )(h, gain.reshape(1, d).astype(_F32), wq, keys)


def _gelu(x):
    return 0.5 * x * (1.0 + lax.erf(x * (2.0 ** -0.5)))


def _peer_act_kernel(x_ref, u_ref, w_ref, s_ref):
    act = _gelu(_nt(x_ref[...], u_ref[...]))
    w = jnp.transpose(w_ref[...], (1, 0, 2))
    for a in range(w.shape[0]):
        cs = slice(a * PEER_NKEYS, (a + 1) * PEER_NKEYS)
        s_ref[:, cs] = (act[:, cs] * w[a]).astype(s_ref.dtype)


def _peer_act(xn, u, w, tm=1024, rows_a=8):
    t, d = xn.shape
    ne = u.shape[0]
    nk = PEER_NKEYS
    tm = _tile(t, tm)
    eb = rows_a * nk
    vmem = 2 * (tm * d * 2 + tm * rows_a * nk * 4 + tm * eb * 2) + eb * d * 2 + 3 * tm * eb * 4
    return pl.pallas_call(
        _peer_act_kernel,
        grid=(ne // eb, t // tm),
        in_specs=[pl.BlockSpec((tm, d), lambda j, i: (i, 0)),
                  pl.BlockSpec((eb, d), lambda j, i: (j, 0), pipeline_mode=pl.Buffered(1)),
                  pl.BlockSpec((tm, rows_a, nk), lambda j, i: (i, j, 0))],
        out_specs=pl.BlockSpec((tm, eb), lambda j, i: (i, j)),
        out_shape=jax.ShapeDtypeStruct((t, ne), _BF16),
        compiler_params=_params(("parallel", "parallel"), vmem),
        name="peer_act",
    )(xn, u, w)


def _mm_acc_kernel(x_ref, w_ref, r_ref, o_ref):
    acc = jnp.dot(x_ref[...], w_ref[...], preferred_element_type=_F32)

    @pl.when(pl.program_id(2) == 0)
    def _():
        o_ref[...] = r_ref[...] + acc

    @pl.when(pl.program_id(2) > 0)
    def _():
        o_ref[...] += acc


def _matmul_acc(x, w, residual, tm=1024, tn=1024, tk=4096):
    m, k = x.shape
    n = w.shape[1]
    tm, tn, tk = _tile(m, tm), _tile(n, tn), _tile(k, tk)
    vmem = 2 * (tm * tk * 2 + tk * tn * 2 + 2 * tm * tn * 4) + tm * tn * 4
    return pl.pallas_call(
        _mm_acc_kernel,
        grid=(m // tm, n // tn, k // tk),
        in_specs=[pl.BlockSpec((tm, tk), lambda i, j, l: (i, l)),
                  pl.BlockSpec((tk, tn), lambda i, j, l: (l, j)),
                  pl.BlockSpec((tm, tn), lambda i, j, l: (i, j))],
        out_specs=pl.BlockSpec((tm, tn), lambda i, j, l: (i, j)),
        out_shape=jax.ShapeDtypeStruct((m, n), _F32),
        compiler_params=_params(("parallel", "parallel", "arbitrary"), vmem),
        name="peer_out",
    )(x, w, residual)


def _layer(h, mem, norm_mix, w_in, gla_w_gate_up, gla_gate_bias, gla_out_norm, swa_q_norm, swa_k_norm,
           swa_sinks, mem_norm, w_mem_kv, mem_q_norm, mem_k_norm, w_branch_gla, w_branch_swa,
           w_branch_mem, w_out, norm_ffn, peer_w_q, peer_sub_keys, peer_u, peer_v, batch, seq, l):
    t, d = h.shape
    mem_len = mem.shape[0] // batch
    qk = GLA_HEADS * GLA_DK
    gv = GLA_HEADS * GLA_DV
    swq = SWA_HEADS * SWA_HD
    skv = SWA_KV_HEADS * SWA_HD
    o_glr = 2 * qk + 2 * gv
    o_sq = o_glr + GLA_RANK
    o_sk = o_sq + swq
    o_mq = o_sk + 2 * skv
    o_gate = o_mq + d
    bf = lambda w: w.astype(_BF16)
    hd = d // MEM_HEADS

    xn = _rmsnorm(h, norm_mix, _BF16)
    proj = functools.partial(_matmul_w32, xn, jnp.swapaxes(w_in, 1, 2), l, transposed=True)
    branch_w = [w_branch_gla, w_branch_swa, w_branch_mem]
    qkvr, *branch_w = proj([(0, o_glr)], _BF16, side=functools.partial(_cast_side, branch_w), name="proj_gla")
    glr = proj([(o_glr, LANES)], _BF16, name="proj_glr")
    skvp = proj([(o_sk, 2 * skv)], _BF16, name="proj_swa_kv")

    tm_gla = _tile(t, 512 * GLA_STEP_CHUNKS)
    tm_swa = _tile(t, 512 * min(SWA_STEP_PAIRS, SWA_KV_HEADS // 2))
    wg = bf(jnp.pad(gla_w_gate_up, ((0, LANES - GLA_RANK), (0, 0))))
    gla = _gla_side(qkvr, glr, wg, gla_gate_bias.reshape(1, qk).astype(_F32),
                    gla_out_norm.reshape(1, gv).astype(_F32), batch, seq, t // tm_gla)
    sqmq, o_a = proj([(o_sq, swq), (o_mq, d)], _BF16, tm=tm_gla, side=gla, name="proj_q_gla")
    swa = _swa_side(sqmq, 0, skvp, 0, swa_sinks, swa_q_norm, swa_k_norm, batch, seq, t // tm_swa)
    gate, o_b = proj([(o_gate, 3 * d)], _BF16, tm=tm_swa, tn=768, side=swa, weight_buffers=1,
                     name="proj_gate_swa")

    memn = _rmsnorm(mem, mem_norm, _BF16)
    kv_m = _matmul_w32(memn, w_mem_kv, l, [(0, 2 * d)], _BF16, name="mem_kv")
    k_c = _rmsnorm(kv_m[:, :d].reshape(-1, hd), mem_k_norm, _BF16).reshape(-1, d)
    mq, mq_blk = (sqmq, swq // hd) if swq % hd == 0 else (sqmq[:, swq:], 0)
    o_c = _memattn(mq, k_c, kv_m, mem_q_norm, batch, seq, mem_len, d, mq_blk, MEM_HEADS)

    mix, wq, u, v = _merge(o_a, o_b, o_c, *branch_w, gate, cast=[peer_w_q[l], peer_u, peer_v])
    h = _matmul_w32(mix, w_out, l, [(0, d)], _F32, residual=h, name="out_proj")

    hn, w = _route(h, norm_ffn, wq, peer_sub_keys.astype(_F32))
    s = _peer_act(hn, u, w)
    return _matmul_acc(s, v, h)


def kernel(x, mem, norm_mix, w_in, gla_w_gate_up, gla_gate_bias, gla_out_norm, swa_q_norm, swa_k_norm,
           swa_sinks, mem_norm, w_mem_kv, mem_q_norm, mem_k_norm, w_branch_gla, w_branch_swa, w_branch_mem,
           w_out, norm_ffn, peer_w_q, peer_sub_keys, peer_u, peer_v):
    batch, seq, d = x.shape
    h = x.reshape(batch * seq, d)
    memf = mem.reshape(-1, d)
    for l in range(norm_mix.shape[0]):
        h = _layer(h, memf, norm_mix[l], w_in, gla_w_gate_up[l], gla_gate_bias[l], gla_out_norm[l],
                   swa_q_norm[l], swa_k_norm[l], swa_sinks[l], mem_norm[l], w_mem_kv, mem_q_norm[l],
                   mem_k_norm[l], w_branch_gla[l], w_branch_swa[l], w_branch_mem[l], w_out, norm_ffn[l],
                   peer_w_q, peer_sub_keys[l], peer_u[l], peer_v[l], batch, seq, l)
    return h.reshape(batch, seq, d)
```

```python
import functools

import jax
import jax.numpy as jnp
from jax import lax
from jax.experimental import pallas as pl
from jax.experimental.pallas import tpu as pltpu

_F32 = jnp.float32
_BF16 = jnp.bfloat16
_NEG_INF = float("-inf")

RMS_EPS = 1e-6
GLA_HEADS = 8
GLA_DK = 256
GLA_DV = 512
GLA_RANK = 16
GLA_TAU = 16.0
GLA_SUB = 8
GLA_STEP_HEADS = 2
GLA_STEP_CHUNKS = 2
_LOG2_E = 1.4426950408889634
SWA_HEADS = 64
SWA_KV_HEADS = 8
SWA_HD = 64
SWA_BLOCK = 128
SWA_STEP_PAIRS = 2
MEM_HEADS = 4
PEER_HEADS = 8
PEER_NKEYS = 128
PEER_DQ = 256
PEER_TOPK = 16

LANES = 128
V7X_VMEM_BYTES = 64 * 1024 * 1024


def _tile(n, pref):
    t = min(n, pref)
    while n % t:
        t -= 1
    return t


def _params(semantics, vmem_bytes):
    limit = min(int(vmem_bytes) + (8 << 20), V7X_VMEM_BYTES - (4 << 20))
    return pltpu.CompilerParams(dimension_semantics=semantics, vmem_limit_bytes=limit)


def _nt(a, b):
    return lax.dot_general(a, b, (((1,), (1,)), ((), ())), preferred_element_type=_F32)


def _log2(n):
    assert n > 0 and n & (n - 1) == 0, n
    return n.bit_length() - 1


def _tn(a, b):
    return lax.dot_general(a, b, (((0,), (0,)), ((), ())), preferred_element_type=_F32)


def _rmsnorm_kernel(x_ref, g_ref, o_ref):
    x = x_ref[...].astype(_F32)
    ms = jnp.mean(x * x, axis=-1, keepdims=True)
    o_ref[...] = (x * lax.rsqrt(ms + RMS_EPS) * g_ref[...]).astype(o_ref.dtype)


def _rmsnorm(x, gain, out_dtype, rows=256):
    r, d = x.shape
    tr = _tile(r, rows)
    blk = tr * d * (x.dtype.itemsize + jnp.dtype(out_dtype).itemsize)
    return pl.pallas_call(
        _rmsnorm_kernel,
        grid=(r // tr,),
        in_specs=[pl.BlockSpec((tr, d), lambda i: (i, 0)), pl.BlockSpec((1, d), lambda i: (0, 0))],
        out_specs=pl.BlockSpec((tr, d), lambda i: (i, 0)),
        out_shape=jax.ShapeDtypeStruct((r, d), out_dtype),
        compiler_params=_params(("parallel",), 2 * blk + 4 * tr * d * 4),
        name="rmsnorm",
    )(x, gain.reshape(1, d).astype(_F32))


def _norm_proj_kernel(x_ref, g_ref, w_ref, xn_ref, o_ref):
    x = x_ref[...]
    ms = jnp.mean(x * x, axis=-1, keepdims=True)
    xn = (x * lax.rsqrt(ms + RMS_EPS) * g_ref[...]).astype(_BF16)
    xn_ref[...] = xn
    o_ref[...] = _nt(xn, w_ref[...].astype(_BF16)).astype(o_ref.dtype)


def _norm_proj(x, gain, wt, layer, col0, rows=512):
    r, d = x.shape
    tr = _tile(r, rows)
    assert col0 % LANES == 0
    return pl.pallas_call(
        _norm_proj_kernel,
        grid=(r // tr,),
        in_specs=[pl.BlockSpec((tr, d), lambda i: (i, 0)),
                  pl.BlockSpec((1, d), lambda i: (0, 0)),
                  pl.BlockSpec((None, LANES, d), lambda i: (layer, col0 // LANES, 0))],
        out_specs=[pl.BlockSpec((tr, d), lambda i: (i, 0)), pl.BlockSpec((tr, LANES), lambda i: (i, 0))],
        out_shape=[jax.ShapeDtypeStruct((r, d), _BF16), jax.ShapeDtypeStruct((r, LANES), _BF16)],
        compiler_params=_params(("parallel",), 2 * tr * d * 6 + 4 * tr * d * 4 + 3 * LANES * d * 4),
        name="norm_proj_glr",
    )(x, gain.reshape(1, d).astype(_F32), wt)


class _Side:
    def __init__(self, kernel, parts, args, in_specs, out_specs, out_shape, scratch_shapes=(), vmem=0):
        self.kernel, self.parts, self.args, self.in_specs = kernel, parts, list(args), list(in_specs)
        self.out_specs, self.out_shape = list(out_specs), list(out_shape)
        self.scratch_shapes, self.vmem = list(scratch_shapes), vmem


def _cast_kernel(step, parts, *refs):
    parts[0]()
    n = len(refs) // 2
    for src, dst in zip(refs[:n], refs[n:]):
        dst[...] = src[...].astype(dst.dtype)


def _cast_side(arrays, steps, ni):
    in_specs, out_specs, out_shape = [], [], []
    for a in arrays:
        r, c = a.shape
        rows = next(b for b in range(16, r + 1, 16) if r % b == 0 and r // b <= steps)
        index = lambda j, i, nblk=r // rows: (jnp.minimum(j * ni + i, nblk - 1), 0)
        in_specs.append(pl.BlockSpec((rows, c), index))
        out_specs.append(pl.BlockSpec((rows, c), index))
        out_shape.append(jax.ShapeDtypeStruct((r, c), _BF16))
    vmem = sum(2 * 6 * s.block_shape[0] * s.block_shape[1] for s in in_specs)
    return _Side(_cast_kernel, 1, list(arrays), in_specs, out_specs, out_shape, vmem=vmem), steps


def _mm_w32_kernel(*refs, shift, has_residual, side, transposed):
    n_in = 2 + bool(shift) + has_residual
    n_side_in = len(side.in_specs) if side else 0
    n_side_out = len(side.out_specs) if side else 0
    x_ref, w_ref = refs[0], refs[1]
    o_ref = refs[n_in + n_side_in]
    wbf_ref = refs[n_in + n_side_in + 1 + n_side_out]
    tn = o_ref.shape[1]

    @pl.when(pl.program_id(1) == 0)
    def _():
        rows = _tile(w_ref.shape[0], 128 if transposed else 512)
        for r0 in range(0, w_ref.shape[0], rows):
            rs = slice(r0, r0 + rows)
            if transposed:
                if r0 + rows + shift <= tn:
                    w = w_ref[r0 + shift:r0 + rows + shift, :]
                else:
                    w = jnp.concatenate([w_ref[r0 + shift:, :], refs[2][:shift, :]], axis=0)
                wbf_ref[:, rs] = w.T.astype(_BF16)
            else:
                w = w_ref[rs, :]
                if shift:
                    w = jnp.concatenate([w, refs[2][rs, :]], axis=1)
                    w = pltpu.roll(w, w.shape[1] - shift, axis=1)[:, :tn]
                wbf_ref[rs, :] = w.astype(_BF16)

    def matmul_rows(rs):
        acc = jnp.dot(x_ref[rs, :], wbf_ref[...], preferred_element_type=_F32)
        if has_residual:
            acc = acc + refs[n_in - 1][rs, :].astype(_F32)
        o_ref[rs, :] = acc.astype(o_ref.dtype)

    if side:
        rows = x_ref.shape[0] // side.parts
        parts = [functools.partial(matmul_rows, slice(r * rows, (r + 1) * rows)) for r in range(side.parts)]
        step = pl.program_id(0) * pl.num_programs(1) + pl.program_id(1)
        side.kernel(step, parts, *refs[n_in:n_in + n_side_in],
                    *refs[n_in + n_side_in + 1:n_in + n_side_in + 1 + n_side_out],
                    *refs[n_in + n_side_in + 2 + n_side_out:])
    else:
        matmul_rows(slice(None))


def _matmul_w32(x, w, layer, windows, out_dtype, tm=1024, tn=512, residual=None, side=None,
                transposed=False, weight_buffers=2, name="matmul_w32"):
    m, k = x.shape
    shift = windows[0][0] % LANES
    assert all(c0 % LANES == shift for c0, _ in windows)
    tm = _tile(m, tm)
    ok = lambda t: all(n % t == 0 and (c0 - shift) % t == 0 for c0, n in windows)
    tn = max(t for t in range(LANES, tn + 1, LANES) if ok(t))
    n = sum(n for _, n in windows)

    def col_block(j, width):
        blk, start = 0, 0
        for c0, wn in windows:
            first = ((c0 - shift) // tn - start) * (tn // width)
            blk = jnp.where(j >= start, first + j * (tn // width), blk)
            start += wn // tn
        return blk

    osz = jnp.dtype(out_dtype).itemsize

    def wspec(width, block_of):
        mode = {} if weight_buffers == 2 else dict(pipeline_mode=pl.Buffered(weight_buffers))
        if transposed:
            return pl.BlockSpec((None, width, k), lambda j, i: (layer, block_of(j), 0), **mode)
        return pl.BlockSpec((None, k, width), lambda j, i: (layer, 0, block_of(j)), **mode)

    in_specs = [pl.BlockSpec((tm, k), lambda j, i: (i, 0)), wspec(tn, lambda j: col_block(j, tn))]
    args = [x, w]
    vmem = 2 * (tm * k * 2 + tm * tn * osz) + weight_buffers * k * tn * 4 + k * tn * 2 + tm * tn * 4
    if shift:
        in_specs.append(wspec(LANES, lambda j: col_block(j, LANES) + tn // LANES))
        args.append(w)
        vmem += weight_buffers * k * LANES * 4
    if residual is not None:
        in_specs.append(pl.BlockSpec((tm, tn), lambda j, i: (i, j)))
        args.append(residual)
        vmem += 2 * tm * tn * residual.dtype.itemsize
    out_specs = [pl.BlockSpec((tm, tn), lambda j, i: (i, j))]
    out_shape = [jax.ShapeDtypeStruct((m, n), out_dtype)]
    scratch = [pltpu.VMEM((k, tn), _BF16)]
    alone = None
    if callable(side):
        side = side((n // tn) * (m // tm), m // tm)
    if side:
        side, steps = side
        if steps != (n // tn) * (m // tm):
            alone, side = _run_side(side, steps, m // tm, name + "_side"), None
    if side:
        in_specs += side.in_specs
        args += side.args
        out_specs += side.out_specs
        out_shape += side.out_shape
        scratch += side.scratch_shapes
        vmem += side.vmem
    out = pl.pallas_call(
        functools.partial(_mm_w32_kernel, shift=shift, has_residual=residual is not None, side=side,
                          transposed=transposed),
        grid=(n // tn, m // tm),
        in_specs=in_specs,
        out_specs=out_specs,
        out_shape=out_shape,
        scratch_shapes=scratch,
        compiler_params=_params(("arbitrary", "arbitrary"), vmem),
        name=name,
    )(*args)
    if alone is not None:
        return [out[0], *alone]
    return out if side else out[0]


def _log_sigmoid(z):
    return jnp.minimum(z, 0.0) - jnp.log(1.0 + jnp.exp(-jnp.abs(z)))


def _cumsum_rows(x):
    c, n = x.shape
    row = lax.broadcasted_iota(jnp.int32, (c, c), 0)
    col = lax.broadcasted_iota(jnp.int32, (c, c), 1)
    tri = jnp.where(row >= col, 1.0, 0.0).astype(_BF16)
    hi = x.astype(_BF16)
    rest = x - hi.astype(_F32)
    mid = rest.astype(_BF16)
    lo = (rest - mid.astype(_F32)).astype(_BF16)
    parts = jnp.dot(tri, jnp.concatenate([hi, mid, lo], axis=1), preferred_element_type=_F32)
    return parts[:, :n] + parts[:, n:2 * n] + parts[:, 2 * n:]


def _gla_chunk(q, k, v, b, state, midway):
    c = q.shape[0]
    row = lax.broadcasted_iota(jnp.int32, (c, c), 0)
    col = lax.broadcasted_iota(jnp.int32, (c, c), 1)
    b_last = b[c - 1:c, :]

    attn = jnp.zeros((c, c), _F32)
    row1 = lax.broadcasted_iota(jnp.int32, (c, 1), 0)
    s = c // 2
    while s >= GLA_SUB:
        pieces = []
        for lo in range(0, c, 2 * s):
            mid = lo + s
            ref_row = b[mid - 1:mid, :]
            pieces.append(ref_row - b[lo:mid, :])
            pieces.append(b[mid:mid + s, :] - ref_row)
        x = jnp.exp2(jnp.concatenate(pieces, axis=0))
        upper = ((row1 >> _log2(s)) & 1) == 1
        qt = jnp.where(upper, q * x, 0.0).astype(_BF16)
        kt = jnp.where(upper, 0.0, k * x).astype(_BF16)
        same = (row >> _log2(2 * s)) == (col >> _log2(2 * s))
        attn = attn + jnp.where(same, _nt(qt, kt), 0.0)
        s //= 2

    midway()
    sub_row = lax.broadcasted_iota(jnp.int32, (GLA_SUB, 1), 0)
    lane = lax.broadcasted_iota(jnp.int32, (GLA_SUB, c), 1)
    diag = []
    for i0 in range(0, c, GLA_SUB):
        qi, ki, bi = q[i0:i0 + GLA_SUB, :], k[i0:i0 + GLA_SUB, :], b[i0:i0 + GLA_SUB, :]
        blk = jnp.zeros((GLA_SUB, c), _F32)
        for j in range(GLA_SUB):
            dec = jnp.exp2(jnp.where(sub_row >= j, bi - bi[j:j + 1, :], _NEG_INF))
            sc = jnp.sum(qi * dec * ki[j:j + 1, :], axis=1, keepdims=True)
            blk = jnp.where(lane == i0 + j, sc, blk)
        diag.append(blk)
    attn = attn + jnp.concatenate(diag, axis=0)

    qb = (q * jnp.exp2(b)).astype(_BF16)
    o = _nt(qb, state.astype(_BF16)) + jnp.dot(attn.astype(_BF16), v, preferred_element_type=_F32)
    kd = (k * jnp.exp2(b_last - b)).astype(_BF16)
    return o, state * jnp.exp2(b_last) + _tn(v, kd)


def _gla_kernel(step, parts, q_ref, k_ref, v_ref, r_ref, glr_ref, wg_ref, bias_ref, gain_ref, o_ref,
                state_ref, *, nc):
    @pl.when(step % nc == 0)
    def _():
        state_ref[...] = jnp.zeros_like(state_ref)

    c = q_ref.shape[0] // GLA_STEP_CHUNKS
    z = jnp.dot(glr_ref[...], wg_ref[...], preferred_element_type=_F32) + bias_ref[...]
    log_a = _log_sigmoid(z) * (_LOG2_E / GLA_TAU)
    b_all = [_cumsum_rows(log_a[s * c:(s + 1) * c, :]) for s in range(GLA_STEP_CHUNKS)]
    states = [state_ref[h] for h in range(GLA_STEP_HEADS)]
    parts = iter(parts)
    for s in range(GLA_STEP_CHUNKS):
        rows = slice(s * c, (s + 1) * c)
        for h in range(GLA_STEP_HEADS):
            next(parts)()
            ks = slice(h * GLA_DK, (h + 1) * GLA_DK)
            vs = slice(h * GLA_DV, (h + 1) * GLA_DV)
            q = q_ref[rows, ks].astype(_F32) * (GLA_DK ** -0.5)
            o, states[h] = _gla_chunk(q, k_ref[rows, ks].astype(_F32), v_ref[rows, vs], b_all[s][:, ks],
                                      states[h], next(parts))
            ms = jnp.mean(o * o, axis=-1, keepdims=True)
            on = o * lax.rsqrt(ms + RMS_EPS) * gain_ref[:, vs]
            r = r_ref[rows, vs].astype(_F32)
            o_ref[rows, vs] = (on * (r * jax.nn.sigmoid(r))).astype(o_ref.dtype)
    for h in range(GLA_STEP_HEADS):
        state_ref[h] = states[h]


def _run_side(side, steps, ni, name):
    def kern(*refs):
        side.kernel(pl.program_id(0) * pl.num_programs(1) + pl.program_id(1),
                    [lambda: None] * side.parts, *refs)

    return pl.pallas_call(
        kern, grid=(steps // ni, ni), in_specs=side.in_specs, out_specs=side.out_specs,
        out_shape=side.out_shape, scratch_shapes=side.scratch_shapes,
        compiler_params=_params(("arbitrary", "arbitrary"), side.vmem), name=name)(*side.args)


def _gla_side(qkvr, glr, wg, bias, gain, batch, seq, ni, chunk=128):
    t = batch * seq
    chunk = chunk * GLA_STEP_CHUNKS
    nc = seq // chunk
    hs = GLA_STEP_HEADS
    wk, wv = hs * GLA_DK, hs * GLA_DV
    groups = GLA_HEADS // hs
    kv = 2 * groups * wk // wv

    def at(col):
        def index(j, i):
            s = j * ni + i
            return (s // (groups * nc)) * nc + s % nc, col + (s // nc) % groups
        return index

    head = lambda j, i: (0, ((j * ni + i) // nc) % groups)
    in_specs = [
        pl.BlockSpec((chunk, wk), at(0)),
        pl.BlockSpec((chunk, wk), at(groups)),
        pl.BlockSpec((chunk, wv), at(kv)),
        pl.BlockSpec((chunk, wv), at(kv + groups)),
        pl.BlockSpec((chunk, LANES), lambda j, i: (at(0)(j, i)[0], 0)),
        pl.BlockSpec((LANES, wk), head),
        pl.BlockSpec((1, wk), head),
        pl.BlockSpec((1, wv), head),
    ]
    side = _Side(functools.partial(_gla_kernel, nc=nc), 2 * hs * GLA_STEP_CHUNKS,
                 [qkvr, qkvr, qkvr, qkvr, glr, wg, bias, gain], in_specs,
                 [pl.BlockSpec((chunk, wv), at(0))], [jax.ShapeDtypeStruct((t, GLA_HEADS * GLA_DV), _BF16)],
                 [pltpu.VMEM((hs, GLA_DV, GLA_DK), _F32)], vmem=16 << 20)
    return side, batch * groups * nc


def _pair_rmsnorm(x, gain2):
    lane = lax.broadcasted_iota(jnp.int32, x.shape, 1)
    low = lane < SWA_HD
    sq = x * x
    s_all = jnp.sum(sq, axis=1, keepdims=True)
    s_low = jnp.sum(jnp.where(low, sq, 0.0), axis=1, keepdims=True)
    ms = jnp.where(low, s_low, s_all - s_low) * (1.0 / SWA_HD)
    return x * lax.rsqrt(ms + RMS_EPS) * gain2


def _swa_kernel(step, parts, sinks_ref, q_ref, kc_ref, kp_ref, vc_ref, vp_ref, qg_ref, kg_ref, o_ref, *, nb, spp):
    pair_groups = SWA_KV_HEADS // 2 // spp
    blk = (step // pair_groups) % nb
    bs = SWA_BLOCK
    group = SWA_HEADS // SWA_KV_HEADS
    pairs = group // 2
    pw = 2 * group * SWA_HD
    lane = lax.broadcasted_iota(jnp.int32, (bs, LANES), 1)
    low = lane < SWA_HD

    q_loc = lax.broadcasted_iota(jnp.int32, (group * bs, bs), 0) & (bs - 1)
    from_cur = lax.broadcasted_iota(jnp.int32, (group * bs, bs), 1) <= q_loc
    prev_bias = jnp.where(blk > 0, 0.0, _NEG_INF)

    for pp in range(spp):
        p = (step % pair_groups) * spp + pp
        ls = slice(pp * LANES, (pp + 1) * LANES)
        tiles = [_pair_rmsnorm(kc_ref[:, ls].astype(_F32), kg_ref[...]),
                 _pair_rmsnorm(kp_ref[:, ls].astype(_F32), kg_ref[...]),
                 vc_ref[:, ls].astype(_F32), vp_ref[:, ls].astype(_F32)]
        swapped = [pltpu.roll(x, SWA_HD, axis=1) for x in tiles]
        for half in range(2):
            kc, kp, vc, vp = [(jnp.where(low, x, xs) if half == 0 else jnp.where(low, xs, x)).astype(_BF16)
                              for x, xs in zip(tiles, swapped)]
            qs, sink = [], []
            for t in range(pairs):
                c0 = pp * pw + half * group * SWA_HD + t * LANES
                qn = _pair_rmsnorm(q_ref[:, c0:c0 + LANES].astype(_F32), qg_ref[...]) * (SWA_HD ** -0.5)
                qs.append(jnp.where(low, qn, 0.0))
                qs.append(jnp.where(low, 0.0, qn))
                head = (2 * p + half) * group + 2 * t
                sink.append(jnp.full((bs, 1), sinks_ref[head], _F32))
                sink.append(jnp.full((bs, 1), sinks_ref[head + 1], _F32))
            qstack = jnp.concatenate(qs, axis=0).astype(_BF16)
            sink = jnp.concatenate(sink, axis=0)
            s = jnp.where(from_cur, _nt(qstack, kc), _nt(qstack, kp) + prev_bias)
            m = jnp.maximum(jnp.max(s, axis=1, keepdims=True), sink)
            e = jnp.exp(s - m)
            denom = jnp.sum(e, axis=1, keepdims=True) + jnp.exp(sink - m)
            e_cur = jnp.where(from_cur, e, 0.0).astype(_BF16)
            e_prev = jnp.where(from_cur, 0.0, e).astype(_BF16)
            o = (jnp.dot(e_cur, vc, preferred_element_type=_F32)
                 + jnp.dot(e_prev, vp, preferred_element_type=_F32)) / denom
            for t in range(pairs):
                c0 = pp * pw + half * group * SWA_HD + t * LANES
                oa = o[(2 * t) * bs:(2 * t + 1) * bs, :]
                ob = o[(2 * t + 1) * bs:(2 * t + 2) * bs, :]
                o_ref[:, c0:c0 + LANES] = jnp.where(low, oa, ob).astype(o_ref.dtype)
            parts[2 * pp + half]()


def _swa_side(sq, q_col, skv, k_col, sinks, q_gain, k_gain, batch, seq, ni):
    t = batch * seq
    nb = seq // SWA_BLOCK
    npairs = SWA_KV_HEADS // 2
    spp = min(SWA_STEP_PAIRS, npairs)
    pair_groups = npairs // spp
    kvw = SWA_KV_HEADS * SWA_HD
    qw, kw = spp * 2 * (SWA_HEADS // SWA_KV_HEADS) * SWA_HD, spp * LANES
    assert q_col % qw == 0 and k_col % kw == 0 and kvw % kw == 0

    def at(col, prev=False):
        def index(j, i):
            s = j * ni + i
            blk = s // pair_groups
            if prev:
                blk = blk - jnp.where(blk % nb > 0, 1, 0)
            return blk, col + s % pair_groups
        return index

    const = lambda j, i: (0, 0)
    kc, vc = k_col // kw, (k_col + kvw) // kw
    in_specs = [
        pl.BlockSpec(memory_space=pltpu.SMEM),
        pl.BlockSpec((SWA_BLOCK, qw), at(q_col // qw)),
        pl.BlockSpec((SWA_BLOCK, kw), at(kc)),
        pl.BlockSpec((SWA_BLOCK, kw), at(kc, prev=True)),
        pl.BlockSpec((SWA_BLOCK, kw), at(vc)),
        pl.BlockSpec((SWA_BLOCK, kw), at(vc, prev=True)),
        pl.BlockSpec((1, LANES), const),
        pl.BlockSpec((1, LANES), const),
    ]
    g2 = lambda g: jnp.concatenate([g, g]).reshape(1, LANES).astype(_F32)
    side = _Side(functools.partial(_swa_kernel, nb=nb, spp=spp), 2 * spp,
                 [sinks.astype(_F32), sq, skv, skv, skv, skv, g2(q_gain), g2(k_gain)], in_specs,
                 [pl.BlockSpec((SWA_BLOCK, qw), at(0))],
                 [jax.ShapeDtypeStruct((t, SWA_HEADS * SWA_HD), _BF16)], vmem=16 << 20)
    return side, batch * nb * pair_groups


def _memattn_kernel(q_ref, k_ref, v_ref, g_ref, o_ref):
    hd = q_ref.shape[1]
    q = q_ref[...].astype(_F32)
    ms = jnp.mean(q * q, axis=-1, keepdims=True)
    qn = (q * lax.rsqrt(ms + RMS_EPS) * g_ref[...] * (hd ** -0.5)).astype(_BF16)
    s = _nt(qn, k_ref[...])
    m = jnp.max(s, axis=-1, keepdims=True)
    e = jnp.exp(s - m)
    denom = jnp.sum(e, axis=-1, keepdims=True)
    o = jnp.dot(e.astype(_BF16), v_ref[...], preferred_element_type=_F32) / denom
    o_ref[...] = o.astype(o_ref.dtype)


def _memattn(mq, kc, vm, q_gain, batch, seq, mem_len, d, q_blk=0, v_blk=0, tm=1024):
    t = mq.shape[0]
    tm = _tile(seq, tm)
    ns = seq // tm
    hd = d // MEM_HEADS
    return pl.pallas_call(
        _memattn_kernel,
        grid=(batch, ns, MEM_HEADS),
        in_specs=[
            pl.BlockSpec((tm, hd), lambda b, i, h: (b * ns + i, q_blk + h)),
            pl.BlockSpec((mem_len, hd), lambda b, i, h: (b, h)),
            pl.BlockSpec((mem_len, hd), lambda b, i, h: (b, v_blk + h)),
            pl.BlockSpec((1, hd), lambda b, i, h: (0, 0)),
        ],
        out_specs=pl.BlockSpec((tm, hd), lambda b, i, h: (b * ns + i, h)),
        out_shape=jax.ShapeDtypeStruct((t, d), _BF16),
        compiler_params=_params(("parallel", "parallel", "arbitrary"),
                                4 * tm * hd * 2 + 4 * mem_len * hd * 2 + 6 * tm * hd * 4),
        name="memattn",
    )(mq, kc, vm, q_gain.reshape(1, hd).astype(_F32))


def _merge_kernel(a_ref, b_ref, c_ref, wa_ref, wb_ref, wc_ref, ga_ref, gb_ref, gc_ref, *rest):
    def branch(x_ref, w_ref, g_ref):
        y = jnp.dot(x_ref[...], w_ref[...], preferred_element_type=_F32)
        return jax.nn.sigmoid(g_ref[...].astype(_F32)) * y

    ncast = (len(rest) - 1) // 2
    o_ref = rest[ncast]
    o_ref[...] = (branch(a_ref, wa_ref, ga_ref) + branch(b_ref, wb_ref, gb_ref)
                  + branch(c_ref, wc_ref, gc_ref)).astype(o_ref.dtype)
    _cast_kernel(None, [lambda: None], *rest[:ncast], *rest[ncast + 1:])


def _merge(oa, ob, oc, wa, wb, wc, gate, gate_col=0, cast=(), tm=512, tn=512):
    t, d = oa.shape
    n = wa.shape[1]
    tm, tn = _tile(t, tm), _tile(n, tn)
    nn = n // tn
    ni = t // tm
    if gate_col % tn:
        gate, gate_col = gate[:, gate_col:gate_col + 3 * n], 0
    g0 = gate_col // tn
    act = lambda o: pl.BlockSpec((tm, o.shape[1]), lambda j, i: (i, 0))
    wsp = lambda w: pl.BlockSpec((w.shape[0], tn), lambda j, i: (0, j), pipeline_mode=pl.Buffered(1))
    gsp = lambda b: pl.BlockSpec((tm, tn), lambda j, i: (i, g0 + b * nn + j))
    casts, _ = _cast_side(cast, nn * ni, ni)
    vmem = 3 * (2 * tm * d * 2 + d * tn * 2 + 2 * tm * tn * 2) + 8 * tm * tn * 4 + casts.vmem
    out = pl.pallas_call(
        _merge_kernel,
        grid=(nn, ni),
        in_specs=[act(oa), act(ob), act(oc), wsp(wa), wsp(wb), wsp(wc), gsp(0), gsp(1), gsp(2)] + casts.in_specs,
        out_specs=[pl.BlockSpec((tm, tn), lambda j, i: (i, j))] + casts.out_specs,
        out_shape=[jax.ShapeDtypeStruct((t, n), _BF16)] + casts.out_shape,
        compiler_params=_params(("arbitrary", "arbitrary"), vmem),
        name="merge",
    )(oa, ob, oc, wa, wb, wc, gate, gate, gate, *casts.args)
    return out


def _extract_topk(s, payload, kk):
    n = s.shape[0]
    pos = lax.broadcasted_iota(jnp.int32, s.shape, 0).astype(_F32)
    vals, sel = [], []
    for r in range(kk):
        m = jnp.max(s, axis=0, keepdims=True)
        first = jnp.min(jnp.where(s == m, pos, float(n)), axis=0, keepdims=True)
        hit = pos == first
        vals.append(m)
        sel.append(first if payload is None else jnp.max(jnp.where(hit, payload, -1.0), axis=0, keepdims=True))
        if r + 1 < kk:
            s = jnp.where(hit, _NEG_INF, s)
    return jnp.concatenate(vals, axis=0), jnp.concatenate(sel, axis=0)


def _route_kernel(h_ref, gain_ref, wq_ref, keys_ref, hn_ref, w_ref, q_ref, i1_ref, i2_ref, g_ref):
    kk = PEER_TOPK
    half = PEER_DQ // 2
    nk = PEER_NKEYS
    step = pl.program_id(0)
    cur, prev = step % 2, (step + 1) % 2
    tokens = h_ref.shape[0] // PEER_HEADS

    @pl.when(step == 0)
    def _():
        q_ref[1] = jnp.zeros(q_ref.shape[1:], q_ref.dtype)
        i1_ref[...] = jnp.zeros_like(i1_ref)
        i2_ref[...] = jnp.zeros_like(i2_ref)
        g_ref[...] = jnp.zeros_like(g_ref)

    x = h_ref[...]
    ms = jnp.mean(x * x, axis=-1, keepdims=True)
    hn = (x * lax.rsqrt(ms + RMS_EPS) * gain_ref[...]).astype(_BF16)
    hn_ref[...] = hn

    sub = lax.broadcasted_iota(jnp.int32, (nk, i1_ref.shape[2]), 0)

    def gate_rows(t0):
        for t in range(t0, t0 + tokens):
            a_t = jnp.where(sub == i1_ref[cur, t:t + 1, :], 1.0, 0.0).astype(_BF16)
            b_t = jnp.where(sub == i2_ref[cur, t:t + 1, :], g_ref[cur, t:t + 1, :], 0.0).astype(_BF16)
            w_ref[t] = _nt(a_t, b_t)

    experts, gates = [], []
    for h in range(PEER_HEADS):
        hs = slice(h * PEER_DQ, (h + 1) * PEER_DQ)
        q_ref[cur, :, hs] = jnp.dot(hn, wq_ref[:, hs], preferred_element_type=_F32)
        top = []
        for p in range(2):
            c0 = (2 * h + p) * half
            st = _nt(keys_ref[2 * h + p].astype(_BF16), q_ref[prev, :, c0:c0 + half].astype(_BF16))
            top.append(_extract_topk(st, None, kk))
        (s0, i0), (s1, i1) = top
        cand, ids = [], []
        tail = kk // 2
        for a in range(tail):
            nb = kk // (a + 1)
            rows = -(-nb // 8) * 8
            c = s0[a:a + 1, :] + s1[0:rows, :]
            if rows != nb:
                c = jnp.where(lax.broadcasted_iota(jnp.int32, c.shape, 0) < nb, c, _NEG_INF)
            cand.append(c)
            ids.append(i0[a:a + 1, :] * PEER_NKEYS + i1[0:rows, :])
        cand.append(s0[tail:kk, :] + s1[0:1, :])
        ids.append(i0[tail:kk, :] * PEER_NKEYS + i1[0:1, :])
        best, e = _extract_topk(jnp.concatenate(cand, axis=0), jnp.concatenate(ids, axis=0), kk)
        ex = jnp.exp(best - best[0:1, :])
        gates.append(ex / jnp.sum(ex, axis=0, keepdims=True))
        experts.append(e)
        gate_rows(h * tokens)
    e = jnp.concatenate(experts, axis=0).T.astype(jnp.int32)
    i1_ref[prev] = e >> _log2(PEER_NKEYS)
    i2_ref[prev] = e & (PEER_NKEYS - 1)
    g_ref[prev] = jnp.concatenate(gates, axis=0).T


def _route(h, gain, wq, sub_keys, tt=128):
    t, d = h.shape
    tt = _tile(t, tt)
    n = t // tt
    j = PEER_HEADS * PEER_TOPK
    nk = PEER_NKEYS
    nq = PEER_HEADS * PEER_DQ
    keys = sub_keys.reshape(PEER_HEADS * 2, nk, PEER_DQ // 2)
    tile = lambda s: (jnp.minimum(s, n - 1), 0)
    vmem = d * nq * 2 + 2 * (tt * nk * nk * 4 + tt * d * 6) + 2 * tt * nq * 4 + (12 << 20)
    return pl.pallas_call(
        _route_kernel,
        grid=(n + 2,),
        in_specs=[pl.BlockSpec((tt, d), tile),
                  pl.BlockSpec((1, d), lambda s: (0, 0)),
                  pl.BlockSpec((d, nq), lambda s: (0, 0), pipeline_mode=pl.Buffered(1)),
                  pl.BlockSpec(keys.shape, lambda s: (0, 0, 0))],
        out_specs=[pl.BlockSpec((tt, d), tile),
                   pl.BlockSpec((tt, nk, nk), lambda s: (jnp.maximum(s - 2, 0), 0, 0))],
        out_shape=[jax.ShapeDtypeStruct((t, d), _BF16), jax.ShapeDtypeStruct((t, nk, nk), _F32)],
        scratch_shapes=[pltpu.VMEM((2, tt, nq), _F32), pltpu.VMEM((2, tt, j), jnp.int32),
                        pltpu.VMEM((2, tt, j), jnp.int32), pltpu.VMEM((2, tt, j), _F32)],
        compiler_params=_params(("arbitrary",), vmem),
        name="peer_route",
    )(h, gain.reshape(1, d).astype(_F32), wq, keys)


def _gelu(x):
    return 0.5 * x * (1.0 + lax.erf(x * (2.0 ** -0.5)))


def _peer_act_kernel(x_ref, u_ref, w_ref, s_ref):
    act = _gelu(_nt(x_ref[...], u_ref[...]))
    w = jnp.transpose(w_ref[...], (1, 0, 2))
    for a in range(w.shape[0]):
        cs = slice(a * PEER_NKEYS, (a + 1) * PEER_NKEYS)
        s_ref[:, cs] = (act[:, cs] * w[a]).astype(s_ref.dtype)


def _peer_act(xn, u, w, tm=1024, rows_a=8):
    t, d = xn.shape
    ne = u.shape[0]
    nk = PEER_NKEYS
    tm = _tile(t, tm)
    eb = rows_a * nk
    vmem = 2 * (tm * d * 2 + tm * rows_a * nk * 4 + tm * eb * 2) + eb * d * 2 + 3 * tm * eb * 4
    return pl.pallas_call(
        _peer_act_kernel,
        grid=(ne // eb, t // tm),
        in_specs=[pl.BlockSpec((tm, d), lambda j, i: (i, 0)),
                  pl.BlockSpec((eb, d), lambda j, i: (j, 0), pipeline_mode=pl.Buffered(1)),
                  pl.BlockSpec((tm, rows_a, nk), lambda j, i: (i, j, 0))],
        out_specs=pl.BlockSpec((tm, eb), lambda j, i: (i, j)),
        out_shape=jax.ShapeDtypeStruct((t, ne), _BF16),
        compiler_params=_params(("parallel", "parallel"), vmem),
        name="peer_act",
    )(xn, u, w)


def _mm_acc_kernel(x_ref, w_ref, r_ref, o_ref):
    acc = jnp.dot(x_ref[...], w_ref[...], preferred_element_type=_F32)

    @pl.when(pl.program_id(2) == 0)
    def _():
        o_ref[...] = r_ref[...] + acc

    @pl.when(pl.program_id(2) > 0)
    def _():
        o_ref[...] += acc


def _matmul_acc(x, w, residual, tm=1024, tn=1024, tk=4096):
    m, k = x.shape
    n = w.shape[1]
    tm, tn, tk = _tile(m, tm), _tile(n, tn), _tile(k, tk)
    vmem = 2 * (tm * tk * 2 + tk * tn * 2 + 2 * tm * tn * 4) + tm * tn * 4
    return pl.pallas_call(
        _mm_acc_kernel,
        grid=(m // tm, n // tn, k // tk),
        in_specs=[pl.BlockSpec((tm, tk), lambda i, j, l: (i, l)),
                  pl.BlockSpec((tk, tn), lambda i, j, l: (l, j)),
                  pl.BlockSpec((tm, tn), lambda i, j, l: (i, j))],
        out_specs=pl.BlockSpec((tm, tn), lambda i, j, l: (i, j)),
        out_shape=jax.ShapeDtypeStruct((m, n), _F32),
        compiler_params=_params(("parallel", "parallel", "arbitrary"), vmem),
        name="peer_out",
    )(x, w, residual)


def _layer(h, mem, norm_mix, w_in, gla_w_gate_up, gla_gate_bias, gla_out_norm, swa_q_norm, swa_k_norm,
           swa_sinks, mem_norm, w_mem_kv, mem_q_norm, mem_k_norm, w_branch_gla, w_branch_swa,
           w_branch_mem, w_out, norm_ffn, peer_w_q, peer_sub_keys, peer_u, peer_v, batch, seq, l):
    t, d = h.shape
    mem_len = mem.shape[0] // batch
    qk = GLA_HEADS * GLA_DK
    gv = GLA_HEADS * GLA_DV
    swq = SWA_HEADS * SWA_HD
    skv = SWA_KV_HEADS * SWA_HD
    o_glr = 2 * qk + 2 * gv
    o_sq = o_glr + GLA_RANK
    o_sk = o_sq + swq
    o_mq = o_sk + 2 * skv
    o_gate = o_mq + d
    bf = lambda w: w.astype(_BF16)
    hd = d // MEM_HEADS

    w_in_t = jnp.swapaxes(w_in, 1, 2)
    xn, glr = _norm_proj(h, norm_mix, w_in_t, l, o_glr)
    proj = functools.partial(_matmul_w32, xn, w_in_t, l, transposed=True)
    branch_w = [w_branch_gla, w_branch_swa, w_branch_mem]
    qkvr, *branch_w = proj([(0, o_glr)], _BF16, side=functools.partial(_cast_side, branch_w), name="proj_gla")
    skvp = proj([(o_sk, 2 * skv)], _BF16, name="proj_swa_kv")

    tm_gla = _tile(t, 512 * GLA_STEP_CHUNKS)
    tm_swa = _tile(t, 512 * min(SWA_STEP_PAIRS, SWA_KV_HEADS // 2))
    wg = bf(jnp.pad(gla_w_gate_up, ((0, LANES - GLA_RANK), (0, 0))))
    gla = _gla_side(qkvr, glr, wg, gla_gate_bias.reshape(1, qk).astype(_F32),
                    gla_out_norm.reshape(1, gv).astype(_F32), batch, seq, t // tm_gla)
    sqmq, o_a = proj([(o_sq, swq), (o_mq, d)], _BF16, tm=tm_gla, side=gla, name="proj_q_gla")
    swa = _swa_side(sqmq, 0, skvp, 0, swa_sinks, swa_q_norm, swa_k_norm, batch, seq, t // tm_swa)
    gate, o_b = proj([(o_gate, 3 * d)], _BF16, tm=tm_swa, tn=768, side=swa, weight_buffers=1,
                     name="proj_gate_swa")

    memn = _rmsnorm(mem, mem_norm, _BF16)
    kv_m = _matmul_w32(memn, w_mem_kv, l, [(0, 2 * d)], _BF16, name="mem_kv")
    k_c = _rmsnorm(kv_m[:, :d].reshape(-1, hd), mem_k_norm, _BF16).reshape(-1, d)
    mq, mq_blk = (sqmq, swq // hd) if swq % hd == 0 else (sqmq[:, swq:], 0)
    o_c = _memattn(mq, k_c, kv_m, mem_q_norm, batch, seq, mem_len, d, mq_blk, MEM_HEADS)

    mix, wq, u, v = _merge(o_a, o_b, o_c, *branch_w, gate, cast=[peer_w_q[l], peer_u, peer_v])
    h = _matmul_w32(mix, w_out, l, [(0, d)], _F32, residual=h, name="out_proj")

    hn, w = _route(h, norm_ffn, wq, peer_sub_keys.astype(_F32))
    s = _peer_act(hn, u, w)
    return _matmul_acc(s, v, h)


def kernel(x, mem, norm_mix, w_in, gla_w_gate_up, gla_gate_bias, gla_out_norm, swa_q_norm, swa_k_norm,
           swa_sinks, mem_norm, w_mem_kv, mem_q_norm, mem_k_norm, w_branch_gla, w_branch_swa, w_branch_mem,
           w_out, norm_ffn, peer_w_q, peer_sub_keys, peer_u, peer_v):
    batch, seq, d = x.shape
    h = x.reshape(batch * seq, d)
    memf = mem.reshape(-1, d)
    for l in range(norm_mix.shape[0]):
        h = _layer(h, memf, norm_mix[l], w_in, gla_w_gate_up[l], gla_gate_bias[l], gla_out_norm[l],
                   swa_q_norm[l], swa_k_norm[l], swa_sinks[l], mem_norm[l], w_mem_kv, mem_q_norm[l],
                   mem_k_norm[l], w_branch_gla[l], w_branch_swa[l], w_branch_mem[l], w_out, norm_ffn[l],
                   peer_w_q, peer_sub_keys[l], peer_u[l], peer_v[l], batch, seq, l)
    return h.reshape(batch, seq, d)
```

```python
import functools

import jax
import jax.numpy as jnp
from jax import lax
from jax.experimental import pallas as pl
from jax.experimental.pallas import tpu as pltpu

_F32 = jnp.float32
_BF16 = jnp.bfloat16
_NEG_INF = float("-inf")

RMS_EPS = 1e-6
GLA_HEADS = 8
GLA_DK = 256
GLA_DV = 512
GLA_RANK = 16
GLA_TAU = 16.0
GLA_SUB = 8
GLA_STEP_HEADS = 2
GLA_STEP_CHUNKS = 2
_LOG2_E = 1.4426950408889634
SWA_HEADS = 64
SWA_KV_HEADS = 8
SWA_HD = 64
SWA_BLOCK = 128
SWA_STEP_PAIRS = 2
MEM_HEADS = 4
PEER_HEADS = 8
PEER_NKEYS = 128
PEER_DQ = 256
PEER_TOPK = 16

LANES = 128
V7X_VMEM_BYTES = 64 * 1024 * 1024


def _tile(n, pref):
    t = min(n, pref)
    while n % t:
        t -= 1
    return t


def _params(semantics, vmem_bytes):
    limit = min(int(vmem_bytes) + (8 << 20), V7X_VMEM_BYTES - (4 << 20))
    return pltpu.CompilerParams(dimension_semantics=semantics, vmem_limit_bytes=limit)


def _nt(a, b):
    return lax.dot_general(a, b, (((1,), (1,)), ((), ())), preferred_element_type=_F32)


def _log2(n):
    assert n > 0 and n & (n - 1) == 0, n
    return n.bit_length() - 1


def _tn(a, b):
    return lax.dot_general(a, b, (((0,), (0,)), ((), ())), preferred_element_type=_F32)


def _rmsnorm_kernel(x_ref, g_ref, o_ref):
    x = x_ref[...].astype(_F32)
    ms = jnp.mean(x * x, axis=-1, keepdims=True)
    o_ref[...] = (x * lax.rsqrt(ms + RMS_EPS) * g_ref[...]).astype(o_ref.dtype)


def _rmsnorm(x, gain, out_dtype, rows=256):
    r, d = x.shape
    tr = _tile(r, rows)
    blk = tr * d * (x.dtype.itemsize + jnp.dtype(out_dtype).itemsize)
    return pl.pallas_call(
        _rmsnorm_kernel,
        grid=(r // tr,),
        in_specs=[pl.BlockSpec((tr, d), lambda i: (i, 0)), pl.BlockSpec((1, d), lambda i: (0, 0))],
        out_specs=pl.BlockSpec((tr, d), lambda i: (i, 0)),
        out_shape=jax.ShapeDtypeStruct((r, d), out_dtype),
        compiler_params=_params(("parallel",), 2 * blk + 4 * tr * d * 4),
        name="rmsnorm",
    )(x, gain.reshape(1, d).astype(_F32))


def _norm_proj_kernel(x_ref, g_ref, wg_ref, wk_ref, wk2_ref, xn_ref, glr_ref, skv_ref, wgb_ref, wkb_ref, *, shift):
    @pl.when(pl.program_id(0) == 0)
    def _():
        wgb_ref[...] = wg_ref[...].astype(_BF16)
        n = wkb_ref.shape[0]
        for r0 in range(0, n, LANES):
            if r0 + LANES + shift <= n:
                w = wk_ref[r0 + shift:r0 + LANES + shift, :]
            else:
                w = jnp.concatenate([wk_ref[r0 + shift:, :], wk2_ref[:shift, :]], axis=0)
            wkb_ref[r0:r0 + LANES, :] = w.astype(_BF16)

    x = x_ref[...]
    ms = jnp.mean(x * x, axis=-1, keepdims=True)
    xn = (x * lax.rsqrt(ms + RMS_EPS) * g_ref[...]).astype(_BF16)
    xn_ref[...] = xn
    glr_ref[...] = _nt(xn, wgb_ref[...]).astype(glr_ref.dtype)
    skv_ref[...] = _nt(xn, wkb_ref[...]).astype(skv_ref.dtype)


def _norm_proj(x, gain, wt, layer, col_a, col_b, n_b, rows=512):
    r, d = x.shape
    tr = _tile(r, rows)
    shift = col_b % LANES
    base = col_b - shift
    assert col_a % LANES == 0 and base % n_b == 0 and n_b % LANES == 0
    once = dict(pipeline_mode=pl.Buffered(1))
    return pl.pallas_call(
        functools.partial(_norm_proj_kernel, shift=shift),
        grid=(r // tr,),
        in_specs=[pl.BlockSpec((tr, d), lambda i: (i, 0)),
                  pl.BlockSpec((1, d), lambda i: (0, 0)),
                  pl.BlockSpec((None, LANES, d), lambda i: (layer, col_a // LANES, 0), **once),
                  pl.BlockSpec((None, n_b, d), lambda i: (layer, base // n_b, 0), **once),
                  pl.BlockSpec((None, LANES, d), lambda i: (layer, (base + n_b) // LANES, 0), **once)],
        out_specs=[pl.BlockSpec((tr, d), lambda i: (i, 0)), pl.BlockSpec((tr, LANES), lambda i: (i, 0)),
                   pl.BlockSpec((tr, n_b), lambda i: (i, 0))],
        out_shape=[jax.ShapeDtypeStruct((r, d), _BF16), jax.ShapeDtypeStruct((r, LANES), _BF16),
                   jax.ShapeDtypeStruct((r, n_b), _BF16)],
        scratch_shapes=[pltpu.VMEM((LANES, d), _BF16), pltpu.VMEM((n_b, d), _BF16)],
        compiler_params=_params(("arbitrary",), 2 * tr * d * 6 + 4 * tr * d * 4 + (n_b + 2 * LANES) * d * 6),
        name="norm_proj",
    )(x, gain.reshape(1, d).astype(_F32), wt, wt, wt)


class _Side:
    def __init__(self, kernel, parts, args, in_specs, out_specs, out_shape, scratch_shapes=(), vmem=0):
        self.kernel, self.parts, self.args, self.in_specs = kernel, parts, list(args), list(in_specs)
        self.out_specs, self.out_shape = list(out_specs), list(out_shape)
        self.scratch_shapes, self.vmem = list(scratch_shapes), vmem


def _cast_kernel(step, parts, *refs):
    parts[0]()
    n = len(refs) // 2
    for src, dst in zip(refs[:n], refs[n:]):
        dst[...] = src[...].astype(dst.dtype)


def _cast_side(arrays, steps, ni):
    in_specs, out_specs, out_shape = [], [], []
    for a in arrays:
        r, c = a.shape
        rows = next(b for b in range(16, r + 1, 16) if r % b == 0 and r // b <= steps)
        index = lambda j, i, nblk=r // rows: (jnp.minimum(j * ni + i, nblk - 1), 0)
        in_specs.append(pl.BlockSpec((rows, c), index))
        out_specs.append(pl.BlockSpec((rows, c), index))
        out_shape.append(jax.ShapeDtypeStruct((r, c), _BF16))
    vmem = sum(2 * 6 * s.block_shape[0] * s.block_shape[1] for s in in_specs)
    return _Side(_cast_kernel, 1, list(arrays), in_specs, out_specs, out_shape, vmem=vmem), steps


def _mm_w32_kernel(*refs, shift, has_residual, side, transposed):
    n_in = 2 + bool(shift) + has_residual
    n_side_in = len(side.in_specs) if side else 0
    n_side_out = len(side.out_specs) if side else 0
    x_ref, w_ref = refs[0], refs[1]
    o_ref = refs[n_in + n_side_in]
    wbf_ref = refs[n_in + n_side_in + 1 + n_side_out]
    tn = o_ref.shape[1]

    @pl.when(pl.program_id(1) == 0)
    def _():
        rows = _tile(w_ref.shape[0], 128 if transposed else 512)
        for r0 in range(0, w_ref.shape[0], rows):
            rs = slice(r0, r0 + rows)
            if transposed:
                if r0 + rows + shift <= tn:
                    w = w_ref[r0 + shift:r0 + rows + shift, :]
                else:
                    w = jnp.concatenate([w_ref[r0 + shift:, :], refs[2][:shift, :]], axis=0)
                wbf_ref[:, rs] = w.T.astype(_BF16)
            else:
                w = w_ref[rs, :]
                if shift:
                    w = jnp.concatenate([w, refs[2][rs, :]], axis=1)
                    w = pltpu.roll(w, w.shape[1] - shift, axis=1)[:, :tn]
                wbf_ref[rs, :] = w.astype(_BF16)

    def matmul_rows(rs):
        acc = jnp.dot(x_ref[rs, :], wbf_ref[...], preferred_element_type=_F32)
        if has_residual:
            acc = acc + refs[n_in - 1][rs, :].astype(_F32)
        o_ref[rs, :] = acc.astype(o_ref.dtype)

    if side:
        rows = x_ref.shape[0] // side.parts
        parts = [functools.partial(matmul_rows, slice(r * rows, (r + 1) * rows)) for r in range(side.parts)]
        step = pl.program_id(0) * pl.num_programs(1) + pl.program_id(1)
        side.kernel(step, parts, *refs[n_in:n_in + n_side_in],
                    *refs[n_in + n_side_in + 1:n_in + n_side_in + 1 + n_side_out],
                    *refs[n_in + n_side_in + 2 + n_side_out:])
    else:
        matmul_rows(slice(None))


def _matmul_w32(x, w, layer, windows, out_dtype, tm=1024, tn=512, residual=None, side=None,
                transposed=False, weight_buffers=2, name="matmul_w32"):
    m, k = x.shape
    shift = windows[0][0] % LANES
    assert all(c0 % LANES == shift for c0, _ in windows)
    tm = _tile(m, tm)
    ok = lambda t: all(n % t == 0 and (c0 - shift) % t == 0 for c0, n in windows)
    tn = max(t for t in range(LANES, tn + 1, LANES) if ok(t))
    n = sum(n for _, n in windows)

    def col_block(j, width):
        blk, start = 0, 0
        for c0, wn in windows:
            first = ((c0 - shift) // tn - start) * (tn // width)
            blk = jnp.where(j >= start, first + j * (tn // width), blk)
            start += wn // tn
        return blk

    osz = jnp.dtype(out_dtype).itemsize

    def wspec(width, block_of):
        mode = {} if weight_buffers == 2 else dict(pipeline_mode=pl.Buffered(weight_buffers))
        if transposed:
            return pl.BlockSpec((None, width, k), lambda j, i: (layer, block_of(j), 0), **mode)
        return pl.BlockSpec((None, k, width), lambda j, i: (layer, 0, block_of(j)), **mode)

    in_specs = [pl.BlockSpec((tm, k), lambda j, i: (i, 0)), wspec(tn, lambda j: col_block(j, tn))]
    args = [x, w]
    vmem = 2 * (tm * k * 2 + tm * tn * osz) + weight_buffers * k * tn * 4 + k * tn * 2 + tm * tn * 4
    if shift:
        in_specs.append(wspec(LANES, lambda j: col_block(j, LANES) + tn // LANES))
        args.append(w)
        vmem += weight_buffers * k * LANES * 4
    if residual is not None:
        in_specs.append(pl.BlockSpec((tm, tn), lambda j, i: (i, j)))
        args.append(residual)
        vmem += 2 * tm * tn * residual.dtype.itemsize
    out_specs = [pl.BlockSpec((tm, tn), lambda j, i: (i, j))]
    out_shape = [jax.ShapeDtypeStruct((m, n), out_dtype)]
    scratch = [pltpu.VMEM((k, tn), _BF16)]
    alone = None
    if callable(side):
        side = side((n // tn) * (m // tm), m // tm)
    if side:
        side, steps = side
        if steps != (n // tn) * (m // tm):
            alone, side = _run_side(side, steps, m // tm, name + "_side"), None
    if side:
        in_specs += side.in_specs
        args += side.args
        out_specs += side.out_specs
        out_shape += side.out_shape
        scratch += side.scratch_shapes
        vmem += side.vmem
    out = pl.pallas_call(
        functools.partial(_mm_w32_kernel, shift=shift, has_residual=residual is not None, side=side,
                          transposed=transposed),
        grid=(n // tn, m // tm),
        in_specs=in_specs,
        out_specs=out_specs,
        out_shape=out_shape,
        scratch_shapes=scratch,
        compiler_params=_params(("arbitrary", "arbitrary"), vmem),
        name=name,
    )(*args)
    if alone is not None:
        return [out[0], *alone]
    return out if side else out[0]


def _log_sigmoid(z):
    return jnp.minimum(z, 0.0) - jnp.log(1.0 + jnp.exp(-jnp.abs(z)))


def _cumsum_rows(x):
    c, n = x.shape
    row = lax.broadcasted_iota(jnp.int32, (c, c), 0)
    col = lax.broadcasted_iota(jnp.int32, (c, c), 1)
    tri = jnp.where(row >= col, 1.0, 0.0).astype(_BF16)
    hi = x.astype(_BF16)
    rest = x - hi.astype(_F32)
    mid = rest.astype(_BF16)
    lo = (rest - mid.astype(_F32)).astype(_BF16)
    parts = jnp.dot(tri, jnp.concatenate([hi, mid, lo], axis=1), preferred_element_type=_F32)
    return parts[:, :n] + parts[:, n:2 * n] + parts[:, 2 * n:]


def _gla_chunk(q, k, v, b, state, midway):
    c = q.shape[0]
    row = lax.broadcasted_iota(jnp.int32, (c, c), 0)
    col = lax.broadcasted_iota(jnp.int32, (c, c), 1)
    b_last = b[c - 1:c, :]

    attn = jnp.zeros((c, c), _F32)
    row1 = lax.broadcasted_iota(jnp.int32, (c, 1), 0)
    s = c // 2
    while s >= GLA_SUB:
        pieces = []
        for lo in range(0, c, 2 * s):
            mid = lo + s
            ref_row = b[mid - 1:mid, :]
            pieces.append(ref_row - b[lo:mid, :])
            pieces.append(b[mid:mid + s, :] - ref_row)
        x = jnp.exp2(jnp.concatenate(pieces, axis=0))
        upper = ((row1 >> _log2(s)) & 1) == 1
        qt = jnp.where(upper, q * x, 0.0).astype(_BF16)
        kt = jnp.where(upper, 0.0, k * x).astype(_BF16)
        same = (row >> _log2(2 * s)) == (col >> _log2(2 * s))
        attn = attn + jnp.where(same, _nt(qt, kt), 0.0)
        s //= 2

    midway()
    sub_row = lax.broadcasted_iota(jnp.int32, (GLA_SUB, 1), 0)
    lane = lax.broadcasted_iota(jnp.int32, (GLA_SUB, c), 1)
    diag = []
    for i0 in range(0, c, GLA_SUB):
        qi, ki, bi = q[i0:i0 + GLA_SUB, :], k[i0:i0 + GLA_SUB, :], b[i0:i0 + GLA_SUB, :]
        blk = jnp.zeros((GLA_SUB, c), _F32)
        for j in range(GLA_SUB):
            dec = jnp.exp2(jnp.where(sub_row >= j, bi - bi[j:j + 1, :], _NEG_INF))
            sc = jnp.sum(qi * dec * ki[j:j + 1, :], axis=1, keepdims=True)
            blk = jnp.where(lane == i0 + j, sc, blk)
        diag.append(blk)
    attn = attn + jnp.concatenate(diag, axis=0)

    qb = (q * jnp.exp2(b)).astype(_BF16)
    o = _nt(qb, state.astype(_BF16)) + jnp.dot(attn.astype(_BF16), v, preferred_element_type=_F32)
    kd = (k * jnp.exp2(b_last - b)).astype(_BF16)
    return o, state * jnp.exp2(b_last) + _tn(v, kd)


def _gla_kernel(step, parts, q_ref, k_ref, v_ref, r_ref, glr_ref, wg_ref, bias_ref, gain_ref, o_ref,
                state_ref, *, nc):
    @pl.when(step % nc == 0)
    def _():
        state_ref[...] = jnp.zeros_like(state_ref)

    c = q_ref.shape[0] // GLA_STEP_CHUNKS
    z = jnp.dot(glr_ref[...], wg_ref[...], preferred_element_type=_F32) + bias_ref[...]
    log_a = _log_sigmoid(z) * (_LOG2_E / GLA_TAU)
    b_all = [_cumsum_rows(log_a[s * c:(s + 1) * c, :]) for s in range(GLA_STEP_CHUNKS)]
    states = [state_ref[h] for h in range(GLA_STEP_HEADS)]
    parts = iter(parts)
    for s in range(GLA_STEP_CHUNKS):
        rows = slice(s * c, (s + 1) * c)
        for h in range(GLA_STEP_HEADS):
            next(parts)()
            ks = slice(h * GLA_DK, (h + 1) * GLA_DK)
            vs = slice(h * GLA_DV, (h + 1) * GLA_DV)
            q = q_ref[rows, ks].astype(_F32) * (GLA_DK ** -0.5)
            o, states[h] = _gla_chunk(q, k_ref[rows, ks].astype(_F32), v_ref[rows, vs], b_all[s][:, ks],
                                      states[h], next(parts))
            ms = jnp.mean(o * o, axis=-1, keepdims=True)
            on = o * lax.rsqrt(ms + RMS_EPS) * gain_ref[:, vs]
            r = r_ref[rows, vs].astype(_F32)
            o_ref[rows, vs] = (on * (r * jax.nn.sigmoid(r))).astype(o_ref.dtype)
    for h in range(GLA_STEP_HEADS):
        state_ref[h] = states[h]


def _run_side(side, steps, ni, name):
    def kern(*refs):
        side.kernel(pl.program_id(0) * pl.num_programs(1) + pl.program_id(1),
                    [lambda: None] * side.parts, *refs)

    return pl.pallas_call(
        kern, grid=(steps // ni, ni), in_specs=side.in_specs, out_specs=side.out_specs,
        out_shape=side.out_shape, scratch_shapes=side.scratch_shapes,
        compiler_params=_params(("arbitrary", "arbitrary"), side.vmem), name=name)(*side.args)


def _gla_side(qkvr, glr, wg, bias, gain, batch, seq, ni, chunk=128):
    t = batch * seq
    chunk = chunk * GLA_STEP_CHUNKS
    nc = seq // chunk
    hs = GLA_STEP_HEADS
    wk, wv = hs * GLA_DK, hs * GLA_DV
    groups = GLA_HEADS // hs
    kv = 2 * groups * wk // wv

    def at(col):
        def index(j, i):
            s = j * ni + i
            return (s // (groups * nc)) * nc + s % nc, col + (s // nc) % groups
        return index

    head = lambda j, i: (0, ((j * ni + i) // nc) % groups)
    in_specs = [
        pl.BlockSpec((chunk, wk), at(0)),
        pl.BlockSpec((chunk, wk), at(groups)),
        pl.BlockSpec((chunk, wv), at(kv)),
        pl.BlockSpec((chunk, wv), at(kv + groups)),
        pl.BlockSpec((chunk, LANES), lambda j, i: (at(0)(j, i)[0], 0)),
        pl.BlockSpec((LANES, wk), head),
        pl.BlockSpec((1, wk), head),
        pl.BlockSpec((1, wv), head),
    ]
    side = _Side(functools.partial(_gla_kernel, nc=nc), 2 * hs * GLA_STEP_CHUNKS,
                 [qkvr, qkvr, qkvr, qkvr, glr, wg, bias, gain], in_specs,
                 [pl.BlockSpec((chunk, wv), at(0))], [jax.ShapeDtypeStruct((t, GLA_HEADS * GLA_DV), _BF16)],
                 [pltpu.VMEM((hs, GLA_DV, GLA_DK), _F32)], vmem=16 << 20)
    return side, batch * groups * nc


def _pair_rmsnorm(x, gain2):
    lane = lax.broadcasted_iota(jnp.int32, x.shape, 1)
    low = lane < SWA_HD
    sq = x * x
    s_all = jnp.sum(sq, axis=1, keepdims=True)
    s_low = jnp.sum(jnp.where(low, sq, 0.0), axis=1, keepdims=True)
    ms = jnp.where(low, s_low, s_all - s_low) * (1.0 / SWA_HD)
    return x * lax.rsqrt(ms + RMS_EPS) * gain2


def _swa_kernel(step, parts, sinks_ref, q_ref, kc_ref, kp_ref, vc_ref, vp_ref, qg_ref, kg_ref, o_ref, *, nb, spp):
    pair_groups = SWA_KV_HEADS // 2 // spp
    blk = (step // pair_groups) % nb
    bs = SWA_BLOCK
    group = SWA_HEADS // SWA_KV_HEADS
    pairs = group // 2
    pw = 2 * group * SWA_HD
    lane = lax.broadcasted_iota(jnp.int32, (bs, LANES), 1)
    low = lane < SWA_HD

    q_loc = lax.broadcasted_iota(jnp.int32, (group * bs, bs), 0) & (bs - 1)
    from_cur = lax.broadcasted_iota(jnp.int32, (group * bs, bs), 1) <= q_loc
    prev_bias = jnp.where(blk > 0, 0.0, _NEG_INF)

    for pp in range(spp):
        p = (step % pair_groups) * spp + pp
        ls = slice(pp * LANES, (pp + 1) * LANES)
        tiles = [_pair_rmsnorm(kc_ref[:, ls].astype(_F32), kg_ref[...]),
                 _pair_rmsnorm(kp_ref[:, ls].astype(_F32), kg_ref[...]),
                 vc_ref[:, ls].astype(_F32), vp_ref[:, ls].astype(_F32)]
        swapped = [pltpu.roll(x, SWA_HD, axis=1) for x in tiles]
        for half in range(2):
            kc, kp, vc, vp = [(jnp.where(low, x, xs) if half == 0 else jnp.where(low, xs, x)).astype(_BF16)
                              for x, xs in zip(tiles, swapped)]
            qs, sink = [], []
            for t in range(pairs):
                c0 = pp * pw + half * group * SWA_HD + t * LANES
                qn = _pair_rmsnorm(q_ref[:, c0:c0 + LANES].astype(_F32), qg_ref[...]) * (SWA_HD ** -0.5)
                qs.append(jnp.where(low, qn, 0.0))
                qs.append(jnp.where(low, 0.0, qn))
                head = (2 * p + half) * group + 2 * t
                sink.append(jnp.full((bs, 1), sinks_ref[head], _F32))
                sink.append(jnp.full((bs, 1), sinks_ref[head + 1], _F32))
            qstack = jnp.concatenate(qs, axis=0).astype(_BF16)
            sink = jnp.concatenate(sink, axis=0)
            s = jnp.where(from_cur, _nt(qstack, kc), _nt(qstack, kp) + prev_bias)
            m = jnp.maximum(jnp.max(s, axis=1, keepdims=True), sink)
            e = jnp.exp(s - m)
            denom = jnp.sum(e, axis=1, keepdims=True) + jnp.exp(sink - m)
            e_cur = jnp.where(from_cur, e, 0.0).astype(_BF16)
            e_prev = jnp.where(from_cur, 0.0, e).astype(_BF16)
            o = (jnp.dot(e_cur, vc, preferred_element_type=_F32)
                 + jnp.dot(e_prev, vp, preferred_element_type=_F32)) / denom
            for t in range(pairs):
                c0 = pp * pw + half * group * SWA_HD + t * LANES
                oa = o[(2 * t) * bs:(2 * t + 1) * bs, :]
                ob = o[(2 * t + 1) * bs:(2 * t + 2) * bs, :]
                o_ref[:, c0:c0 + LANES] = jnp.where(low, oa, ob).astype(o_ref.dtype)
            parts[2 * pp + half]()


def _swa_side(sq, q_col, skv, k_col, sinks, q_gain, k_gain, batch, seq, ni):
    t = batch * seq
    nb = seq // SWA_BLOCK
    npairs = SWA_KV_HEADS // 2
    spp = min(SWA_STEP_PAIRS, npairs)
    pair_groups = npairs // spp
    kvw = SWA_KV_HEADS * SWA_HD
    qw, kw = spp * 2 * (SWA_HEADS // SWA_KV_HEADS) * SWA_HD, spp * LANES
    assert q_col % qw == 0 and k_col % kw == 0 and kvw % kw == 0

    def at(col, prev=False):
        def index(j, i):
            s = j * ni + i
            blk = s // pair_groups
            if prev:
                blk = blk - jnp.where(blk % nb > 0, 1, 0)
            return blk, col + s % pair_groups
        return index

    const = lambda j, i: (0, 0)
    kc, vc = k_col // kw, (k_col + kvw) // kw
    in_specs = [
        pl.BlockSpec(memory_space=pltpu.SMEM),
        pl.BlockSpec((SWA_BLOCK, qw), at(q_col // qw)),
        pl.BlockSpec((SWA_BLOCK, kw), at(kc)),
        pl.BlockSpec((SWA_BLOCK, kw), at(kc, prev=True)),
        pl.BlockSpec((SWA_BLOCK, kw), at(vc)),
        pl.BlockSpec((SWA_BLOCK, kw), at(vc, prev=True)),
        pl.BlockSpec((1, LANES), const),
        pl.BlockSpec((1, LANES), const),
    ]
    g2 = lambda g: jnp.concatenate([g, g]).reshape(1, LANES).astype(_F32)
    side = _Side(functools.partial(_swa_kernel, nb=nb, spp=spp), 2 * spp,
                 [sinks.astype(_F32), sq, skv, skv, skv, skv, g2(q_gain), g2(k_gain)], in_specs,
                 [pl.BlockSpec((SWA_BLOCK, qw), at(0))],
                 [jax.ShapeDtypeStruct((t, SWA_HEADS * SWA_HD), _BF16)], vmem=16 << 20)
    return side, batch * nb * pair_groups


def _memattn_kernel(q_ref, k_ref, v_ref, g_ref, o_ref):
    hd = q_ref.shape[1]
    q = q_ref[...].astype(_F32)
    ms = jnp.mean(q * q, axis=-1, keepdims=True)
    qn = (q * lax.rsqrt(ms + RMS_EPS) * g_ref[...] * (hd ** -0.5)).astype(_BF16)
    s = _nt(qn, k_ref[...])
    m = jnp.max(s, axis=-1, keepdims=True)
    e = jnp.exp(s - m)
    denom = jnp.sum(e, axis=-1, keepdims=True)
    o = jnp.dot(e.astype(_BF16), v_ref[...], preferred_element_type=_F32) / denom
    o_ref[...] = o.astype(o_ref.dtype)


def _memattn(mq, kc, vm, q_gain, batch, seq, mem_len, d, q_blk=0, v_blk=0, tm=1024):
    t = mq.shape[0]
    tm = _tile(seq, tm)
    ns = seq // tm
    hd = d // MEM_HEADS
    return pl.pallas_call(
        _memattn_kernel,
        grid=(batch, ns, MEM_HEADS),
        in_specs=[
            pl.BlockSpec((tm, hd), lambda b, i, h: (b * ns + i, q_blk + h)),
            pl.BlockSpec((mem_len, hd), lambda b, i, h: (b, h)),
            pl.BlockSpec((mem_len, hd), lambda b, i, h: (b, v_blk + h)),
            pl.BlockSpec((1, hd), lambda b, i, h: (0, 0)),
        ],
        out_specs=pl.BlockSpec((tm, hd), lambda b, i, h: (b * ns + i, h)),
        out_shape=jax.ShapeDtypeStruct((t, d), _BF16),
        compiler_params=_params(("parallel", "parallel", "arbitrary"),
                                4 * tm * hd * 2 + 4 * mem_len * hd * 2 + 6 * tm * hd * 4),
        name="memattn",
    )(mq, kc, vm, q_gain.reshape(1, hd).astype(_F32))


def _merge_kernel(a_ref, b_ref, c_ref, wa_ref, wb_ref, wc_ref, ga_ref, gb_ref, gc_ref, *rest):
    def branch(x_ref, w_ref, g_ref):
        y = jnp.dot(x_ref[...], w_ref[...], preferred_element_type=_F32)
        return jax.nn.sigmoid(g_ref[...].astype(_F32)) * y

    ncast = (len(rest) - 1) // 2
    o_ref = rest[ncast]
    o_ref[...] = (branch(a_ref, wa_ref, ga_ref) + branch(b_ref, wb_ref, gb_ref)
                  + branch(c_ref, wc_ref, gc_ref)).astype(o_ref.dtype)
    _cast_kernel(None, [lambda: None], *rest[:ncast], *rest[ncast + 1:])


def _merge(oa, ob, oc, wa, wb, wc, gate, gate_col=0, cast=(), tm=512, tn=512):
    t, d = oa.shape
    n = wa.shape[1]
    tm, tn = _tile(t, tm), _tile(n, tn)
    nn = n // tn
    ni = t // tm
    if gate_col % tn:
        gate, gate_col = gate[:, gate_col:gate_col + 3 * n], 0
    g0 = gate_col // tn
    act = lambda o: pl.BlockSpec((tm, o.shape[1]), lambda j, i: (i, 0))
    wsp = lambda w: pl.BlockSpec((w.shape[0], tn), lambda j, i: (0, j), pipeline_mode=pl.Buffered(1))
    gsp = lambda b: pl.BlockSpec((tm, tn), lambda j, i: (i, g0 + b * nn + j))
    casts, _ = _cast_side(cast, nn * ni, ni)
    vmem = 3 * (2 * tm * d * 2 + d * tn * 2 + 2 * tm * tn * 2) + 8 * tm * tn * 4 + casts.vmem
    out = pl.pallas_call(
        _merge_kernel,
        grid=(nn, ni),
        in_specs=[act(oa), act(ob), act(oc), wsp(wa), wsp(wb), wsp(wc), gsp(0), gsp(1), gsp(2)] + casts.in_specs,
        out_specs=[pl.BlockSpec((tm, tn), lambda j, i: (i, j))] + casts.out_specs,
        out_shape=[jax.ShapeDtypeStruct((t, n), _BF16)] + casts.out_shape,
        compiler_params=_params(("arbitrary", "arbitrary"), vmem),
        name="merge",
    )(oa, ob, oc, wa, wb, wc, gate, gate, gate, *casts.args)
    return out


def _extract_topk(s, payload, kk):
    n = s.shape[0]
    pos = lax.broadcasted_iota(jnp.int32, s.shape, 0).astype(_F32)
    vals, sel = [], []
    for r in range(kk):
        m = jnp.max(s, axis=0, keepdims=True)
        first = jnp.min(jnp.where(s == m, pos, float(n)), axis=0, keepdims=True)
        hit = pos == first
        vals.append(m)
        sel.append(first if payload is None else jnp.max(jnp.where(hit, payload, -1.0), axis=0, keepdims=True))
        if r + 1 < kk:
            s = jnp.where(hit, _NEG_INF, s)
    return jnp.concatenate(vals, axis=0), jnp.concatenate(sel, axis=0)


def _route_kernel(h_ref, gain_ref, wq_ref, keys_ref, hn_ref, w_ref, q_ref, i1_ref, i2_ref, g_ref):
    kk = PEER_TOPK
    half = PEER_DQ // 2
    nk = PEER_NKEYS
    step = pl.program_id(0)
    cur, prev = step % 2, (step + 1) % 2
    tokens = h_ref.shape[0] // PEER_HEADS

    @pl.when(step == 0)
    def _():
        q_ref[1] = jnp.zeros(q_ref.shape[1:], q_ref.dtype)
        i1_ref[...] = jnp.zeros_like(i1_ref)
        i2_ref[...] = jnp.zeros_like(i2_ref)
        g_ref[...] = jnp.zeros_like(g_ref)

    x = h_ref[...]
    ms = jnp.mean(x * x, axis=-1, keepdims=True)
    hn = (x * lax.rsqrt(ms + RMS_EPS) * gain_ref[...]).astype(_BF16)
    hn_ref[...] = hn

    sub = lax.broadcasted_iota(jnp.int32, (nk, i1_ref.shape[2]), 0)

    def gate_rows(t0):
        for t in range(t0, t0 + tokens):
            a_t = jnp.where(sub == i1_ref[cur, t:t + 1, :], 1.0, 0.0).astype(_BF16)
            b_t = jnp.where(sub == i2_ref[cur, t:t + 1, :], g_ref[cur, t:t + 1, :], 0.0).astype(_BF16)
            w_ref[t] = _nt(a_t, b_t)

    experts, gates = [], []
    for h in range(PEER_HEADS):
        hs = slice(h * PEER_DQ, (h + 1) * PEER_DQ)
        q_ref[cur, :, hs] = jnp.dot(hn, wq_ref[:, hs], preferred_element_type=_F32)
        top = []
        for p in range(2):
            c0 = (2 * h + p) * half
            st = _nt(keys_ref[2 * h + p].astype(_BF16), q_ref[prev, :, c0:c0 + half].astype(_BF16))
            top.append(_extract_topk(st, None, kk))
        (s0, i0), (s1, i1) = top
        cand, ids = [], []
        tail = kk // 2
        for a in range(tail):
            nb = kk // (a + 1)
            rows = -(-nb // 8) * 8
            c = s0[a:a + 1, :] + s1[0:rows, :]
            if rows != nb:
                c = jnp.where(lax.broadcasted_iota(jnp.int32, c.shape, 0) < nb, c, _NEG_INF)
            cand.append(c)
            ids.append(i0[a:a + 1, :] * PEER_NKEYS + i1[0:rows, :])
        cand.append(s0[tail:kk, :] + s1[0:1, :])
        ids.append(i0[tail:kk, :] * PEER_NKEYS + i1[0:1, :])
        best, e = _extract_topk(jnp.concatenate(cand, axis=0), jnp.concatenate(ids, axis=0), kk)
        ex = jnp.exp(best - best[0:1, :])
        gates.append(ex / jnp.sum(ex, axis=0, keepdims=True))
        experts.append(e)
        gate_rows(h * tokens)
    e = jnp.concatenate(experts, axis=0).T.astype(jnp.int32)
    i1_ref[prev] = e >> _log2(PEER_NKEYS)
    i2_ref[prev] = e & (PEER_NKEYS - 1)
    g_ref[prev] = jnp.concatenate(gates, axis=0).T


def _route(h, gain, wq, sub_keys, tt=128):
    t, d = h.shape
    tt = _tile(t, tt)
    n = t // tt
    j = PEER_HEADS * PEER_TOPK
    nk = PEER_NKEYS
    nq = PEER_HEADS * PEER_DQ
    keys = sub_keys.reshape(PEER_HEADS * 2, nk, PEER_DQ // 2)
    tile = lambda s: (jnp.minimum(s, n - 1), 0)
    vmem = d * nq * 2 + 2 * (tt * nk * nk * 4 + tt * d * 6) + 2 * tt * nq * 4 + (12 << 20)
    return pl.pallas_call(
        _route_kernel,
        grid=(n + 2,),
        in_specs=[pl.BlockSpec((tt, d), tile),
                  pl.BlockSpec((1, d), lambda s: (0, 0)),
                  pl.BlockSpec((d, nq), lambda s: (0, 0), pipeline_mode=pl.Buffered(1)),
                  pl.BlockSpec(keys.shape, lambda s: (0, 0, 0))],
        out_specs=[pl.BlockSpec((tt, d), tile),
                   pl.BlockSpec((tt, nk, nk), lambda s: (jnp.maximum(s - 2, 0), 0, 0))],
        out_shape=[jax.ShapeDtypeStruct((t, d), _BF16), jax.ShapeDtypeStruct((t, nk, nk), _F32)],
        scratch_shapes=[pltpu.VMEM((2, tt, nq), _F32), pltpu.VMEM((2, tt, j), jnp.int32),
                        pltpu.VMEM((2, tt, j), jnp.int32), pltpu.VMEM((2, tt, j), _F32)],
        compiler_params=_params(("arbitrary",), vmem),
        name="peer_route",
    )(h, gain.reshape(1, d).astype(_F32), wq, keys)


def _gelu(x):
    return 0.5 * x * (1.0 + lax.erf(x * (2.0 ** -0.5)))


def _peer_act_kernel(x_ref, u_ref, w_ref, s_ref):
    act = _gelu(_nt(x_ref[...], u_ref[...]))
    w = jnp.transpose(w_ref[...], (1, 0, 2))
    for a in range(w.shape[0]):
        cs = slice(a * PEER_NKEYS, (a + 1) * PEER_NKEYS)
        s_ref[:, cs] = (act[:, cs] * w[a]).astype(s_ref.dtype)


def _peer_act(xn, u, w, tm=1024, rows_a=8):
    t, d = xn.shape
    ne = u.shape[0]
    nk = PEER_NKEYS
    tm = _tile(t, tm)
    eb = rows_a * nk
    vmem = 2 * (tm * d * 2 + tm * rows_a * nk * 4 + tm * eb * 2) + eb * d * 2 + 3 * tm * eb * 4
    return pl.pallas_call(
        _peer_act_kernel,
        grid=(ne // eb, t // tm),
        in_specs=[pl.BlockSpec((tm, d), lambda j, i: (i, 0)),
                  pl.BlockSpec((eb, d), lambda j, i: (j, 0), pipeline_mode=pl.Buffered(1)),
                  pl.BlockSpec((tm, rows_a, nk), lambda j, i: (i, j, 0))],
        out_specs=pl.BlockSpec((tm, eb), lambda j, i: (i, j)),
        out_shape=jax.ShapeDtypeStruct((t, ne), _BF16),
        compiler_params=_params(("parallel", "parallel"), vmem),
        name="peer_act",
    )(xn, u, w)


def _mm_acc_kernel(x_ref, w_ref, r_ref, o_ref):
    acc = jnp.dot(x_ref[...], w_ref[...], preferred_element_type=_F32)

    @pl.when(pl.program_id(2) == 0)
    def _():
        o_ref[...] = r_ref[...] + acc

    @pl.when(pl.program_id(2) > 0)
    def _():
        o_ref[...] += acc


def _matmul_acc(x, w, residual, tm=1024, tn=1024, tk=4096):
    m, k = x.shape
    n = w.shape[1]
    tm, tn, tk = _tile(m, tm), _tile(n, tn), _tile(k, tk)
    vmem = 2 * (tm * tk * 2 + tk * tn * 2 + 2 * tm * tn * 4) + tm * tn * 4
    return pl.pallas_call(
        _mm_acc_kernel,
        grid=(m // tm, n // tn, k // tk),
        in_specs=[pl.BlockSpec((tm, tk), lambda i, j, l: (i, l)),
                  pl.BlockSpec((tk, tn), lambda i, j, l: (l, j)),
                  pl.BlockSpec((tm, tn), lambda i, j, l: (i, j))],
        out_specs=pl.BlockSpec((tm, tn), lambda i, j, l: (i, j)),
        out_shape=jax.ShapeDtypeStruct((m, n), _F32),
        compiler_params=_params(("parallel", "parallel", "arbitrary"), vmem),
        name="peer_out",
    )(x, w, residual)


def _layer(h, mem, norm_mix, w_in, gla_w_gate_up, gla_gate_bias, gla_out_norm, swa_q_norm, swa_k_norm,
           swa_sinks, mem_norm, w_mem_kv, mem_q_norm, mem_k_norm, w_branch_gla, w_branch_swa,
           w_branch_mem, w_out, norm_ffn, peer_w_q, peer_sub_keys, peer_u, peer_v, batch, seq, l):
    t, d = h.shape
    mem_len = mem.shape[0] // batch
    qk = GLA_HEADS * GLA_DK
    gv = GLA_HEADS * GLA_DV
    swq = SWA_HEADS * SWA_HD
    skv = SWA_KV_HEADS * SWA_HD
    o_glr = 2 * qk + 2 * gv
    o_sq = o_glr + GLA_RANK
    o_sk = o_sq + swq
    o_mq = o_sk + 2 * skv
    o_gate = o_mq + d
    bf = lambda w: w.astype(_BF16)
    hd = d // MEM_HEADS

    w_in_t = jnp.swapaxes(w_in, 1, 2)
    xn, glr, skvp = _norm_proj(h, norm_mix, w_in_t, l, o_glr, o_sk, 2 * skv)
    proj = functools.partial(_matmul_w32, xn, w_in_t, l, transposed=True)
    branch_w = [w_branch_gla, w_branch_swa, w_branch_mem]
    qkvr, *branch_w = proj([(0, o_glr)], _BF16, side=functools.partial(_cast_side, branch_w), name="proj_gla")

    tm_gla = _tile(t, 512 * GLA_STEP_CHUNKS)
    tm_swa = _tile(t, 512 * min(SWA_STEP_PAIRS, SWA_KV_HEADS // 2))
    wg = bf(jnp.pad(gla_w_gate_up, ((0, LANES - GLA_RANK), (0, 0))))
    gla = _gla_side(qkvr, glr, wg, gla_gate_bias.reshape(1, qk).astype(_F32),
                    gla_out_norm.reshape(1, gv).astype(_F32), batch, seq, t // tm_gla)
    sqmq, o_a = proj([(o_sq, swq), (o_mq, d)], _BF16, tm=tm_gla, side=gla, name="proj_q_gla")
    swa = _swa_side(sqmq, 0, skvp, 0, swa_sinks, swa_q_norm, swa_k_norm, batch, seq, t // tm_swa)
    gate, o_b = proj([(o_gate, 3 * d)], _BF16, tm=tm_swa, tn=768, side=swa, weight_buffers=1,
                     name="proj_gate_swa")

    memn = _rmsnorm(mem, mem_norm, _BF16)
    kv_m = _matmul_w32(memn, w_mem_kv, l, [(0, 2 * d)], _BF16, name="mem_kv")
    k_c = _rmsnorm(kv_m[:, :d].reshape(-1, hd), mem_k_norm, _BF16).reshape(-1, d)
    mq, mq_blk = (sqmq, swq // hd) if swq % hd == 0 else (sqmq[:, swq:], 0)
    o_c = _memattn(mq, k_c, kv_m, mem_q_norm, batch, seq, mem_len, d, mq_blk, MEM_HEADS)

    mix, wq, u, v = _merge(o_a, o_b, o_c, *branch_w, gate, cast=[peer_w_q[l], peer_u, peer_v])
    h = _matmul_w32(mix, w_out, l, [(0, d)], _F32, residual=h, name="out_proj")

    hn, w = _route(h, norm_ffn, wq, peer_sub_keys.astype(_F32))
    s = _peer_act(hn, u, w)
    return _matmul_acc(s, v, h)


def kernel(x, mem, norm_mix, w_in, gla_w_gate_up, gla_gate_bias, gla_out_norm, swa_q_norm, swa_k_norm,
           swa_sinks, mem_norm, w_mem_kv, mem_q_norm, mem_k_norm, w_branch_gla, w_branch_swa, w_branch_mem,
           w_out, norm_ffn, peer_w_q, peer_sub_keys, peer_u, peer_v):
    batch, seq, d = x.shape
    h = x.reshape(batch * seq, d)
    memf = mem.reshape(-1, d)
    for l in range(norm_mix.shape[0]):
        h = _layer(h, memf, norm_mix[l], w_in, gla_w_gate_up[l], gla_gate_bias[l], gla_out_norm[l],
                   swa_q_norm[l], swa_k_norm[l], swa_sinks[l], mem_norm[l], w_mem_kv, mem_q_norm[l],
                   mem_k_norm[l], w_branch_gla[l], w_branch_swa[l], w_branch_mem[l], w_out, norm_ffn[l],
                   peer_w_q, peer_sub_keys[l], peer_u[l], peer_v[l], batch, seq, l)
    return h.reshape(batch, seq, d)
```
